```python
import math, functools
import jax, jax.numpy as jnp
from jax import lax
import numpy as np

D_MODEL = 2048
BATCH = 8
SEQ = 2048
DEPTH = 2
DEC_BATCH = 8
DEC_SEQ = 64
PAST_LEN = 2048

CHUNK = 64
D_FF = 3 * D_MODEL
D_RNN = D_MODEL
N_RG_HEADS = 16
RG_BLOCK = D_RNN // N_RG_HEADS
CONV_W = 4
LRU_C = 8.0
D_POOL = D_MODEL // 2
POOL_WINDOWS = (2, 4, 8, 16)
N_POOL_GROUPS = 4
POOL_GROUP = D_POOL // N_POOL_GROUPS
POOL_MAX = 16
IN_COLS = 2 * D_RNN + D_POOL + 2 * D_MODEL
EPS = 1e-6

kernel_name = "hawk_pool_macaron_stream_step"


def rms_norm(x, g):
    xf = x.astype(jnp.float32)
    y = xf * lax.rsqrt(jnp.mean(xf * xf, axis=-1, keepdims=True) + EPS)
    return (y * g.astype(jnp.float32)).astype(x.dtype)


def swiglu(x, w_in, w_out):
    gate, up = jnp.split(x @ w_in, 2, axis=-1)
    return (jax.nn.silu(gate) * up) @ w_out


def causal_conv(x, prefix, w, b):
    T = x.shape[1]
    z = jnp.concatenate([prefix.astype(x.dtype), x], axis=1)
    y = sum(z[:, k:k + T] * w[k] for k in range(CONV_W)) + b
    return y, z[:, -(CONV_W - 1):]


def rg_lru(x, h0, w_a, b_a, w_x, b_x, lam):
    B, T, C = x.shape
    xh = x.reshape(B, T, N_RG_HEADS, RG_BLOCK)
    r = jax.nn.sigmoid((jnp.einsum('btnc,ncd->btnd', xh, w_a).reshape(B, T, C) + b_a).astype(jnp.float32))
    i = jax.nn.sigmoid((jnp.einsum('btnc,ncd->btnd', xh, w_x).reshape(B, T, C) + b_x).astype(jnp.float32))
    log_a = -LRU_C * r * jax.nn.softplus(-lam.astype(jnp.float32))
    a = jnp.exp(log_a)
    u = jnp.sqrt(-jnp.expm1(2.0 * log_a)) * (i * x.astype(jnp.float32))

    def step(h, au):
        a_t, u_t = au
        h = a_t * h + u_t
        return h, h

    h_last, hs = lax.scan(step, h0.astype(jnp.float32), (jnp.swapaxes(a, 0, 1), jnp.swapaxes(u, 0, 1)))
    return jnp.swapaxes(hs, 0, 1).astype(x.dtype), h_last


def multiscale_pool(x, prefix, n_valid_prefix):
    B, T, _ = x.shape
    P = POOL_MAX - 1
    z = jnp.concatenate([prefix.astype(x.dtype), x], axis=1)
    zf = z.astype(jnp.float32)
    c = jnp.concatenate([jnp.zeros((B, 1, D_POOL), jnp.float32), jnp.cumsum(zf, axis=1)], axis=1)
    end = c[:, P + 1:]
    pos = jnp.arange(T)
    outs = []
    for g, w in enumerate(POOL_WINDOWS):
        sl = slice(g * POOL_GROUP, (g + 1) * POOL_GROUP)
        start = c[:, P + 1 - w:P + 1 - w + T, sl]
        cnt = jnp.minimum(pos + 1 + n_valid_prefix, w).astype(jnp.float32)[None, :, None]
        outs.append((end[..., sl] - start) / cnt)
    mean = jnp.concatenate(outs, axis=-1)
    return (mean - zf[:, P:]).astype(x.dtype), z[:, -P:]


def layer(x, conv_prefix, h0, pool_prefix, n_valid_pool, gains, w_ffn_in, w_ffn_out, w_in, conv_w, conv_b,
          w_rg_a, b_rg_a, w_rg_x, b_rg_x, lru_param, w_pool_mix, pool_scale, w_br_rg, w_br_pool, w_out):
    B, T, _ = x.shape
    x = x + 0.5 * rms_norm(swiglu(rms_norm(x, gains[0]), w_ffn_in[0], w_ffn_out[0]), gains[1])
    h = rms_norm(x, gains[2])
    proj = h @ w_in
    o1 = D_RNN
    o2 = o1 + D_RNN
    o3 = o2 + D_POOL
    o4 = o3 + D_MODEL
    x_rg, y_gelu, x_pool, g_rg, g_pool = proj[..., :o1], proj[..., o1:o2], proj[..., o2:o3], proj[..., o3:o4], proj[..., o4:]
    xc, conv_state = causal_conv(x_rg, conv_prefix, conv_w, conv_b)
    rec, h_last = rg_lru(xc, h0, w_rg_a, b_rg_a, w_rg_x, b_rg_x, lru_param)
    branch_rg = (rec * jax.nn.gelu(y_gelu)) @ w_br_rg
    pooled, pool_state = multiscale_pool(x_pool, pool_prefix, n_valid_pool)
    pooled = jnp.einsum('btgc,gcd->btgd', pooled.reshape(B, T, N_POOL_GROUPS, POOL_GROUP), w_pool_mix).reshape(B, T, D_POOL)
    branch_pool = (pooled * pool_scale) @ w_br_pool
    mix = (jax.nn.sigmoid(g_rg) * branch_rg + jax.nn.sigmoid(g_pool) * branch_pool) @ w_out
    x = x + rms_norm(mix, gains[3])
    x = x + 0.5 * rms_norm(swiglu(rms_norm(x, gains[4]), w_ffn_in[1], w_ffn_out[1]), gains[5])
    return x, conv_state, h_last, pool_state


def setup_inputs(seed: int = 0) -> dict:
    key = jax.random.key(seed)
    ks = jax.random.split(key, 24)

    def nrm(k, shape, scale):
        return jax.random.normal(k, shape, jnp.float32) * scale

    u = jax.random.uniform(ks[12], (DEPTH, D_RNN), jnp.float32, minval=0.9, maxval=0.999)
    return {
        "x_prompt": nrm(ks[0], (BATCH, SEQ, D_MODEL), 1.0),
        "x_sample": nrm(ks[1], (DEC_BATCH, DEC_SEQ, D_MODEL), 1.0),
        "state_conv": nrm(ks[2], (DEPTH, DEC_BATCH, CONV_W - 1, D_RNN), 1.0),
        "state_h": nrm(ks[3], (DEPTH, DEC_BATCH, D_RNN), 0.5),
        "state_pool": nrm(ks[4], (DEPTH, DEC_BATCH, POOL_MAX - 1, D_POOL), 1.0),
        "norm_gains": 1.0 + nrm(ks[5], (DEPTH, 6, D_MODEL), 0.05),
        "w_ffn_in": nrm(ks[6], (DEPTH, 2, D_MODEL, 2 * D_FF), D_MODEL ** -0.5),
        "w_ffn_out": nrm(ks[7], (DEPTH, 2, D_FF, D_MODEL), D_FF ** -0.5),
        "w_in": nrm(ks[8], (DEPTH, D_MODEL, IN_COLS), D_MODEL ** -0.5),
        "conv_w": nrm(ks[9], (DEPTH, CONV_W, D_RNN), CONV_W ** -0.5),
        "conv_b": nrm(ks[10], (DEPTH, D_RNN), 0.01),
        "w_rg_a": nrm(ks[11], (DEPTH, N_RG_HEADS, RG_BLOCK, RG_BLOCK), RG_BLOCK ** -0.5),
        "b_rg_a": nrm(ks[13], (DEPTH, D_RNN), 0.01),
        "w_rg_x": nrm(ks[14], (DEPTH, N_RG_HEADS, RG_BLOCK, RG_BLOCK), RG_BLOCK ** -0.5),
        "b_rg_x": nrm(ks[15], (DEPTH, D_RNN), 0.01),
        "lru_param": jnp.log(u) - jnp.log1p(-u),
        "w_pool_mix": nrm(ks[16], (DEPTH, N_POOL_GROUPS, POOL_GROUP, POOL_GROUP), POOL_GROUP ** -0.5),
        "pool_scale": 1.0 + nrm(ks[17], (DEPTH, D_POOL), 0.05),
        "w_br_rg": nrm(ks[18], (DEPTH, D_RNN, D_MODEL), D_RNN ** -0.5),
        "w_br_pool": nrm(ks[19], (DEPTH, D_POOL, D_MODEL), D_POOL ** -0.5),
        "w_out": nrm(ks[20], (DEPTH, D_MODEL, D_MODEL), D_MODEL ** -0.5),
    }


def reference(x_prompt, x_sample, state_conv, state_h, state_pool, norm_gains, w_ffn_in, w_ffn_out, w_in,
              conv_w, conv_b, w_rg_a, b_rg_a, w_rg_x, b_rg_x, lru_param, w_pool_mix, pool_scale,
              w_br_rg, w_br_pool, w_out):
    Bp = x_prompt.shape[0]
    conv_p, h_p, pool_p = [], [], []
    conv_s, h_s, pool_s = [], [], []
    yp = x_prompt
    ys = x_sample
    for l in range(DEPTH):
        params = (norm_gains[l], w_ffn_in[l], w_ffn_out[l], w_in[l], conv_w[l], conv_b[l],
                  w_rg_a[l], b_rg_a[l], w_rg_x[l], b_rg_x[l], lru_param[l], w_pool_mix[l],
                  pool_scale[l], w_br_rg[l], w_br_pool[l], w_out[l])
        yp, c_new, h_new, p_new = layer(
            yp,
            jnp.zeros((Bp, CONV_W - 1, D_RNN), yp.dtype),
            jnp.zeros((Bp, D_RNN), jnp.float32),
            jnp.zeros((Bp, POOL_MAX - 1, D_POOL), yp.dtype),
            0, *params)
        conv_p.append(c_new)
        h_p.append(h_new)
        pool_p.append(p_new)
        ys, c_new, h_new, p_new = layer(ys, state_conv[l], state_h[l], state_pool[l], POOL_MAX - 1, *params)
        conv_s.append(c_new)
        h_s.append(h_new)
        pool_s.append(p_new)
    new_conv_prompt = jnp.stack(conv_p)
    new_h_prompt = jnp.stack(h_p)
    new_pool_prompt = jnp.stack(pool_p)
    new_conv_sample = jnp.stack(conv_s)
    new_h_sample = jnp.stack(h_s)
    new_pool_sample = jnp.stack(pool_s)
    return (yp, ys, new_conv_prompt, new_h_prompt, new_pool_prompt, new_conv_sample, new_h_sample, new_pool_sample)
```

```python
import functools
import math

import jax
import jax.numpy as jnp
from jax import lax
from jax.experimental import pallas as pl
from jax.experimental.pallas import tpu as pltpu

F32 = jnp.float32
BF16 = jnp.bfloat16

D_MODEL = 2048
D_FF = 3 * D_MODEL
D_RNN = D_MODEL
D_POOL = D_MODEL // 2
N_RG_HEADS = 16
RG_BLOCK = D_RNN // N_RG_HEADS
CONV_W = 4
LRU_C = 8.0
POOL_WINDOWS = (2, 4, 8, 16)
POOL_GROUP = D_POOL // len(POOL_WINDOWS)
POOL_MAX = 16
EPS = 1e-6

SUBLANES = 8
MXU_COLS = 256
ROW_TILE = 512
FFN_CHUNK = 512
RG_CHUNK = 512
OUT_CHUNK = 512
NORM_ROWS = 32
ACC_COLS = 512
VMEM_LIMIT = 56 * 1024 * 1024

CONV_ROWS = (CONV_W - 1) * SUBLANES
POOL_ROWS = (POOL_MAX - 1) * SUBLANES
N_RG = D_RNN // RG_CHUNK
RG_HEADS = RG_CHUNK // RG_BLOCK
PAIR = 2 * SUBLANES
N_GROUPS = 4


def _rms(x, g):
    ms = jnp.mean(x * x, axis=-1, keepdims=True)
    return x * lax.rsqrt(ms + EPS) * g


def _norm_rows(x_ref, g, dst_ref, lo, hi):
    for r in range(lo, hi, NORM_ROWS):
        rows = slice(r, r + NORM_ROWS)
        y = _rms(x_ref[rows, :], g).astype(BF16)
        dst_ref[rows, :] = y
    return y[:PAIR, :RG_BLOCK]


def _norm_to_bf16(x_ref, g, dst_ref):
    _norm_rows(x_ref, g, dst_ref, 0, x_ref.shape[0])


def _anchor(operand_ref, never, deps):
    for k, dep in enumerate(deps):
        tile = (slice(0, 2 * SUBLANES), slice(k * 128, (k + 1) * 128))
        operand_ref[tile] = jnp.where(never, dep, operand_ref[tile])


def _residual_norm(x_ref, g, o_ref, scale):
    for r in range(0, x_ref.shape[0], NORM_ROWS):
        rows = slice(r, r + NORM_ROWS)
        y = _rms(o_ref[rows, :], g)
        if scale != 1.0:
            y = scale * y
        o_ref[rows, :] = x_ref[rows, :] + y


def _sigmoid(x):
    return 1.0 / (1.0 + jnp.exp(-x))


def _gelu_tanh(x):
    c = math.sqrt(2.0 / math.pi)
    return x * (0.5 * (1.0 + jnp.tanh(c * (x + 0.044715 * (x * x * x)))))


def _dot(a, b):
    return jnp.dot(a, b, preferred_element_type=F32)


def _interleave(matmul_pieces, vector_groups):
    for k in range(max(len(matmul_pieces), len(vector_groups))):
        if k < len(matmul_pieces):
            matmul_pieces[k]()
        if k < len(vector_groups):
            vector_groups[k]()


def _ffn_kernel(x_ref, g_ref, wg_ref, wu_ref, wo_ref, o_ref, xn_ref, *, pre, post, n_chunks):
    f = pl.program_id(1)

    @pl.when(f == 0)
    def _():
        _norm_to_bf16(x_ref, g_ref[pre:pre + 1, :], xn_ref)
        o_ref[...] = jnp.zeros(o_ref.shape, F32)

    xn = xn_ref[...]
    gate = _dot(xn, wg_ref[...])
    up = _dot(xn, wu_ref[...])
    hid = (gate * _sigmoid(gate) * up).astype(BF16)
    for n in range(D_MODEL // ACC_COLS):
        cs = slice(n * ACC_COLS, (n + 1) * ACC_COLS)
        o_ref[:, cs] += _dot(hid, wo_ref[:, cs])

    @pl.when(f == n_chunks - 1)
    def _():
        _residual_norm(x_ref, g_ref[post:post + 1, :], o_ref, 0.5)


def _ffn(x, gains, w_in, w_out, layer, which):
    rows = x.shape[0]
    n_chunks = D_FF // FFN_CHUNK
    kern = functools.partial(_ffn_kernel, pre=4 * which, post=4 * which + 1, n_chunks=n_chunks)
    return pl.pallas_call(
        kern,
        out_shape=jax.ShapeDtypeStruct((rows, D_MODEL), F32),
        grid=(rows // ROW_TILE, n_chunks),
        in_specs=[
            pl.BlockSpec((ROW_TILE, D_MODEL), lambda i, f: (i, 0)),
            pl.BlockSpec((None, 6, D_MODEL), lambda i, f: (layer, 0, 0)),
            pl.BlockSpec((None, None, D_MODEL, FFN_CHUNK), lambda i, f: (layer, which, 0, f)),
            pl.BlockSpec((None, None, D_MODEL, FFN_CHUNK), lambda i, f: (layer, which, 0, n_chunks + f)),
            pl.BlockSpec((None, None, FFN_CHUNK, D_MODEL), lambda i, f: (layer, which, f, 0)),
        ],
        out_specs=pl.BlockSpec((ROW_TILE, D_MODEL), lambda i, f: (i, 0)),
        scratch_shapes=[pltpu.VMEM((ROW_TILE, D_MODEL), BF16)],
        compiler_params=pltpu.CompilerParams(
            dimension_semantics=("arbitrary", "arbitrary"), vmem_limit_bytes=VMEM_LIMIT),
        name=f"ffn_l{layer}_{which}",
    )(x, gains, w_in, w_in, w_out)


N_STEPS = N_RG + 2


def _project_pieces(hb_ref, par, wxr_ref, wy_ref, zc_ref, y_ref, j):
    def piece(q):
        def run():
            hb = hb_ref[par]
            cs = slice((q % 2) * MXU_COLS, (q % 2 + 1) * MXU_COLS)
            if q < 2:
                zc_ref[j % 2][CONV_ROWS:, cs] = _dot(hb, wxr_ref[:, cs])
            else:
                y_ref[j % 3][:, cs] = _dot(hb, wy_ref[:, cs])
        return run
    return [piece(q) for q in range(2 * RG_CHUNK // MXU_COLS)]


def _pool_project_pieces(hb_ref, par, wp_ref, zp_ref):
    def piece(q):
        def run():
            cs = slice(q * MXU_COLS, (q + 1) * MXU_COLS)
            zp_ref[POOL_ROWS:, cs] = _dot(hb_ref[par], wp_ref[:, cs])
        return run
    return [piece(q) for q in range(D_POOL // MXU_COLS)]


def _conv(j, seq, rgp_ref, zc_ref, xc_ref, xcb_ref, cpre_ref, convo_ref):
    tm = ROW_TILE
    zc = zc_ref[j % 2]
    zc[0:CONV_ROWS, :] = cpre_ref[j]
    xc = rgp_ref[4:5, :]
    for k in range(CONV_W):
        xc = xc + zc[k * SUBLANES:k * SUBLANES + tm, :] * rgp_ref[k:k + 1, :]
    new_pre = zc[tm:tm + CONV_ROWS, :]
    cpre_ref[j] = new_pre
    convo_ref[seq, j] = new_pre
    xc_ref[j % 2][...] = xc
    xcb = xc.astype(BF16)
    xcb_ref[...] = xcb
    return xcb[:PAIR, :RG_BLOCK]


def _gate_project(j, wax_ref, xcb_ref, ri_ref):
    for hh in range(RG_HEADS):
        sl = slice(hh * RG_BLOCK, (hh + 1) * RG_BLOCK)
        ri_ref[j % 2][hh] = _dot(xcb_ref[:, sl], wax_ref[hh])


def _recur_groups(j, seq, rgp_ref, ri_ref, xc_ref, y_ref, hst_ref, cst_ref, recg_ref, ho_ref):
    tm = ROW_TILE
    carry = {}

    def setup():
        neg_lam = -rgp_ref[7:8, :]
        softplus = jnp.maximum(neg_lam, 0.0) + jnp.log1p(jnp.exp(-jnp.abs(neg_lam)))
        cst_ref[0] = jnp.broadcast_to(rgp_ref[5:6, :], (PAIR, RG_CHUNK))
        cst_ref[1] = jnp.broadcast_to(rgp_ref[6:7, :], (PAIR, RG_CHUNK))
        cst_ref[2] = jnp.broadcast_to(-LRU_C * softplus, (PAIR, RG_CHUNK))
        carry["h"] = hst_ref[j]

    def pair(r0):
        rows = slice(r0, r0 + PAIR)
        a_parts, u_parts = [], []
        for hh in range(RG_HEADS):
            sl = slice(hh * RG_BLOCK, (hh + 1) * RG_BLOCK)
            ri = ri_ref[j % 2][hh, rows, :]
            r = _sigmoid(ri[:, :RG_BLOCK] + cst_ref[0, :, sl])
            ig = _sigmoid(ri[:, RG_BLOCK:] + cst_ref[1, :, sl])
            log_a = cst_ref[2, :, sl] * r
            a = jnp.exp(log_a)
            u_parts.append(jnp.sqrt(-jnp.tanh(log_a) * (a * a + 1.0)) * (ig * xc_ref[j % 2][rows, sl]))
            a_parts.append(a)
        a = jnp.concatenate(a_parts, axis=1)
        u = jnp.concatenate(u_parts, axis=1)
        h1 = a[:SUBLANES] * carry["h"] + u[:SUBLANES]
        h2 = a[SUBLANES:] * h1 + u[SUBLANES:]
        carry["h"] = h2
        rec = jnp.concatenate([h1, h2], axis=0)
        out = (rec * _gelu_tanh(y_ref[j % 3][rows, :])).astype(BF16)
        recg_ref[rows, :] = out
        return out[:, :RG_BLOCK]

    def group(k):
        def run():
            if k == 0:
                setup()
            span = tm // N_GROUPS
            for r0 in range(k * span, (k + 1) * span, PAIR):
                dep = pair(r0)
            if k == N_GROUPS - 1:
                hst_ref[j] = carry["h"]
                ho_ref[seq, j] = carry["h"]
            return dep
        return run

    return [group(k) for k in range(N_GROUPS)]


def _pool_groups(i, seq, n_prompt_tiles, wpm_ref, ps_ref, zp_ref, pooled_ref, poolo_ref):
    tm = ROW_TILE
    frames = tm // SUBLANES

    def group(g):
        def run():
            w = POOL_WINDOWS[g]
            frame = lax.shift_right_logical(lax.broadcasted_iota(jnp.int32, (tm, POOL_GROUP), 0), 3)
            seen = jnp.where(i >= n_prompt_tiles, POOL_MAX - 1, i * frames) + 1
            seen = (frame + seen).astype(F32)
            cs = slice(g * POOL_GROUP, (g + 1) * POOL_GROUP)
            s = zp_ref[POOL_ROWS - (w - 1) * SUBLANES:, cs]
            shift = SUBLANES
            while shift < w * SUBLANES:
                s = s[shift:, :] + s[:s.shape[0] - shift, :]
                shift *= 2
            mean = s / jnp.minimum(seen, float(w))
            pooled = (mean - zp_ref[POOL_ROWS:, cs]).astype(BF16)
            mixed = _dot(pooled, wpm_ref[g]) * ps_ref[:, cs]
            pooled_ref[:, cs] = mixed.astype(BF16)
            if g == len(POOL_WINDOWS) - 1:
                new_pre = zp_ref[tm:tm + POOL_ROWS, :]
                zp_ref[0:POOL_ROWS, :] = new_pre
                poolo_ref[seq] = new_pre
        return run

    return [group(g) for g in range(len(POOL_WINDOWS))]


def _mixer_a_kernel(x_ref, g_ref, wxr_ref, wy_ref, wp_ref, rgp_conv_ref, rgp_recur_ref, wax_ref, wpm_ref,
                    ps_ref, sconv_ref, sh_ref, spool_ref,
                    recg_ref, pooled_ref, convo_ref, ho_ref, poolo_ref,
                    hb_ref, cpre_ref, hst_ref, zc0_ref, zc1_ref, xc0_ref, xc1_ref, xcb_ref, ri0_ref, ri1_ref,
                    y0_ref, y1_ref, y2_ref, cst_ref, zp_ref,
                    *, n_prompt_tiles):
    zc_ref = (zc0_ref, zc1_ref)
    xc_ref = (xc0_ref, xc1_ref)
    ri_ref = (ri0_ref, ri1_ref)
    y_ref = (y0_ref, y1_ref, y2_ref)
    i = pl.program_id(0)
    s = pl.program_id(1)
    seq = (i >= n_prompt_tiles).astype(jnp.int32)
    par = i % 2

    def project(j):
        return _project_pieces(hb_ref, par, wxr_ref, wy_ref, zc_ref, y_ref, j)

    def conv(j):
        return lambda: _conv(j, seq, rgp_conv_ref, zc_ref, xc_ref, xcb_ref, cpre_ref, convo_ref)

    def gate_project(j):
        return lambda: _gate_project(j, wax_ref, xcb_ref, ri_ref)

    def recur(j):
        return _recur_groups(j, seq, rgp_recur_ref, ri_ref, xc_ref, y_ref, hst_ref, cst_ref, recg_ref, ho_ref)

    def norm_next():
        span = ROW_TILE // N_GROUPS
        return [functools.partial(_norm_rows, x_ref, g_ref[2:3, :], hb_ref.at[1 - par], k * span, (k + 1) * span)
                for k in range(N_GROUPS)]

    @pl.when(s == 0)
    def _():
        @pl.when(i == 0)
        def _():
            cpre_ref[...] = jnp.zeros(cpre_ref.shape, F32)
            hst_ref[...] = jnp.zeros(hst_ref.shape, F32)
            zp_ref[0:POOL_ROWS, :] = jnp.zeros((POOL_ROWS, D_POOL), F32)
            _norm_to_bf16(x_ref, g_ref[2:3, :], hb_ref.at[0])

        @pl.when(i == n_prompt_tiles)
        def _():
            for j in range(N_RG):
                cs = slice(j * RG_CHUNK, (j + 1) * RG_CHUNK)
                cpre_ref[j] = sconv_ref[:, cs]
                hst_ref[j] = sh_ref[:, cs]
            zp_ref[0:POOL_ROWS, :] = spool_ref[...]

        for piece in project(0):
            piece()

    never = i < 0

    def overlapped(pieces, conv_stage, groups, gate_stage):
        pieces[0]()
        deps = [conv_stage(), groups[0]()]
        for k in range(1, len(pieces)):
            _anchor(hb_ref.at[par], never, deps)
            pieces[k]()
            deps = [groups[k]()]
        _anchor(xcb_ref, never, deps)
        gate_stage()

    @pl.when(s == 1)
    def _():
        overlapped(project(1), conv(0), norm_next(), gate_project(0))

    for step in range(2, N_RG):
        @pl.when(s == step)
        def _(step=step):
            overlapped(project(step), conv(step - 1), recur(step - 2), gate_project(step - 1))

    @pl.when(s == N_RG)
    def _():
        overlapped(_pool_project_pieces(hb_ref, par, wp_ref, zp_ref), conv(N_RG - 1), recur(N_RG - 2),
                   gate_project(N_RG - 1))

    @pl.when(s == N_RG + 1)
    def _():
        _interleave(_pool_groups(i, seq, n_prompt_tiles, wpm_ref, ps_ref, zp_ref, pooled_ref, poolo_ref),
                    recur(N_RG - 1))


def _mixer_a(x, gains, w_in, rgp, wax, wpm, pscale, sconv, sh, spool, layer, n_prompt_tiles):
    rows = x.shape[0]
    n_tiles = rows // ROW_TILE
    pool_blk = (2 * D_RNN) // D_POOL

    def chunk(s, lag):
        return jnp.clip(s - lag, 0, N_RG - 1)

    def x_tile(i, s):
        return jnp.minimum(i + (s >= 1).astype(jnp.int32), n_tiles - 1)

    kern = functools.partial(_mixer_a_kernel, n_prompt_tiles=n_prompt_tiles)
    return pl.pallas_call(
        kern,
        out_shape=(
            jax.ShapeDtypeStruct((rows, D_RNN), BF16),
            jax.ShapeDtypeStruct((rows, D_POOL), BF16),
            jax.ShapeDtypeStruct((2, N_RG, CONV_ROWS, RG_CHUNK), F32),
            jax.ShapeDtypeStruct((2, N_RG, SUBLANES, RG_CHUNK), F32),
            jax.ShapeDtypeStruct((2, POOL_ROWS, D_POOL), F32),
        ),
        grid=(n_tiles, N_STEPS),
        in_specs=[
            pl.BlockSpec((ROW_TILE, D_MODEL), lambda i, s: (x_tile(i, s), 0)),
            pl.BlockSpec((None, 6, D_MODEL), lambda i, s: (layer, 0, 0)),
            pl.BlockSpec((None, D_MODEL, RG_CHUNK), lambda i, s: (layer, 0, chunk(s, 0))),
            pl.BlockSpec((None, D_MODEL, RG_CHUNK), lambda i, s: (layer, 0, N_RG + chunk(s, 0))),
            pl.BlockSpec((None, D_MODEL, D_POOL), lambda i, s: (layer, 0, pool_blk)),
            pl.BlockSpec((None, 8, RG_CHUNK), lambda i, s: (layer, 0, chunk(s, 1))),
            pl.BlockSpec((None, 8, RG_CHUNK), lambda i, s: (layer, 0, chunk(s, 2))),
            pl.BlockSpec((None, RG_HEADS, RG_BLOCK, 2 * RG_BLOCK), lambda i, s: (layer, chunk(s, 1), 0, 0)),
            pl.BlockSpec((None, len(POOL_WINDOWS), POOL_GROUP, POOL_GROUP), lambda i, s: (layer, 0, 0, 0)),
            pl.BlockSpec((None, 1, D_POOL), lambda i, s: (layer, 0, 0)),
            pl.BlockSpec((None, CONV_ROWS, D_RNN), lambda i, s: (layer, 0, 0)),
            pl.BlockSpec((None, SUBLANES, D_RNN), lambda i, s: (layer, 0, 0)),
            pl.BlockSpec((None, POOL_ROWS, D_POOL), lambda i, s: (layer, 0, 0)),
        ],
        out_specs=(
            pl.BlockSpec((ROW_TILE, RG_CHUNK), lambda i, s: (i, chunk(s, 2))),
            pl.BlockSpec((ROW_TILE, D_POOL), lambda i, s: (i, 0)),
            pl.BlockSpec((2, N_RG, CONV_ROWS, RG_CHUNK), lambda i, s: (0, 0, 0, 0)),
            pl.BlockSpec((2, N_RG, SUBLANES, RG_CHUNK), lambda i, s: (0, 0, 0, 0)),
            pl.BlockSpec((2, POOL_ROWS, D_POOL), lambda i, s: (0, 0, 0)),
        ),
        scratch_shapes=[
            pltpu.VMEM((2, ROW_TILE, D_MODEL), BF16),
            pltpu.VMEM((N_RG, CONV_ROWS, RG_CHUNK), F32),
            pltpu.VMEM((N_RG, SUBLANES, RG_CHUNK), F32),
            pltpu.VMEM((CONV_ROWS + ROW_TILE, RG_CHUNK), F32),
            pltpu.VMEM((CONV_ROWS + ROW_TILE, RG_CHUNK), F32),
            pltpu.VMEM((ROW_TILE, RG_CHUNK), F32),
            pltpu.VMEM((ROW_TILE, RG_CHUNK), F32),
            pltpu.VMEM((ROW_TILE, RG_CHUNK), BF16),
            pltpu.VMEM((RG_HEADS, ROW_TILE, 2 * RG_BLOCK), F32),
            pltpu.VMEM((RG_HEADS, ROW_TILE, 2 * RG_BLOCK), F32),
            pltpu.VMEM((ROW_TILE, RG_CHUNK), F32),
            pltpu.VMEM((ROW_TILE, RG_CHUNK), F32),
            pltpu.VMEM((ROW_TILE, RG_CHUNK), F32),
            pltpu.VMEM((3, PAIR, RG_CHUNK), F32),
            pltpu.VMEM((POOL_ROWS + ROW_TILE, D_POOL), F32),
        ],
        compiler_params=pltpu.CompilerParams(
            dimension_semantics=("arbitrary", "arbitrary"), vmem_limit_bytes=VMEM_LIMIT),
        name=f"mixer_a_l{layer}",
    )(x, gains, w_in, w_in, w_in, rgp, rgp, wax, wpm, pscale, sconv, sh, spool)


def _mixer_b_kernel(x_ref, g_ref, recg_ref, pooled_ref, wgr_ref, wgp_ref, wbr_ref, wbp_ref, wo_ref,
                    o_ref, hb_ref, *, n_chunks):
    n = pl.program_id(1)

    @pl.when(n == 0)
    def _():
        _norm_to_bf16(x_ref, g_ref[2:3, :], hb_ref)
        o_ref[...] = jnp.zeros(o_ref.shape, F32)

    hb = hb_ref[...]
    g_rg = _dot(hb, wgr_ref[...])
    g_pool = _dot(hb, wgp_ref[...])
    br = _dot(recg_ref[...], wbr_ref[...])
    bp = _dot(pooled_ref[...], wbp_ref[...])
    mix = (_sigmoid(g_rg) * br + _sigmoid(g_pool) * bp).astype(BF16)
    for c in range(D_MODEL // ACC_COLS):
        cs = slice(c * ACC_COLS, (c + 1) * ACC_COLS)
        o_ref[:, cs] += _dot(mix, wo_ref[:, cs])

    @pl.when(n == n_chunks - 1)
    def _():
        _residual_norm(x_ref, g_ref[3:4, :], o_ref, 1.0)


def _mixer_b(x, gains, recg, pooled, w_in, w_br_rg, w_br_pool, w_out, layer):
    rows = x.shape[0]
    n_chunks = D_MODEL // OUT_CHUNK
    g_rg_blk = (2 * D_RNN + D_POOL) // OUT_CHUNK
    g_pool_blk = g_rg_blk + n_chunks
    kern = functools.partial(_mixer_b_kernel, n_chunks=n_chunks)
    return pl.pallas_call(
        kern,
        out_shape=jax.ShapeDtypeStruct((rows, D_MODEL), F32),
        grid=(rows // ROW_TILE, n_chunks),
        in_specs=[
            pl.BlockSpec((ROW_TILE, D_MODEL), lambda i, n: (i, 0)),
            pl.BlockSpec((None, 6, D_MODEL), lambda i, n: (layer, 0, 0)),
            pl.BlockSpec((ROW_TILE, D_RNN), lambda i, n: (i, 0)),
            pl.BlockSpec((ROW_TILE, D_POOL), lambda i, n: (i, 0)),
            pl.BlockSpec((None, D_MODEL, OUT_CHUNK), lambda i, n: (layer, 0, g_rg_blk + n)),
            pl.BlockSpec((None, D_MODEL, OUT_CHUNK), lambda i, n: (layer, 0, g_pool_blk + n)),
            pl.BlockSpec((None, D_RNN, OUT_CHUNK), lambda i, n: (layer, 0, n)),
            pl.BlockSpec((None, D_POOL, OUT_CHUNK), lambda i, n: (layer, 0, n)),
            pl.BlockSpec((None, OUT_CHUNK, D_MODEL), lambda i, n: (layer, n, 0)),
        ],
        out_specs=pl.BlockSpec((ROW_TILE, D_MODEL), lambda i, n: (i, 0)),
        scratch_shapes=[pltpu.VMEM((ROW_TILE, D_MODEL), BF16)],
        compiler_params=pltpu.CompilerParams(
            dimension_semantics=("arbitrary", "arbitrary"), vmem_limit_bytes=VMEM_LIMIT),
        name=f"mixer_b_l{layer}",
    )(x, gains, recg, pooled, w_in, w_in, w_br_rg, w_br_pool, w_out)


def _time_major(x):
    b, t, c = x.shape
    return jnp.swapaxes(x, 0, 1).reshape(t * b, c)


def _batch_major(x, b):
    return jnp.swapaxes(x.reshape(-1, b, x.shape[-1]), 0, 1)


def kernel(x_prompt, x_sample, state_conv, state_h, state_pool, norm_gains, w_ffn_in, w_ffn_out, w_in,
           conv_w, conv_b, w_rg_a, b_rg_a, w_rg_x, b_rg_x, lru_param, w_pool_mix, pool_scale,
           w_br_rg, w_br_pool, w_out):
    depth = norm_gains.shape[0]
    batch, seq, _ = x_prompt.shape
    dec_batch = x_sample.shape[0]
    assert batch == SUBLANES and dec_batch == SUBLANES
    n_prompt_rows = batch * seq
    assert n_prompt_rows % ROW_TILE == 0 and x_sample.shape[1] * dec_batch == ROW_TILE
    n_prompt_tiles = n_prompt_rows // ROW_TILE

    x = jnp.concatenate([_time_major(x_prompt), _time_major(x_sample)], axis=0)

    w_ffn_in_b = w_ffn_in.astype(BF16)
    w_ffn_out_b = w_ffn_out.astype(BF16)
    w_in_b = w_in.astype(BF16)
    w_br_rg_b = w_br_rg.astype(BF16)
    w_br_pool_b = w_br_pool.astype(BF16)
    w_out_b = w_out.astype(BF16)
    wpm_b = w_pool_mix.astype(BF16)
    wax_b = jnp.concatenate([w_rg_a, w_rg_x], axis=-1).astype(BF16)
    rgp = jnp.concatenate(
        [conv_w, conv_b[:, None], b_rg_a[:, None], b_rg_x[:, None], lru_param[:, None]], axis=1)
    pscale = pool_scale[:, None, :]
    sconv = jnp.swapaxes(state_conv, 1, 2).reshape(depth, CONV_ROWS, D_RNN)
    spool = jnp.swapaxes(state_pool, 1, 2).reshape(depth, POOL_ROWS, D_POOL)

    convs, hs, pools = [], [], []
    for l in range(depth):
        x = _ffn(x, norm_gains, w_ffn_in_b, w_ffn_out_b, l, 0)
        recg, pooled, conv_o, h_o, pool_o = _mixer_a(
            x, norm_gains, w_in_b, rgp, wax_b, wpm_b, pscale, sconv, state_h, spool, l, n_prompt_tiles)
        x = _mixer_b(x, norm_gains, recg, pooled, w_in_b, w_br_rg_b, w_br_pool_b, w_out_b, l)
        x = _ffn(x, norm_gains, w_ffn_in_b, w_ffn_out_b, l, 1)
        convs.append(jnp.swapaxes(conv_o, 1, 2).reshape(2, CONV_ROWS, D_RNN))
        hs.append(jnp.swapaxes(h_o, 1, 2).reshape(2, SUBLANES, D_RNN))
        pools.append(pool_o)

    y_prompt = _batch_major(x[:n_prompt_rows], batch)
    y_sample = _batch_major(x[n_prompt_rows:], dec_batch)
    conv_all = jnp.stack(convs)
    h_all = jnp.stack(hs)
    pool_all = jnp.stack(pools)

    def unroll_state(s, which, frames):
        s = s[:, which]
        return jnp.swapaxes(s.reshape(depth, frames, SUBLANES, s.shape[-1]), 1, 2)

    return (
        y_prompt,
        y_sample,
        unroll_state(conv_all, 0, CONV_W - 1),
        h_all[:, 0],
        unroll_state(pool_all, 0, POOL_MAX - 1),
        unroll_state(conv_all, 1, CONV_W - 1),
        h_all[:, 1],
        unroll_state(pool_all, 1, POOL_MAX - 1),
    )
```

```python
import functools
import math

import jax
import jax.numpy as jnp
from jax import lax
from jax.experimental import pallas as pl
from jax.experimental.pallas import tpu as pltpu

F32 = jnp.float32
BF16 = jnp.bfloat16

D_MODEL = 2048
D_FF = 3 * D_MODEL
D_RNN = D_MODEL
D_POOL = D_MODEL // 2
N_RG_HEADS = 16
RG_BLOCK = D_RNN // N_RG_HEADS
CONV_W = 4
LRU_C = 8.0
POOL_WINDOWS = (2, 4, 8, 16)
POOL_GROUP = D_POOL // len(POOL_WINDOWS)
POOL_MAX = 16
EPS = 1e-6

SUBLANES = 8
MXU_COLS = 256
ROW_TILE = 512
FFN_CHUNK = 1024
RG_CHUNK = 512
OUT_CHUNK = 512
NORM_ROWS = 32
ACC_COLS = 512
VMEM_LIMIT = 56 * 1024 * 1024

CONV_ROWS = (CONV_W - 1) * SUBLANES
POOL_ROWS = (POOL_MAX - 1) * SUBLANES
N_RG = D_RNN // RG_CHUNK
RG_HEADS = RG_CHUNK // RG_BLOCK
PAIR = 2 * SUBLANES
N_GROUPS = 4


def _rms(x, g):
    ms = jnp.mean(x * x, axis=-1, keepdims=True)
    return x * lax.rsqrt(ms + EPS) * g


def _norm_rows(x_ref, g, dst_ref, lo, hi):
    for r in range(lo, hi, NORM_ROWS):
        rows = slice(r, r + NORM_ROWS)
        dst_ref[rows, :] = _rms(x_ref[rows, :], g).astype(BF16)


def _norm_to_bf16(x_ref, g, dst_ref):
    _norm_rows(x_ref, g, dst_ref, 0, x_ref.shape[0])


def _residual_norm(x_ref, g, o_ref, scale):
    for r in range(0, x_ref.shape[0], NORM_ROWS):
        rows = slice(r, r + NORM_ROWS)
        y = _rms(o_ref[rows, :], g)
        if scale != 1.0:
            y = scale * y
        o_ref[rows, :] = x_ref[rows, :] + y


def _sigmoid(x):
    return 1.0 / (1.0 + jnp.exp(-x))


def _gelu_tanh(x):
    c = math.sqrt(2.0 / math.pi)
    return x * (0.5 * (1.0 + jnp.tanh(c * (x + 0.044715 * (x * x * x)))))


def _dot(a, b):
    return jnp.dot(a, b, preferred_element_type=F32)


def _interleave(matmul_pieces, vector_groups):
    for k in range(max(len(matmul_pieces), len(vector_groups))):
        if k < len(matmul_pieces):
            matmul_pieces[k]()
        if k < len(vector_groups):
            vector_groups[k]()


def _ffn_kernel(x_ref, g_ref, wg_ref, wu_ref, wo_ref, *rest, pre, post, n_chunks, n_casts):
    cast_src = rest[:n_casts]
    o_ref = rest[n_casts]
    cast_dst = rest[n_casts + 1:2 * n_casts + 1]
    xn_ref = rest[2 * n_casts + 1]
    f = pl.program_id(1)

    for src, dst in zip(cast_src, cast_dst):
        dst[...] = src[...].astype(BF16)

    @pl.when(f == 0)
    def _():
        _norm_to_bf16(x_ref, g_ref[pre:pre + 1, :], xn_ref)
        o_ref[...] = jnp.zeros(o_ref.shape, F32)

    xn = xn_ref[...]
    gate = _dot(xn, wg_ref[...])
    up = _dot(xn, wu_ref[...])
    hid = (gate * _sigmoid(gate) * up).astype(BF16)
    for n in range(D_MODEL // ACC_COLS):
        cs = slice(n * ACC_COLS, (n + 1) * ACC_COLS)
        o_ref[:, cs] += _dot(hid, wo_ref[:, cs])

    @pl.when(f == n_chunks - 1)
    def _():
        _residual_norm(x_ref, g_ref[post:post + 1, :], o_ref, 0.5)


BF16_TILE_ROWS = 16


def _ffn(x, gains, w_in, w_out, layer, which, casts=()):
    rows = x.shape[0]
    n_chunks = D_FF // FFN_CHUNK
    n_tiles = rows // ROW_TILE
    n_steps = n_tiles * n_chunks

    cast_in_specs, cast_out_specs, cast_out_shapes = [], [], []
    for src, lead in casts:
        m_rows, m_cols = src.shape[-2:]
        per_step = -(-m_rows // n_steps)
        blk_rows = -(-per_step // BF16_TILE_ROWS) * BF16_TILE_ROWS
        assert m_rows % blk_rows == 0
        n_blk = m_rows // blk_rows

        def blk(i, f, n_blk=n_blk):
            return jnp.minimum(i * n_chunks + f, n_blk - 1)

        cast_in_specs.append(pl.BlockSpec(
            (None,) * len(lead) + (blk_rows, m_cols), lambda i, f, lead=lead, blk=blk: lead + (blk(i, f), 0)))
        cast_out_specs.append(pl.BlockSpec((blk_rows, m_cols), lambda i, f, blk=blk: (blk(i, f), 0)))
        cast_out_shapes.append(jax.ShapeDtypeStruct((m_rows, m_cols), BF16))

    kern = functools.partial(_ffn_kernel, pre=4 * which, post=4 * which + 1, n_chunks=n_chunks,
                             n_casts=len(casts))
    outs = pl.pallas_call(
        kern,
        out_shape=[jax.ShapeDtypeStruct((rows, D_MODEL), F32)] + cast_out_shapes,
        grid=(n_tiles, n_chunks),
        in_specs=[
            pl.BlockSpec((ROW_TILE, D_MODEL), lambda i, f: (i, 0)),
            pl.BlockSpec((None, 6, D_MODEL), lambda i, f: (layer, 0, 0)),
            pl.BlockSpec((D_MODEL, FFN_CHUNK), lambda i, f: (0, f)),
            pl.BlockSpec((D_MODEL, FFN_CHUNK), lambda i, f: (0, n_chunks + f)),
            pl.BlockSpec((FFN_CHUNK, D_MODEL), lambda i, f: (f, 0)),
        ] + cast_in_specs,
        out_specs=[pl.BlockSpec((ROW_TILE, D_MODEL), lambda i, f: (i, 0))] + cast_out_specs,
        scratch_shapes=[pltpu.VMEM((ROW_TILE, D_MODEL), BF16)],
        compiler_params=pltpu.CompilerParams(
            dimension_semantics=("arbitrary", "arbitrary"), vmem_limit_bytes=VMEM_LIMIT),
        name=f"ffn_l{layer}_{which}",
    )(x, gains, w_in, w_in, w_out, *[src for src, _ in casts])
    return outs[0], list(outs[1:])


N_STEPS = N_RG + 2


def _project_pieces(hb_ref, par, wxr_ref, wy_ref, zc_ref, y_ref, j):
    def piece(q):
        def run():
            hb = hb_ref[par]
            cs = slice((q % 2) * MXU_COLS, (q % 2 + 1) * MXU_COLS)
            if q < 2:
                zc_ref[j % 2][CONV_ROWS:, cs] = _dot(hb, wxr_ref[:, cs])
            else:
                y_ref[j % 3][:, cs] = _dot(hb, wy_ref[:, cs])
        return run
    return [piece(q) for q in range(2 * RG_CHUNK // MXU_COLS)]


def _pool_project_pieces(hb_ref, par, wp_ref, zp_ref):
    def piece(q):
        def run():
            cs = slice(q * MXU_COLS, (q + 1) * MXU_COLS)
            zp_ref[POOL_ROWS:, cs] = _dot(hb_ref[par], wp_ref[:, cs])
        return run
    return [piece(q) for q in range(D_POOL // MXU_COLS)]


def _conv(j, seq, rgp_ref, zc_ref, xc_ref, xcb_ref, cpre_ref, convo_ref):
    tm = ROW_TILE
    zc = zc_ref[j % 2]
    zc[0:CONV_ROWS, :] = cpre_ref[j]
    xc = rgp_ref[4:5, :]
    for k in range(CONV_W):
        xc = xc + zc[k * SUBLANES:k * SUBLANES + tm, :] * rgp_ref[k:k + 1, :]
    new_pre = zc[tm:tm + CONV_ROWS, :]
    cpre_ref[j] = new_pre
    convo_ref[seq, j] = new_pre
    xc_ref[j % 2][...] = xc
    xcb_ref[...] = xc.astype(BF16)


def _gate_project(j, wax_ref, xcb_ref, ri_ref):
    for hh in range(RG_HEADS):
        sl = slice(hh * RG_BLOCK, (hh + 1) * RG_BLOCK)
        ri_ref[j % 2][hh] = _dot(xcb_ref[:, sl], wax_ref[hh])


def _recur_groups(j, seq, rgp_ref, ri_ref, xc_ref, y_ref, hst_ref, cst_ref, recg_ref, ho_ref):
    tm = ROW_TILE
    carry = {}

    def setup():
        neg_lam = -rgp_ref[7:8, :]
        softplus = jnp.maximum(neg_lam, 0.0) + jnp.log1p(jnp.exp(-jnp.abs(neg_lam)))
        cst_ref[0] = jnp.broadcast_to(rgp_ref[5:6, :], (PAIR, RG_CHUNK))
        cst_ref[1] = jnp.broadcast_to(rgp_ref[6:7, :], (PAIR, RG_CHUNK))
        cst_ref[2] = jnp.broadcast_to(-LRU_C * softplus, (PAIR, RG_CHUNK))
        carry["h"] = hst_ref[j]

    def pair(r0):
        rows = slice(r0, r0 + PAIR)
        a_parts, u_parts = [], []
        for hh in range(RG_HEADS):
            sl = slice(hh * RG_BLOCK, (hh + 1) * RG_BLOCK)
            ri = ri_ref[j % 2][hh, rows, :]
            r = _sigmoid(ri[:, :RG_BLOCK] + cst_ref[0, :, sl])
            ig = _sigmoid(ri[:, RG_BLOCK:] + cst_ref[1, :, sl])
            log_a = cst_ref[2, :, sl] * r
            a = jnp.exp(log_a)
            u_parts.append(jnp.sqrt(-jnp.tanh(log_a) * (a * a + 1.0)) * (ig * xc_ref[j % 2][rows, sl]))
            a_parts.append(a)
        a = jnp.concatenate(a_parts, axis=1)
        u = jnp.concatenate(u_parts, axis=1)
        h1 = a[:SUBLANES] * carry["h"] + u[:SUBLANES]
        h2 = a[SUBLANES:] * h1 + u[SUBLANES:]
        carry["h"] = h2
        rec = jnp.concatenate([h1, h2], axis=0)
        recg_ref[rows, :] = (rec * _gelu_tanh(y_ref[j % 3][rows, :])).astype(BF16)

    def group(k):
        def run():
            if k == 0:
                setup()
            span = tm // N_GROUPS
            for r0 in range(k * span, (k + 1) * span, PAIR):
                pair(r0)
            if k == N_GROUPS - 1:
                hst_ref[j] = carry["h"]
                ho_ref[seq, j] = carry["h"]
        return run

    return [group(k) for k in range(N_GROUPS)]


def _pool_groups(i, seq, n_prompt_tiles, wpm_ref, ps_ref, zp_ref, pooled_ref, poolo_ref):
    tm = ROW_TILE
    frames = tm // SUBLANES

    def group(g):
        def run():
            w = POOL_WINDOWS[g]
            frame = lax.shift_right_logical(lax.broadcasted_iota(jnp.int32, (tm, POOL_GROUP), 0), 3)
            seen = jnp.where(i >= n_prompt_tiles, POOL_MAX - 1, i * frames) + 1
            seen = (frame + seen).astype(F32)
            cs = slice(g * POOL_GROUP, (g + 1) * POOL_GROUP)
            s = zp_ref[POOL_ROWS - (w - 1) * SUBLANES:, cs]
            shift = SUBLANES
            while shift < w * SUBLANES:
                s = s[shift:, :] + s[:s.shape[0] - shift, :]
                shift *= 2
            mean = s / jnp.minimum(seen, float(w))
            pooled = (mean - zp_ref[POOL_ROWS:, cs]).astype(BF16)
            mixed = _dot(pooled, wpm_ref[g]) * ps_ref[:, cs]
            pooled_ref[:, cs] = mixed.astype(BF16)
            if g == len(POOL_WINDOWS) - 1:
                new_pre = zp_ref[tm:tm + POOL_ROWS, :]
                zp_ref[0:POOL_ROWS, :] = new_pre
                poolo_ref[seq] = new_pre
        return run

    return [group(g) for g in range(len(POOL_WINDOWS))]


def _mixer_a_kernel(x_ref, g_ref, wxr_ref, wy_ref, wp_ref, rgp_conv_ref, rgp_recur_ref, wax_ref, wpm_ref,
                    ps_ref, sconv_ref, sh_ref, spool_ref,
                    recg_ref, pooled_ref, convo_ref, ho_ref, poolo_ref,
                    hb_ref, cpre_ref, hst_ref, zc0_ref, zc1_ref, xc0_ref, xc1_ref, xcb_ref, ri0_ref, ri1_ref,
                    y0_ref, y1_ref, y2_ref, cst_ref, zp_ref,
                    *, n_prompt_tiles):
    zc_ref = (zc0_ref, zc1_ref)
    xc_ref = (xc0_ref, xc1_ref)
    ri_ref = (ri0_ref, ri1_ref)
    y_ref = (y0_ref, y1_ref, y2_ref)
    i = pl.program_id(0)
    s = pl.program_id(1)
    seq = (i >= n_prompt_tiles).astype(jnp.int32)
    par = i % 2

    def project(j):
        return _project_pieces(hb_ref, par, wxr_ref, wy_ref, zc_ref, y_ref, j)

    def conv(j):
        return lambda: _conv(j, seq, rgp_conv_ref, zc_ref, xc_ref, xcb_ref, cpre_ref, convo_ref)

    def gate_project(j):
        return lambda: _gate_project(j, wax_ref, xcb_ref, ri_ref)

    def recur(j):
        return _recur_groups(j, seq, rgp_recur_ref, ri_ref, xc_ref, y_ref, hst_ref, cst_ref, recg_ref, ho_ref)

    def norm_next():
        span = ROW_TILE // N_GROUPS
        return [functools.partial(_norm_rows, x_ref, g_ref[2:3, :], hb_ref.at[1 - par], k * span, (k + 1) * span)
                for k in range(N_GROUPS)]

    @pl.when(s == 0)
    def _():
        @pl.when(i == 0)
        def _():
            cpre_ref[...] = jnp.zeros(cpre_ref.shape, F32)
            hst_ref[...] = jnp.zeros(hst_ref.shape, F32)
            zp_ref[0:POOL_ROWS, :] = jnp.zeros((POOL_ROWS, D_POOL), F32)
            _norm_to_bf16(x_ref, g_ref[2:3, :], hb_ref.at[0])

        @pl.when(i == n_prompt_tiles)
        def _():
            for j in range(N_RG):
                cs = slice(j * RG_CHUNK, (j + 1) * RG_CHUNK)
                cpre_ref[j] = sconv_ref[:, cs]
                hst_ref[j] = sh_ref[:, cs]
            zp_ref[0:POOL_ROWS, :] = spool_ref[...]

        for piece in project(0):
            piece()

    def overlapped(pieces, conv_stage, groups, gate_stage):
        conv_stage()
        _interleave(pieces, groups)
        gate_stage()

    @pl.when(s == 1)
    def _():
        overlapped(project(1), conv(0), norm_next(), gate_project(0))

    for step in range(2, N_RG):
        @pl.when(s == step)
        def _(step=step):
            overlapped(project(step), conv(step - 1), recur(step - 2), gate_project(step - 1))

    @pl.when(s == N_RG)
    def _():
        overlapped(_pool_project_pieces(hb_ref, par, wp_ref, zp_ref), conv(N_RG - 1), recur(N_RG - 2),
                   gate_project(N_RG - 1))

    @pl.when(s == N_RG + 1)
    def _():
        _interleave(_pool_groups(i, seq, n_prompt_tiles, wpm_ref, ps_ref, zp_ref, pooled_ref, poolo_ref),
                    recur(N_RG - 1))


def _mixer_a(x, gains, w_in, rgp, wax, wpm, pscale, sconv, sh, spool, layer, n_prompt_tiles):
    rows = x.shape[0]
    n_tiles = rows // ROW_TILE
    pool_blk = (2 * D_RNN) // D_POOL

    def chunk(s, lag):
        return jnp.clip(s - lag, 0, N_RG - 1)

    def x_tile(i, s):
        return jnp.minimum(i + (s >= 1).astype(jnp.int32), n_tiles - 1)

    kern = functools.partial(_mixer_a_kernel, n_prompt_tiles=n_prompt_tiles)
    return pl.pallas_call(
        kern,
        out_shape=(
            jax.ShapeDtypeStruct((rows, D_RNN), BF16),
            jax.ShapeDtypeStruct((rows, D_POOL), BF16),
            jax.ShapeDtypeStruct((2, N_RG, CONV_ROWS, RG_CHUNK), F32),
            jax.ShapeDtypeStruct((2, N_RG, SUBLANES, RG_CHUNK), F32),
            jax.ShapeDtypeStruct((2, POOL_ROWS, D_POOL), F32),
        ),
        grid=(n_tiles, N_STEPS),
        in_specs=[
            pl.BlockSpec((ROW_TILE, D_MODEL), lambda i, s: (x_tile(i, s), 0)),
            pl.BlockSpec((None, 6, D_MODEL), lambda i, s: (layer, 0, 0)),
            pl.BlockSpec((D_MODEL, RG_CHUNK), lambda i, s: (0, chunk(s, 0))),
            pl.BlockSpec((D_MODEL, RG_CHUNK), lambda i, s: (0, N_RG + chunk(s, 0))),
            pl.BlockSpec((D_MODEL, D_POOL), lambda i, s: (0, pool_blk)),
            pl.BlockSpec((None, 8, RG_CHUNK), lambda i, s: (layer, 0, chunk(s, 1))),
            pl.BlockSpec((None, 8, RG_CHUNK), lambda i, s: (layer, 0, chunk(s, 2))),
            pl.BlockSpec((None, RG_HEADS, RG_BLOCK, 2 * RG_BLOCK), lambda i, s: (layer, chunk(s, 1), 0, 0)),
            pl.BlockSpec((None, len(POOL_WINDOWS), POOL_GROUP, POOL_GROUP), lambda i, s: (layer, 0, 0, 0)),
            pl.BlockSpec((None, 1, D_POOL), lambda i, s: (layer, 0, 0)),
            pl.BlockSpec((None, CONV_ROWS, D_RNN), lambda i, s: (layer, 0, 0)),
            pl.BlockSpec((None, SUBLANES, D_RNN), lambda i, s: (layer, 0, 0)),
            pl.BlockSpec((None, POOL_ROWS, D_POOL), lambda i, s: (layer, 0, 0)),
        ],
        out_specs=(
            pl.BlockSpec((ROW_TILE, RG_CHUNK), lambda i, s: (i, chunk(s, 2))),
            pl.BlockSpec((ROW_TILE, D_POOL), lambda i, s: (i, 0)),
            pl.BlockSpec((2, N_RG, CONV_ROWS, RG_CHUNK), lambda i, s: (0, 0, 0, 0)),
            pl.BlockSpec((2, N_RG, SUBLANES, RG_CHUNK), lambda i, s: (0, 0, 0, 0)),
            pl.BlockSpec((2, POOL_ROWS, D_POOL), lambda i, s: (0, 0, 0)),
        ),
        scratch_shapes=[
            pltpu.VMEM((2, ROW_TILE, D_MODEL), BF16),
            pltpu.VMEM((N_RG, CONV_ROWS, RG_CHUNK), F32),
            pltpu.VMEM((N_RG, SUBLANES, RG_CHUNK), F32),
            pltpu.VMEM((CONV_ROWS + ROW_TILE, RG_CHUNK), F32),
            pltpu.VMEM((CONV_ROWS + ROW_TILE, RG_CHUNK), F32),
            pltpu.VMEM((ROW_TILE, RG_CHUNK), F32),
            pltpu.VMEM((ROW_TILE, RG_CHUNK), F32),
            pltpu.VMEM((ROW_TILE, RG_CHUNK), BF16),
            pltpu.VMEM((RG_HEADS, ROW_TILE, 2 * RG_BLOCK), F32),
            pltpu.VMEM((RG_HEADS, ROW_TILE, 2 * RG_BLOCK), F32),
            pltpu.VMEM((ROW_TILE, RG_CHUNK), F32),
            pltpu.VMEM((ROW_TILE, RG_CHUNK), F32),
            pltpu.VMEM((ROW_TILE, RG_CHUNK), F32),
            pltpu.VMEM((3, PAIR, RG_CHUNK), F32),
            pltpu.VMEM((POOL_ROWS + ROW_TILE, D_POOL), F32),
        ],
        compiler_params=pltpu.CompilerParams(
            dimension_semantics=("arbitrary", "arbitrary"), vmem_limit_bytes=VMEM_LIMIT),
        name=f"mixer_a_l{layer}",
    )(x, gains, w_in, w_in, w_in, rgp, rgp, wax, wpm, pscale, sconv, sh, spool)


def _mixer_b_kernel(x_ref, g_ref, recg_ref, pooled_ref, wgr_ref, wgp_ref, wbr_ref, wbp_ref, wo_ref,
                    o_ref, hb_ref, *, n_chunks):
    n = pl.program_id(1)

    @pl.when(n == 0)
    def _():
        _norm_to_bf16(x_ref, g_ref[2:3, :], hb_ref)
        o_ref[...] = jnp.zeros(o_ref.shape, F32)

    hb = hb_ref[...]
    g_rg = _dot(hb, wgr_ref[...])
    g_pool = _dot(hb, wgp_ref[...])
    br = _dot(recg_ref[...], wbr_ref[...])
    bp = _dot(pooled_ref[...], wbp_ref[...])
    mix = (_sigmoid(g_rg) * br + _sigmoid(g_pool) * bp).astype(BF16)
    for c in range(D_MODEL // ACC_COLS):
        cs = slice(c * ACC_COLS, (c + 1) * ACC_COLS)
        o_ref[:, cs] += _dot(mix, wo_ref[:, cs])

    @pl.when(n == n_chunks - 1)
    def _():
        _residual_norm(x_ref, g_ref[3:4, :], o_ref, 1.0)


def _mixer_b(x, gains, recg, pooled, w_in, w_br_rg, w_br_pool, w_out, layer):
    rows = x.shape[0]
    n_chunks = D_MODEL // OUT_CHUNK
    g_rg_blk = (2 * D_RNN + D_POOL) // OUT_CHUNK
    g_pool_blk = g_rg_blk + n_chunks
    kern = functools.partial(_mixer_b_kernel, n_chunks=n_chunks)
    return pl.pallas_call(
        kern,
        out_shape=jax.ShapeDtypeStruct((rows, D_MODEL), F32),
        grid=(rows // ROW_TILE, n_chunks),
        in_specs=[
            pl.BlockSpec((ROW_TILE, D_MODEL), lambda i, n: (i, 0)),
            pl.BlockSpec((None, 6, D_MODEL), lambda i, n: (layer, 0, 0)),
            pl.BlockSpec((ROW_TILE, D_RNN), lambda i, n: (i, 0)),
            pl.BlockSpec((ROW_TILE, D_POOL), lambda i, n: (i, 0)),
            pl.BlockSpec((D_MODEL, OUT_CHUNK), lambda i, n: (0, g_rg_blk + n)),
            pl.BlockSpec((D_MODEL, OUT_CHUNK), lambda i, n: (0, g_pool_blk + n)),
            pl.BlockSpec((D_RNN, OUT_CHUNK), lambda i, n: (0, n)),
            pl.BlockSpec((D_POOL, OUT_CHUNK), lambda i, n: (0, n)),
            pl.BlockSpec((OUT_CHUNK, D_MODEL), lambda i, n: (n, 0)),
        ],
        out_specs=pl.BlockSpec((ROW_TILE, D_MODEL), lambda i, n: (i, 0)),
        scratch_shapes=[pltpu.VMEM((ROW_TILE, D_MODEL), BF16)],
        compiler_params=pltpu.CompilerParams(
            dimension_semantics=("arbitrary", "arbitrary"), vmem_limit_bytes=VMEM_LIMIT),
        name=f"mixer_b_l{layer}",
    )(x, gains, recg, pooled, w_in, w_in, w_br_rg, w_br_pool, w_out)


FRAMES = ROW_TILE // SUBLANES


def _to_time_major_kernel(xp_ref, xs_ref, o_ref, *, n_prompt_tiles):
    i = pl.program_id(0)

    def gather(src_ref):
        for t in range(FRAMES):
            o_ref[t * SUBLANES:(t + 1) * SUBLANES, :] = src_ref[:, t, :]

    @pl.when(i < n_prompt_tiles)
    def _():
        gather(xp_ref)

    @pl.when(i >= n_prompt_tiles)
    def _():
        gather(xs_ref)


def _to_time_major(x_prompt, x_sample, n_prompt_tiles):
    n_tiles = n_prompt_tiles + 1
    return pl.pallas_call(
        functools.partial(_to_time_major_kernel, n_prompt_tiles=n_prompt_tiles),
        out_shape=jax.ShapeDtypeStruct((n_tiles * ROW_TILE, D_MODEL), F32),
        grid=(n_tiles,),
        in_specs=[
            pl.BlockSpec((SUBLANES, FRAMES, D_MODEL), lambda i: (0, jnp.minimum(i, n_prompt_tiles - 1), 0)),
            pl.BlockSpec((SUBLANES, FRAMES, D_MODEL), lambda i: (0, 0, 0)),
        ],
        out_specs=pl.BlockSpec((ROW_TILE, D_MODEL), lambda i: (i, 0)),
        compiler_params=pltpu.CompilerParams(dimension_semantics=("arbitrary",), vmem_limit_bytes=VMEM_LIMIT),
        name="to_time_major",
    )(x_prompt, x_sample)


def _to_batch_major_kernel(x_ref, yp_ref, ys_ref, *, n_prompt_tiles):
    i = pl.program_id(0)

    def scatter(dst_ref):
        for t in range(FRAMES):
            dst_ref[:, t, :] = x_ref[t * SUBLANES:(t + 1) * SUBLANES, :]

    @pl.when(i < n_prompt_tiles)
    def _():
        scatter(yp_ref)

    @pl.when(i >= n_prompt_tiles)
    def _():
        scatter(ys_ref)


def _to_batch_major(x, n_prompt_tiles, seq):
    n_tiles = n_prompt_tiles + 1
    return pl.pallas_call(
        functools.partial(_to_batch_major_kernel, n_prompt_tiles=n_prompt_tiles),
        out_shape=(
            jax.ShapeDtypeStruct((SUBLANES, seq, D_MODEL), F32),
            jax.ShapeDtypeStruct((SUBLANES, FRAMES, D_MODEL), F32),
        ),
        grid=(n_tiles,),
        in_specs=[pl.BlockSpec((ROW_TILE, D_MODEL), lambda i: (i, 0))],
        out_specs=(
            pl.BlockSpec((SUBLANES, FRAMES, D_MODEL), lambda i: (0, jnp.minimum(i, n_prompt_tiles - 1), 0)),
            pl.BlockSpec((SUBLANES, FRAMES, D_MODEL), lambda i: (0, 0, 0)),
        ),
        compiler_params=pltpu.CompilerParams(dimension_semantics=("arbitrary",), vmem_limit_bytes=VMEM_LIMIT),
        name="to_batch_major",
    )(x)


def kernel(x_prompt, x_sample, state_conv, state_h, state_pool, norm_gains, w_ffn_in, w_ffn_out, w_in,
           conv_w, conv_b, w_rg_a, b_rg_a, w_rg_x, b_rg_x, lru_param, w_pool_mix, pool_scale,
           w_br_rg, w_br_pool, w_out):
    depth = norm_gains.shape[0]
    batch, seq, _ = x_prompt.shape
    dec_batch = x_sample.shape[0]
    assert batch == SUBLANES and dec_batch == SUBLANES
    n_prompt_rows = batch * seq
    assert n_prompt_rows % ROW_TILE == 0 and x_sample.shape[1] * dec_batch == ROW_TILE
    n_prompt_tiles = n_prompt_rows // ROW_TILE

    x = _to_time_major(x_prompt, x_sample, n_prompt_tiles)

    ffn_in_b = w_ffn_in[0, 0].astype(BF16)
    ffn_out_b = w_ffn_out[0, 0].astype(BF16)
    wpm_b = w_pool_mix.astype(BF16)
    wax_b = jnp.concatenate([w_rg_a, w_rg_x], axis=-1).astype(BF16)
    rgp = jnp.concatenate(
        [conv_w, conv_b[:, None], b_rg_a[:, None], b_rg_x[:, None], lru_param[:, None]], axis=1)
    pscale = pool_scale[:, None, :]
    sconv = jnp.swapaxes(state_conv, 1, 2).reshape(depth, CONV_ROWS, D_RNN)
    spool = jnp.swapaxes(state_pool, 1, 2).reshape(depth, POOL_ROWS, D_POOL)

    convs, hs, pools = [], [], []
    for l in range(depth):
        x, (w_in_b, w_br_rg_b, w_br_pool_b, w_out_b, ffn_in_b, ffn_out_b) = _ffn(
            x, norm_gains, ffn_in_b, ffn_out_b, l, 0,
            casts=[(w_in, (l,)), (w_br_rg, (l,)), (w_br_pool, (l,)), (w_out, (l,)),
                   (w_ffn_in, (l, 1)), (w_ffn_out, (l, 1))])
        recg, pooled, conv_o, h_o, pool_o = _mixer_a(
            x, norm_gains, w_in_b, rgp, wax_b, wpm_b, pscale, sconv, state_h, spool, l, n_prompt_tiles)
        x = _mixer_b(x, norm_gains, recg, pooled, w_in_b, w_br_rg_b, w_br_pool_b, w_out_b, l)
        next_ffn = [(w_ffn_in, (l + 1, 0)), (w_ffn_out, (l + 1, 0))] if l + 1 < depth else []
        x, next_b = _ffn(x, norm_gains, ffn_in_b, ffn_out_b, l, 1, casts=next_ffn)
        if next_b:
            ffn_in_b, ffn_out_b = next_b
        convs.append(jnp.swapaxes(conv_o, 1, 2).reshape(2, CONV_ROWS, D_RNN))
        hs.append(jnp.swapaxes(h_o, 1, 2).reshape(2, SUBLANES, D_RNN))
        pools.append(pool_o)

    y_prompt, y_sample = _to_batch_major(x, n_prompt_tiles, seq)
    conv_all = jnp.stack(convs)
    h_all = jnp.stack(hs)
    pool_all = jnp.stack(pools)

    def unroll_state(s, which, frames):
        s = s[:, which]
        return jnp.swapaxes(s.reshape(depth, frames, SUBLANES, s.shape[-1]), 1, 2)

    return (
        y_prompt,
        y_sample,
        unroll_state(conv_all, 0, CONV_W - 1),
        h_all[:, 0],
        unroll_state(pool_all, 0, POOL_MAX - 1),
        unroll_state(conv_all, 1, CONV_W - 1),
        h_all[:, 1],
        unroll_state(pool_all, 1, POOL_MAX - 1),
    )
```

```python
import functools
import math

import jax
import jax.numpy as jnp
from jax import lax
from jax.experimental import pallas as pl
from jax.experimental.pallas import tpu as pltpu

F32 = jnp.float32
BF16 = jnp.bfloat16

D_MODEL = 2048
D_FF = 3 * D_MODEL
D_RNN = D_MODEL
D_POOL = D_MODEL // 2
N_RG_HEADS = 16
RG_BLOCK = D_RNN // N_RG_HEADS
CONV_W = 4
LRU_C = 8.0
POOL_WINDOWS = (2, 4, 8, 16)
POOL_GROUP = D_POOL // len(POOL_WINDOWS)
POOL_MAX = 16
EPS = 1e-6

SUBLANES = 8
MXU_COLS = 256
ROW_TILE = 512
FFN_CHUNK = 1024
RG_CHUNK = 512
OUT_CHUNK = 512
NORM_ROWS = 32
ACC_COLS = 512
VMEM_LIMIT = 56 * 1024 * 1024
MIXER_A_VMEM_LIMIT = 60 * 1024 * 1024

CONV_ROWS = (CONV_W - 1) * SUBLANES
POOL_ROWS = (POOL_MAX - 1) * SUBLANES
N_RG = D_RNN // RG_CHUNK
RG_HEADS = RG_CHUNK // RG_BLOCK
PAIR = 2 * SUBLANES
N_GROUPS = 4


def _rms(x, g):
    ms = jnp.mean(x * x, axis=-1, keepdims=True)
    return x * lax.rsqrt(ms + EPS) * g


def _norm_rows(x_ref, g, dst_ref, lo, hi):
    for r in range(lo, hi, NORM_ROWS):
        rows = slice(r, r + NORM_ROWS)
        dst_ref[rows, :] = _rms(x_ref[rows, :], g).astype(BF16)


def _norm_to_bf16(x_ref, g, dst_ref):
    _norm_rows(x_ref, g, dst_ref, 0, x_ref.shape[0])


def _residual_norm(x_ref, g, o_ref, scale):
    for r in range(0, x_ref.shape[0], NORM_ROWS):
        rows = slice(r, r + NORM_ROWS)
        y = _rms(o_ref[rows, :], g)
        if scale != 1.0:
            y = scale * y
        o_ref[rows, :] = x_ref[rows, :] + y


def _sigmoid(x):
    return 1.0 / (1.0 + jnp.exp(-x))


def _gelu_tanh(x):
    c = math.sqrt(2.0 / math.pi)
    return x * (0.5 * (1.0 + jnp.tanh(c * (x + 0.044715 * (x * x * x)))))


def _dot(a, b):
    return jnp.dot(a, b, preferred_element_type=F32)


def _interleave(matmul_pieces, vector_groups):
    for k in range(max(len(matmul_pieces), len(vector_groups))):
        if k < len(matmul_pieces):
            matmul_pieces[k]()
        if k < len(vector_groups):
            vector_groups[k]()


def _ffn_kernel(x_ref, g_ref, wg_ref, wu_ref, wo_ref, *rest, pre, post, n_chunks, n_casts):
    cast_src = rest[:n_casts]
    o_ref = rest[n_casts]
    cast_dst = rest[n_casts + 1:2 * n_casts + 1]
    xn_ref = rest[2 * n_casts + 1]
    f = pl.program_id(1)

    for src, dst in zip(cast_src, cast_dst):
        dst[...] = src[...].astype(BF16)

    @pl.when(f == 0)
    def _():
        _norm_to_bf16(x_ref, g_ref[pre:pre + 1, :], xn_ref)
        o_ref[...] = jnp.zeros(o_ref.shape, F32)

    xn = xn_ref[...]
    gate = _dot(xn, wg_ref[...])
    up = _dot(xn, wu_ref[...])
    hid = (gate * _sigmoid(gate) * up).astype(BF16)
    for n in range(D_MODEL // ACC_COLS):
        cs = slice(n * ACC_COLS, (n + 1) * ACC_COLS)
        o_ref[:, cs] += _dot(hid, wo_ref[:, cs])

    @pl.when(f == n_chunks - 1)
    def _():
        _residual_norm(x_ref, g_ref[post:post + 1, :], o_ref, 0.5)


BF16_TILE_ROWS = 16


def _ffn(x, gains, w_in, w_out, layer, which, casts=()):
    rows = x.shape[0]
    n_chunks = D_FF // FFN_CHUNK
    n_tiles = rows // ROW_TILE
    n_steps = n_tiles * n_chunks

    cast_in_specs, cast_out_specs, cast_out_shapes = [], [], []
    for src, lead in casts:
        m_rows, m_cols = src.shape[-2:]
        per_step = -(-m_rows // n_steps)
        blk_rows = -(-per_step // BF16_TILE_ROWS) * BF16_TILE_ROWS
        assert m_rows % blk_rows == 0
        n_blk = m_rows // blk_rows

        def blk(i, f, n_blk=n_blk):
            return jnp.minimum(i * n_chunks + f, n_blk - 1)

        cast_in_specs.append(pl.BlockSpec(
            (None,) * len(lead) + (blk_rows, m_cols), lambda i, f, lead=lead, blk=blk: lead + (blk(i, f), 0)))
        cast_out_specs.append(pl.BlockSpec((blk_rows, m_cols), lambda i, f, blk=blk: (blk(i, f), 0)))
        cast_out_shapes.append(jax.ShapeDtypeStruct((m_rows, m_cols), BF16))

    kern = functools.partial(_ffn_kernel, pre=4 * which, post=4 * which + 1, n_chunks=n_chunks,
                             n_casts=len(casts))
    outs = pl.pallas_call(
        kern,
        out_shape=[jax.ShapeDtypeStruct((rows, D_MODEL), F32)] + cast_out_shapes,
        grid=(n_tiles, n_chunks),
        in_specs=[
            pl.BlockSpec((ROW_TILE, D_MODEL), lambda i, f: (i, 0)),
            pl.BlockSpec((None, 6, D_MODEL), lambda i, f: (layer, 0, 0)),
            pl.BlockSpec((D_MODEL, FFN_CHUNK), lambda i, f: (0, f)),
            pl.BlockSpec((D_MODEL, FFN_CHUNK), lambda i, f: (0, n_chunks + f)),
            pl.BlockSpec((FFN_CHUNK, D_MODEL), lambda i, f: (f, 0)),
        ] + cast_in_specs,
        out_specs=[pl.BlockSpec((ROW_TILE, D_MODEL), lambda i, f: (i, 0))] + cast_out_specs,
        scratch_shapes=[pltpu.VMEM((ROW_TILE, D_MODEL), BF16)],
        compiler_params=pltpu.CompilerParams(
            dimension_semantics=("arbitrary", "arbitrary"), vmem_limit_bytes=VMEM_LIMIT),
        name=f"ffn_l{layer}_{which}",
    )(x, gains, w_in, w_in, w_out, *[src for src, _ in casts])
    return outs[0], list(outs[1:])


N_STEPS = N_RG + 2


def _project_pieces(hb_ref, par, wxr_ref, wy_ref, zc_ref, y_ref, j):
    def piece(q):
        def run():
            hb = hb_ref[par]
            cs = slice((q % 2) * MXU_COLS, (q % 2 + 1) * MXU_COLS)
            if q < 2:
                zc_ref[CONV_ROWS:, cs] = _dot(hb, wxr_ref[:, cs])
            else:
                y_ref[j % 2][:, cs] = _dot(hb, wy_ref[:, cs])
        return run
    return [piece(q) for q in range(2 * RG_CHUNK // MXU_COLS)]


def _pool_project_pieces(hb_ref, par, wp_ref, zp_ref):
    def piece(q):
        def run():
            cs = slice(q * MXU_COLS, (q + 1) * MXU_COLS)
            zp_ref[POOL_ROWS:, cs] = _dot(hb_ref[par], wp_ref[:, cs])
        return run
    return [piece(q) for q in range(D_POOL // MXU_COLS)]


def _conv(j, seq, rgp_ref, zc_ref, xc_ref, xcb_ref, cpre_ref, convo_ref):
    tm = ROW_TILE
    zc = zc_ref
    zc[0:CONV_ROWS, :] = cpre_ref[j]
    xc = rgp_ref[4:5, :]
    for k in range(CONV_W):
        xc = xc + zc[k * SUBLANES:k * SUBLANES + tm, :] * rgp_ref[k:k + 1, :]
    new_pre = zc[tm:tm + CONV_ROWS, :]
    cpre_ref[j] = new_pre
    convo_ref[seq, j] = new_pre
    xc_ref[j % 2][...] = xc
    xcb_ref[j % 2][...] = xc.astype(BF16)


def _recur_groups(j, seq, rgp_ref, wax_ref, xcb_ref, xc_ref, y_ref, hst_ref, cst_ref, recg_ref, ho_ref):
    tm = ROW_TILE
    carry = {}

    def setup():
        carry["ri"] = [_dot(xcb_ref[j % 2][:, hh * RG_BLOCK:(hh + 1) * RG_BLOCK], wax_ref[hh])
                       for hh in range(RG_HEADS)]
        neg_lam = -rgp_ref[7:8, :]
        softplus = jnp.maximum(neg_lam, 0.0) + jnp.log1p(jnp.exp(-jnp.abs(neg_lam)))
        cst_ref[0] = jnp.broadcast_to(rgp_ref[5:6, :], (PAIR, RG_CHUNK))
        cst_ref[1] = jnp.broadcast_to(rgp_ref[6:7, :], (PAIR, RG_CHUNK))
        cst_ref[2] = jnp.broadcast_to(-LRU_C * softplus, (PAIR, RG_CHUNK))
        carry["h"] = hst_ref[j]

    def pair(r0):
        rows = slice(r0, r0 + PAIR)
        a_parts, u_parts = [], []
        for hh in range(RG_HEADS):
            sl = slice(hh * RG_BLOCK, (hh + 1) * RG_BLOCK)
            ri = carry["ri"][hh][rows, :]
            r = _sigmoid(ri[:, :RG_BLOCK] + cst_ref[0, :, sl])
            ig = _sigmoid(ri[:, RG_BLOCK:] + cst_ref[1, :, sl])
            log_a = cst_ref[2, :, sl] * r
            a = jnp.exp(log_a)
            u_parts.append(jnp.sqrt(-jnp.tanh(log_a) * (a * a + 1.0)) * (ig * xc_ref[j % 2][rows, sl]))
            a_parts.append(a)
        a = jnp.concatenate(a_parts, axis=1)
        u = jnp.concatenate(u_parts, axis=1)
        h1 = a[:SUBLANES] * carry["h"] + u[:SUBLANES]
        h2 = a[SUBLANES:] * h1 + u[SUBLANES:]
        carry["h"] = h2
        rec = jnp.concatenate([h1, h2], axis=0)
        recg_ref[rows, :] = (rec * _gelu_tanh(y_ref[j % 2][rows, :])).astype(BF16)

    def group(k):
        def run():
            if k == 0:
                setup()
            span = tm // N_GROUPS
            for r0 in range(k * span, (k + 1) * span, PAIR):
                pair(r0)
            if k == N_GROUPS - 1:
                hst_ref[j] = carry["h"]
                ho_ref[seq, j] = carry["h"]
        return run

    return [group(k) for k in range(N_GROUPS)]


def _pool_groups(i, seq, n_prompt_tiles, wpm_ref, ps_ref, zp_ref, pooled_ref, poolo_ref):
    tm = ROW_TILE
    frames = tm // SUBLANES

    def group(g):
        def run():
            w = POOL_WINDOWS[g]
            frame = lax.shift_right_logical(lax.broadcasted_iota(jnp.int32, (tm, POOL_GROUP), 0), 3)
            seen = jnp.where(i >= n_prompt_tiles, POOL_MAX - 1, i * frames) + 1
            seen = (frame + seen).astype(F32)
            cs = slice(g * POOL_GROUP, (g + 1) * POOL_GROUP)
            s = zp_ref[POOL_ROWS - (w - 1) * SUBLANES:, cs]
            shift = SUBLANES
            while shift < w * SUBLANES:
                s = s[shift:, :] + s[:s.shape[0] - shift, :]
                shift *= 2
            mean = s / jnp.minimum(seen, float(w))
            pooled = (mean - zp_ref[POOL_ROWS:, cs]).astype(BF16)
            mixed = _dot(pooled, wpm_ref[g]) * ps_ref[:, cs]
            pooled_ref[:, cs] = mixed.astype(BF16)
            if g == len(POOL_WINDOWS) - 1:
                new_pre = zp_ref[tm:tm + POOL_ROWS, :]
                zp_ref[0:POOL_ROWS, :] = new_pre
                poolo_ref[seq] = new_pre
        return run

    return [group(g) for g in range(len(POOL_WINDOWS))]


def _mixer_a_kernel(x_ref, g_ref, wxr_ref, wy_ref, wp_ref, rgp_conv_ref, rgp_recur_ref, wax_ref, wbr_ref,
                    wpm_ref, ps_ref, sconv_ref, sh_ref, spool_ref,
                    br_ref, pooled_ref, convo_ref, ho_ref, poolo_ref,
                    hb_ref, cpre_ref, hst_ref, zc_ref, xc0_ref, xc1_ref, xcb0_ref, xcb1_ref,
                    y0_ref, y1_ref, cst_ref, recg_ref, zp_ref,
                    *, n_prompt_tiles):
    xc_ref = (xc0_ref, xc1_ref)
    xcb_ref = (xcb0_ref, xcb1_ref)
    y_ref = (y0_ref, y1_ref)
    i = pl.program_id(0)
    s = pl.program_id(1)
    seq = (i >= n_prompt_tiles).astype(jnp.int32)
    par = i % 2

    def project(j):
        return _project_pieces(hb_ref, par, wxr_ref, wy_ref, zc_ref, y_ref, j)

    def conv(j):
        _conv(j, seq, rgp_conv_ref, zc_ref, xc_ref, xcb_ref, cpre_ref, convo_ref)

    def recur(j):
        return _recur_groups(j, seq, rgp_recur_ref, wax_ref, xcb_ref, xc_ref, y_ref, hst_ref, cst_ref,
                             recg_ref, ho_ref)

    def norm_next():
        span = ROW_TILE // N_GROUPS
        return [functools.partial(_norm_rows, x_ref, g_ref[2:3, :], hb_ref.at[1 - par], k * span, (k + 1) * span)
                for k in range(N_GROUPS)]

    @pl.when(s == 0)
    def _():
        @pl.when(i == 0)
        def _():
            cpre_ref[...] = jnp.zeros(cpre_ref.shape, F32)
            hst_ref[...] = jnp.zeros(hst_ref.shape, F32)
            zp_ref[0:POOL_ROWS, :] = jnp.zeros((POOL_ROWS, D_POOL), F32)
            _norm_to_bf16(x_ref, g_ref[2:3, :], hb_ref.at[0])

        @pl.when(i == n_prompt_tiles)
        def _():
            for j in range(N_RG):
                cs = slice(j * RG_CHUNK, (j + 1) * RG_CHUNK)
                cpre_ref[j] = sconv_ref[:, cs]
                hst_ref[j] = sh_ref[:, cs]
            zp_ref[0:POOL_ROWS, :] = spool_ref[...]

        for piece in project(0):
            piece()
        conv(0)

    def branch_project(j):
        rec = recg_ref[...]
        for c in range(D_MODEL // ACC_COLS):
            cs = slice(c * ACC_COLS, (c + 1) * ACC_COLS)
            part = _dot(rec, wbr_ref[:, cs])
            if j == 0:
                br_ref[:, cs] = part
            else:
                br_ref[:, cs] += part

    for step in range(1, N_RG):
        @pl.when(s == step)
        def _(step=step):
            groups = recur(step - 1)
            groups[0]()
            _interleave(project(step), groups[1:])
            branch_project(step - 1)
            conv(step)

    @pl.when(s == N_RG)
    def _():
        groups = recur(N_RG - 1)
        groups[0]()
        _interleave(_pool_project_pieces(hb_ref, par, wp_ref, zp_ref), groups[1:])
        branch_project(N_RG - 1)

    @pl.when(s == N_RG + 1)
    def _():
        _interleave(_pool_groups(i, seq, n_prompt_tiles, wpm_ref, ps_ref, zp_ref, pooled_ref, poolo_ref),
                    norm_next())


def _mixer_a(x, gains, w_in, w_br_rg, rgp, wax, wpm, pscale, sconv, sh, spool, layer, n_prompt_tiles):
    rows = x.shape[0]
    n_tiles = rows // ROW_TILE
    pool_blk = (2 * D_RNN) // D_POOL

    def chunk(s, lag):
        return jnp.clip(s - lag, 0, N_RG - 1)

    def x_tile(i, s):
        return jnp.minimum(i + (s >= 1).astype(jnp.int32), n_tiles - 1)

    kern = functools.partial(_mixer_a_kernel, n_prompt_tiles=n_prompt_tiles)
    return pl.pallas_call(
        kern,
        out_shape=(
            jax.ShapeDtypeStruct((rows, D_MODEL), F32),
            jax.ShapeDtypeStruct((rows, D_POOL), BF16),
            jax.ShapeDtypeStruct((2, N_RG, CONV_ROWS, RG_CHUNK), F32),
            jax.ShapeDtypeStruct((2, N_RG, SUBLANES, RG_CHUNK), F32),
            jax.ShapeDtypeStruct((2, POOL_ROWS, D_POOL), F32),
        ),
        grid=(n_tiles, N_STEPS),
        in_specs=[
            pl.BlockSpec((ROW_TILE, D_MODEL), lambda i, s: (x_tile(i, s), 0)),
            pl.BlockSpec((None, 6, D_MODEL), lambda i, s: (layer, 0, 0)),
            pl.BlockSpec((D_MODEL, RG_CHUNK), lambda i, s: (0, chunk(s, 0))),
            pl.BlockSpec((D_MODEL, RG_CHUNK), lambda i, s: (0, N_RG + chunk(s, 0))),
            pl.BlockSpec((D_MODEL, D_POOL), lambda i, s: (0, pool_blk)),
            pl.BlockSpec((None, 8, RG_CHUNK), lambda i, s: (layer, 0, chunk(s, 0))),
            pl.BlockSpec((None, 8, RG_CHUNK), lambda i, s: (layer, 0, chunk(s, 1))),
            pl.BlockSpec((None, RG_HEADS, RG_BLOCK, 2 * RG_BLOCK), lambda i, s: (layer, chunk(s, 1), 0, 0)),
            pl.BlockSpec((RG_CHUNK, D_MODEL), lambda i, s: (chunk(s, 1), 0)),
            pl.BlockSpec((None, len(POOL_WINDOWS), POOL_GROUP, POOL_GROUP), lambda i, s: (layer, 0, 0, 0)),
            pl.BlockSpec((None, 1, D_POOL), lambda i, s: (layer, 0, 0)),
            pl.BlockSpec((None, CONV_ROWS, D_RNN), lambda i, s: (layer, 0, 0)),
            pl.BlockSpec((None, SUBLANES, D_RNN), lambda i, s: (layer, 0, 0)),
            pl.BlockSpec((None, POOL_ROWS, D_POOL), lambda i, s: (layer, 0, 0)),
        ],
        out_specs=(
            pl.BlockSpec((ROW_TILE, D_MODEL), lambda i, s: (i, 0)),
            pl.BlockSpec((ROW_TILE, D_POOL), lambda i, s: (i, 0)),
            pl.BlockSpec((2, N_RG, CONV_ROWS, RG_CHUNK), lambda i, s: (0, 0, 0, 0)),
            pl.BlockSpec((2, N_RG, SUBLANES, RG_CHUNK), lambda i, s: (0, 0, 0, 0)),
            pl.BlockSpec((2, POOL_ROWS, D_POOL), lambda i, s: (0, 0, 0)),
        ),
        scratch_shapes=[
            pltpu.VMEM((2, ROW_TILE, D_MODEL), BF16),
            pltpu.VMEM((N_RG, CONV_ROWS, RG_CHUNK), F32),
            pltpu.VMEM((N_RG, SUBLANES, RG_CHUNK), F32),
            pltpu.VMEM((CONV_ROWS + ROW_TILE, RG_CHUNK), F32),
            pltpu.VMEM((ROW_TILE, RG_CHUNK), F32),
            pltpu.VMEM((ROW_TILE, RG_CHUNK), F32),
            pltpu.VMEM((ROW_TILE, RG_CHUNK), BF16),
            pltpu.VMEM((ROW_TILE, RG_CHUNK), BF16),
            pltpu.VMEM((ROW_TILE, RG_CHUNK), F32),
            pltpu.VMEM((ROW_TILE, RG_CHUNK), F32),
            pltpu.VMEM((3, PAIR, RG_CHUNK), F32),
            pltpu.VMEM((ROW_TILE, RG_CHUNK), BF16),
            pltpu.VMEM((POOL_ROWS + ROW_TILE, D_POOL), F32),
        ],
        compiler_params=pltpu.CompilerParams(
            dimension_semantics=("arbitrary", "arbitrary"), vmem_limit_bytes=MIXER_A_VMEM_LIMIT),
        name=f"mixer_a_l{layer}",
    )(x, gains, w_in, w_in, w_in, rgp, rgp, wax, w_br_rg, wpm, pscale, sconv, sh, spool)


def _mixer_b_kernel(x_ref, g_ref, br_ref, pooled_ref, wgr_ref, wgp_ref, wbp_ref, wo_ref,
                    o_ref, hb_ref, *, n_chunks):
    n = pl.program_id(1)

    @pl.when(n == 0)
    def _():
        _norm_to_bf16(x_ref, g_ref[2:3, :], hb_ref)
        o_ref[...] = jnp.zeros(o_ref.shape, F32)

    hb = hb_ref[...]
    g_rg = _dot(hb, wgr_ref[...])
    g_pool = _dot(hb, wgp_ref[...])
    bp = _dot(pooled_ref[...], wbp_ref[...])
    mix = (_sigmoid(g_rg) * br_ref[...] + _sigmoid(g_pool) * bp).astype(BF16)
    for c in range(D_MODEL // ACC_COLS):
        cs = slice(c * ACC_COLS, (c + 1) * ACC_COLS)
        o_ref[:, cs] += _dot(mix, wo_ref[:, cs])

    @pl.when(n == n_chunks - 1)
    def _():
        _residual_norm(x_ref, g_ref[3:4, :], o_ref, 1.0)


def _mixer_b(x, gains, br, pooled, w_in, w_br_pool, w_out, layer):
    rows = x.shape[0]
    n_chunks = D_MODEL // OUT_CHUNK
    g_rg_blk = (2 * D_RNN + D_POOL) // OUT_CHUNK
    g_pool_blk = g_rg_blk + n_chunks
    kern = functools.partial(_mixer_b_kernel, n_chunks=n_chunks)
    return pl.pallas_call(
        kern,
        out_shape=jax.ShapeDtypeStruct((rows, D_MODEL), F32),
        grid=(rows // ROW_TILE, n_chunks),
        in_specs=[
            pl.BlockSpec((ROW_TILE, D_MODEL), lambda i, n: (i, 0)),
            pl.BlockSpec((None, 6, D_MODEL), lambda i, n: (layer, 0, 0)),
            pl.BlockSpec((ROW_TILE, OUT_CHUNK), lambda i, n: (i, n)),
            pl.BlockSpec((ROW_TILE, D_POOL), lambda i, n: (i, 0)),
            pl.BlockSpec((D_MODEL, OUT_CHUNK), lambda i, n: (0, g_rg_blk + n)),
            pl.BlockSpec((D_MODEL, OUT_CHUNK), lambda i, n: (0, g_pool_blk + n)),
            pl.BlockSpec((D_POOL, OUT_CHUNK), lambda i, n: (0, n)),
            pl.BlockSpec((OUT_CHUNK, D_MODEL), lambda i, n: (n, 0)),
        ],
        out_specs=pl.BlockSpec((ROW_TILE, D_MODEL), lambda i, n: (i, 0)),
        scratch_shapes=[pltpu.VMEM((ROW_TILE, D_MODEL), BF16)],
        compiler_params=pltpu.CompilerParams(
            dimension_semantics=("arbitrary", "arbitrary"), vmem_limit_bytes=VMEM_LIMIT),
        name=f"mixer_b_l{layer}",
    )(x, gains, br, pooled, w_in, w_in, w_br_pool, w_out)


FRAMES = ROW_TILE // SUBLANES


def _to_time_major_kernel(xp_ref, xs_ref, o_ref, *, n_prompt_tiles):
    i = pl.program_id(0)

    def gather(src_ref):
        for t in range(FRAMES):
            o_ref[t * SUBLANES:(t + 1) * SUBLANES, :] = src_ref[:, t, :]

    @pl.when(i < n_prompt_tiles)
    def _():
        gather(xp_ref)

    @pl.when(i >= n_prompt_tiles)
    def _():
        gather(xs_ref)


def _to_time_major(x_prompt, x_sample, n_prompt_tiles):
    n_tiles = n_prompt_tiles + 1
    return pl.pallas_call(
        functools.partial(_to_time_major_kernel, n_prompt_tiles=n_prompt_tiles),
        out_shape=jax.ShapeDtypeStruct((n_tiles * ROW_TILE, D_MODEL), F32),
        grid=(n_tiles,),
        in_specs=[
            pl.BlockSpec((SUBLANES, FRAMES, D_MODEL), lambda i: (0, jnp.minimum(i, n_prompt_tiles - 1), 0)),
            pl.BlockSpec((SUBLANES, FRAMES, D_MODEL), lambda i: (0, 0, 0)),
        ],
        out_specs=pl.BlockSpec((ROW_TILE, D_MODEL), lambda i: (i, 0)),
        compiler_params=pltpu.CompilerParams(dimension_semantics=("arbitrary",), vmem_limit_bytes=VMEM_LIMIT),
        name="to_time_major",
    )(x_prompt, x_sample)


def _to_batch_major_kernel(x_ref, yp_ref, ys_ref, *, n_prompt_tiles):
    i = pl.program_id(0)

    def scatter(dst_ref):
        for t in range(FRAMES):
            dst_ref[:, t, :] = x_ref[t * SUBLANES:(t + 1) * SUBLANES, :]

    @pl.when(i < n_prompt_tiles)
    def _():
        scatter(yp_ref)

    @pl.when(i >= n_prompt_tiles)
    def _():
        scatter(ys_ref)


def _to_batch_major(x, n_prompt_tiles, seq):
    n_tiles = n_prompt_tiles + 1
    return pl.pallas_call(
        functools.partial(_to_batch_major_kernel, n_prompt_tiles=n_prompt_tiles),
        out_shape=(
            jax.ShapeDtypeStruct((SUBLANES, seq, D_MODEL), F32),
            jax.ShapeDtypeStruct((SUBLANES, FRAMES, D_MODEL), F32),
        ),
        grid=(n_tiles,),
        in_specs=[pl.BlockSpec((ROW_TILE, D_MODEL), lambda i: (i, 0))],
        out_specs=(
            pl.BlockSpec((SUBLANES, FRAMES, D_MODEL), lambda i: (0, jnp.minimum(i, n_prompt_tiles - 1), 0)),
            pl.BlockSpec((SUBLANES, FRAMES, D_MODEL), lambda i: (0, 0, 0)),
        ),
        compiler_params=pltpu.CompilerParams(dimension_semantics=("arbitrary",), vmem_limit_bytes=VMEM_LIMIT),
        name="to_batch_major",
    )(x)


def kernel(x_prompt, x_sample, state_conv, state_h, state_pool, norm_gains, w_ffn_in, w_ffn_out, w_in,
           conv_w, conv_b, w_rg_a, b_rg_a, w_rg_x, b_rg_x, lru_param, w_pool_mix, pool_scale,
           w_br_rg, w_br_pool, w_out):
    depth = norm_gains.shape[0]
    batch, seq, _ = x_prompt.shape
    dec_batch = x_sample.shape[0]
    assert batch == SUBLANES and dec_batch == SUBLANES
    n_prompt_rows = batch * seq
    assert n_prompt_rows % ROW_TILE == 0 and x_sample.shape[1] * dec_batch == ROW_TILE
    n_prompt_tiles = n_prompt_rows // ROW_TILE

    x = _to_time_major(x_prompt, x_sample, n_prompt_tiles)

    ffn_in_b = w_ffn_in[0, 0].astype(BF16)
    ffn_out_b = w_ffn_out[0, 0].astype(BF16)
    wpm_b = w_pool_mix.astype(BF16)
    wax_b = jnp.concatenate([w_rg_a, w_rg_x], axis=-1).astype(BF16)
    rgp = jnp.concatenate(
        [conv_w, conv_b[:, None], b_rg_a[:, None], b_rg_x[:, None], lru_param[:, None]], axis=1)
    pscale = pool_scale[:, None, :]
    sconv = jnp.swapaxes(state_conv, 1, 2).reshape(depth, CONV_ROWS, D_RNN)
    spool = jnp.swapaxes(state_pool, 1, 2).reshape(depth, POOL_ROWS, D_POOL)

    convs, hs, pools = [], [], []
    for l in range(depth):
        x, (w_in_b, w_br_rg_b, w_br_pool_b, w_out_b, ffn_in_b, ffn_out_b) = _ffn(
            x, norm_gains, ffn_in_b, ffn_out_b, l, 0,
            casts=[(w_in, (l,)), (w_br_rg, (l,)), (w_br_pool, (l,)), (w_out, (l,)),
                   (w_ffn_in, (l, 1)), (w_ffn_out, (l, 1))])
        br, pooled, conv_o, h_o, pool_o = _mixer_a(
            x, norm_gains, w_in_b, w_br_rg_b, rgp, wax_b, wpm_b, pscale, sconv, state_h, spool, l,
            n_prompt_tiles)
        x = _mixer_b(x, norm_gains, br, pooled, w_in_b, w_br_pool_b, w_out_b, l)
        next_ffn = [(w_ffn_in, (l + 1, 0)), (w_ffn_out, (l + 1, 0))] if l + 1 < depth else []
        x, next_b = _ffn(x, norm_gains, ffn_in_b, ffn_out_b, l, 1, casts=next_ffn)
        if next_b:
            ffn_in_b, ffn_out_b = next_b
        convs.append(jnp.swapaxes(conv_o, 1, 2).reshape(2, CONV_ROWS, D_RNN))
        hs.append(jnp.swapaxes(h_o, 1, 2).reshape(2, SUBLANES, D_RNN))
        pools.append(pool_o)

    y_prompt, y_sample = _to_batch_major(x, n_prompt_tiles, seq)
    conv_all = jnp.stack(convs)
    h_all = jnp.stack(hs)
    pool_all = jnp.stack(pools)

    def unroll_state(s, which, frames):
        s = s[:, which]
        return jnp.swapaxes(s.reshape(depth, frames, SUBLANES, s.shape[-1]), 1, 2)

    return (
        y_prompt,
        y_sample,
        unroll_state(conv_all, 0, CONV_W - 1),
        h_all[:, 0],
        unroll_state(pool_all, 0, POOL_MAX - 1),
        unroll_state(conv_all, 1, CONV_W - 1),
        h_all[:, 1],
        unroll_state(pool_all, 1, POOL_MAX - 1),
    )
```

```python
import functools
import math

import jax
import jax.numpy as jnp
from jax import lax
from jax.experimental import pallas as pl
from jax.experimental.pallas import tpu as pltpu

F32 = jnp.float32
BF16 = jnp.bfloat16

D_MODEL = 2048
D_FF = 3 * D_MODEL
D_RNN = D_MODEL
D_POOL = D_MODEL // 2
N_RG_HEADS = 16
RG_BLOCK = D_RNN // N_RG_HEADS
CONV_W = 4
LRU_C = 8.0
POOL_WINDOWS = (2, 4, 8, 16)
POOL_GROUP = D_POOL // len(POOL_WINDOWS)
POOL_MAX = 16
EPS = 1e-6

SUBLANES = 8
MXU_COLS = 256
ROW_TILE = 512
FFN_CHUNK = 1024
RG_CHUNK = 512
OUT_CHUNK = 512
NORM_ROWS = 32
ACC_COLS = 512
VMEM_LIMIT = 56 * 1024 * 1024

CONV_ROWS = (CONV_W - 1) * SUBLANES
POOL_ROWS = (POOL_MAX - 1) * SUBLANES
N_RG = D_RNN // RG_CHUNK
RG_HEADS = RG_CHUNK // RG_BLOCK
PAIR = 2 * SUBLANES
N_GROUPS = 4


def _rms(x, g):
    ms = jnp.mean(x * x, axis=-1, keepdims=True)
    return x * lax.rsqrt(ms + EPS) * g


def _norm_rows(x_ref, g, dst_refs, lo, hi):
    for r in range(lo, hi, NORM_ROWS):
        rows = slice(r, r + NORM_ROWS)
        y = _rms(x_ref[rows, :], g).astype(BF16)
        for dst_ref in dst_refs:
            dst_ref[rows, :] = y


def _norm_to_bf16(x_ref, g, *dst_refs):
    _norm_rows(x_ref, g, dst_refs, 0, x_ref.shape[0])


def _residual_norm(x_ref, g, o_ref):
    for r in range(0, x_ref.shape[0], NORM_ROWS):
        rows = slice(r, r + NORM_ROWS)
        o_ref[rows, :] = x_ref[rows, :] + _rms(o_ref[rows, :], g)


def _sigmoid(x):
    return 1.0 / (1.0 + jnp.exp(-x))


def _gelu_tanh(x):
    c = math.sqrt(2.0 / math.pi)
    half = 0.5 * x
    return half + half * jnp.tanh(x * (c + (c * 0.044715) * (x * x)))


def _sqrt_nonneg(z):
    return jnp.where(z > 0.0, z * lax.rsqrt(z), 0.0)


def _dot(a, b):
    return jnp.dot(a, b, preferred_element_type=F32)


def _interleave(matmul_pieces, vector_groups):
    for k in range(max(len(matmul_pieces), len(vector_groups))):
        if k < len(matmul_pieces):
            matmul_pieces[k]()
        if k < len(vector_groups):
            vector_groups[k]()


def _ffn_kernel(x_ref, g_ref, wg_ref, wu_ref, wo_ref, *rest, pre, post, n_chunks, n_casts):
    cast_src = rest[:n_casts]
    o_ref = rest[n_casts]
    cast_dst = rest[n_casts + 1:2 * n_casts + 1]
    xn_ref = rest[2 * n_casts + 1]
    f = pl.program_id(1)

    for src, dst in zip(cast_src, cast_dst):
        dst[...] = src[...].astype(BF16)

    def chunk(first):
        xn = xn_ref[...]
        gate = _dot(xn, wg_ref[...])
        up = _dot(xn, wu_ref[...])
        hid = (gate * _sigmoid(gate) * up).astype(BF16)
        for n in range(D_MODEL // ACC_COLS):
            cs = slice(n * ACC_COLS, (n + 1) * ACC_COLS)
            part = _dot(hid, wo_ref[:, cs])
            if first:
                o_ref[:, cs] = part
            else:
                o_ref[:, cs] += part

    @pl.when(f == 0)
    def _():
        _norm_to_bf16(x_ref, g_ref[pre:pre + 1, :], xn_ref)
        chunk(True)

    @pl.when(f > 0)
    def _():
        chunk(False)

    @pl.when(f == n_chunks - 1)
    def _():
        _residual_norm(x_ref, 0.5 * g_ref[post:post + 1, :], o_ref)


BF16_TILE_ROWS = 16


def _ffn(x, gains, w_in, w_out, layer, which, casts=()):
    rows = x.shape[0]
    n_chunks = D_FF // FFN_CHUNK
    n_tiles = rows // ROW_TILE
    n_steps = n_tiles * n_chunks

    cast_in_specs, cast_out_specs, cast_out_shapes = [], [], []
    for src, lead in casts:
        m_rows, m_cols = src.shape[-2:]
        per_step = -(-m_rows // n_steps)
        blk_rows = -(-per_step // BF16_TILE_ROWS) * BF16_TILE_ROWS
        assert m_rows % blk_rows == 0
        n_blk = m_rows // blk_rows

        def blk(i, f, n_blk=n_blk):
            return jnp.minimum(i * n_chunks + f, n_blk - 1)

        cast_in_specs.append(pl.BlockSpec(
            (None,) * len(lead) + (blk_rows, m_cols), lambda i, f, lead=lead, blk=blk: lead + (blk(i, f), 0)))
        cast_out_specs.append(pl.BlockSpec((blk_rows, m_cols), lambda i, f, blk=blk: (blk(i, f), 0)))
        cast_out_shapes.append(jax.ShapeDtypeStruct((m_rows, m_cols), BF16))

    kern = functools.partial(_ffn_kernel, pre=4 * which, post=4 * which + 1, n_chunks=n_chunks,
                             n_casts=len(casts))
    outs = pl.pallas_call(
        kern,
        out_shape=[jax.ShapeDtypeStruct((rows, D_MODEL), F32)] + cast_out_shapes,
        grid=(n_tiles, n_chunks),
        in_specs=[
            pl.BlockSpec((ROW_TILE, D_MODEL), lambda i, f: (i, 0)),
            pl.BlockSpec((None, 6, D_MODEL), lambda i, f: (layer, 0, 0)),
            pl.BlockSpec((D_MODEL, FFN_CHUNK), lambda i, f: (0, f)),
            pl.BlockSpec((D_MODEL, FFN_CHUNK), lambda i, f: (0, n_chunks + f)),
            pl.BlockSpec((FFN_CHUNK, D_MODEL), lambda i, f: (f, 0)),
        ] + cast_in_specs,
        out_specs=[pl.BlockSpec((ROW_TILE, D_MODEL), lambda i, f: (i, 0))] + cast_out_specs,
        scratch_shapes=[pltpu.VMEM((ROW_TILE, D_MODEL), BF16)],
        compiler_params=pltpu.CompilerParams(
            dimension_semantics=("arbitrary", "arbitrary"), vmem_limit_bytes=VMEM_LIMIT),
        name=f"ffn_l{layer}_{which}",
    )(x, gains, w_in, w_in, w_out, *[src for src, _ in casts])
    return outs[0], list(outs[1:])


N_STEPS = N_RG + 2


def _project_pieces(hb_ref, par, wxr_ref, wy_ref, zc_ref, y_ref, j):
    def piece(q):
        def run():
            hb = hb_ref[par]
            cs = slice((q % 2) * MXU_COLS, (q % 2 + 1) * MXU_COLS)
            if q < 2:
                zc_ref[j % 2][CONV_ROWS:, cs] = _dot(hb, wxr_ref[:, cs])
            else:
                y_ref[j % 3][:, cs] = _dot(hb, wy_ref[:, cs])
        return run
    return [piece(q) for q in range(2 * RG_CHUNK // MXU_COLS)]


def _pool_project_pieces(hb_ref, par, wp_ref, zp_ref):
    def piece(q):
        def run():
            cs = slice(q * MXU_COLS, (q + 1) * MXU_COLS)
            zp_ref[POOL_ROWS:, cs] = _dot(hb_ref[par], wp_ref[:, cs])
        return run
    return [piece(q) for q in range(D_POOL // MXU_COLS)]


def _conv(j, seq, rgp_ref, zc_ref, xc_ref, xcb_ref, cpre_ref, convo_ref):
    tm = ROW_TILE
    zc = zc_ref[j % 2]
    zc[0:CONV_ROWS, :] = cpre_ref[j]
    xc = rgp_ref[4:5, :]
    for k in range(CONV_W):
        xc = xc + zc[k * SUBLANES:k * SUBLANES + tm, :] * rgp_ref[k:k + 1, :]
    new_pre = zc[tm:tm + CONV_ROWS, :]
    cpre_ref[j] = new_pre
    convo_ref[seq, j] = new_pre
    xc_ref[j % 2][...] = xc
    xcb_ref[...] = xc.astype(BF16)


def _gate_project(j, wax_ref, xcb_ref, ri_ref):
    for hh in range(RG_HEADS):
        sl = slice(hh * RG_BLOCK, (hh + 1) * RG_BLOCK)
        ri_ref[j % 2][hh] = _dot(xcb_ref[:, sl], wax_ref[hh])


def _recur_groups(j, seq, rgp_ref, ri_ref, xc_ref, y_ref, hst_ref, cst_ref, recg_ref, ho_ref):
    tm = ROW_TILE
    carry = {}

    def setup():
        neg_lam = -rgp_ref[7:8, :]
        softplus = jnp.maximum(neg_lam, 0.0) + jnp.log1p(jnp.exp(-jnp.abs(neg_lam)))
        cst_ref[0] = jnp.broadcast_to(rgp_ref[5:6, :], (PAIR, RG_CHUNK))
        cst_ref[1] = jnp.broadcast_to(rgp_ref[6:7, :], (PAIR, RG_CHUNK))
        cst_ref[2] = jnp.broadcast_to(-LRU_C * softplus, (PAIR, RG_CHUNK))
        carry["h"] = hst_ref[j]

    def pair(r0):
        rows = slice(r0, r0 + PAIR)
        a_parts, u_parts = [], []
        for hh in range(RG_HEADS):
            sl = slice(hh * RG_BLOCK, (hh + 1) * RG_BLOCK)
            ri = ri_ref[j % 2][hh, rows, :]
            r = _sigmoid(ri[:, :RG_BLOCK] + cst_ref[0, :, sl])
            ig = _sigmoid(ri[:, RG_BLOCK:] + cst_ref[1, :, sl])
            log_a = cst_ref[2, :, sl] * r
            a = jnp.exp(log_a)
            u_parts.append(_sqrt_nonneg(-jnp.tanh(log_a) * (a * a + 1.0)) * (ig * xc_ref[j % 2][rows, sl]))
            a_parts.append(a)
        a = jnp.concatenate(a_parts, axis=1)
        u = jnp.concatenate(u_parts, axis=1)
        h1 = a[:SUBLANES] * carry["h"] + u[:SUBLANES]
        h2 = a[SUBLANES:] * h1 + u[SUBLANES:]
        carry["h"] = h2
        rec = jnp.concatenate([h1, h2], axis=0)
        recg_ref[rows, :] = (rec * _gelu_tanh(y_ref[j % 3][rows, :])).astype(BF16)

    def group(k):
        def run():
            if k == 0:
                setup()
            span = tm // N_GROUPS
            for r0 in range(k * span, (k + 1) * span, PAIR):
                pair(r0)
            if k == N_GROUPS - 1:
                hst_ref[j] = carry["h"]
                ho_ref[seq, j] = carry["h"]
        return run

    return [group(k) for k in range(N_GROUPS)]


def _pool_groups(i, seq, n_prompt_tiles, wpm_ref, ps_ref, zp_ref, pooled_ref, poolo_ref):
    tm = ROW_TILE
    frames = tm // SUBLANES

    def group(g):
        def run():
            w = POOL_WINDOWS[g]
            frame = lax.shift_right_logical(lax.broadcasted_iota(jnp.int32, (tm, POOL_GROUP), 0), 3)
            seen = jnp.where(i >= n_prompt_tiles, POOL_MAX - 1, i * frames) + 1
            seen = (frame + seen).astype(F32)
            cs = slice(g * POOL_GROUP, (g + 1) * POOL_GROUP)
            s = zp_ref[POOL_ROWS - (w - 1) * SUBLANES:, cs]
            shift = SUBLANES
            while shift < w * SUBLANES:
                s = s[shift:, :] + s[:s.shape[0] - shift, :]
                shift *= 2
            mean = s / jnp.minimum(seen, float(w))
            pooled = (mean - zp_ref[POOL_ROWS:, cs]).astype(BF16)
            mixed = _dot(pooled, wpm_ref[g]) * ps_ref[:, cs]
            pooled_ref[:, cs] = mixed.astype(BF16)
            if g == len(POOL_WINDOWS) - 1:
                new_pre = zp_ref[tm:tm + POOL_ROWS, :]
                zp_ref[0:POOL_ROWS, :] = new_pre
                poolo_ref[seq] = new_pre
        return run

    return [group(g) for g in range(len(POOL_WINDOWS))]


def _mixer_a_kernel(x_ref, g_ref, wxr_ref, wy_ref, wp_ref, rgp_conv_ref, rgp_recur_ref, wax_ref, wpm_ref,
                    ps_ref, sconv_ref, sh_ref, spool_ref,
                    recg_ref, pooled_ref, hbn_ref, convo_ref, ho_ref, poolo_ref,
                    hb_ref, cpre_ref, hst_ref, zc0_ref, zc1_ref, xc0_ref, xc1_ref, xcb_ref, ri0_ref, ri1_ref,
                    y0_ref, y1_ref, y2_ref, cst_ref, zp_ref,
                    *, n_prompt_tiles):
    zc_ref = (zc0_ref, zc1_ref)
    xc_ref = (xc0_ref, xc1_ref)
    ri_ref = (ri0_ref, ri1_ref)
    y_ref = (y0_ref, y1_ref, y2_ref)
    i = pl.program_id(0)
    s = pl.program_id(1)
    seq = (i >= n_prompt_tiles).astype(jnp.int32)
    par = i % 2

    def project(j):
        return _project_pieces(hb_ref, par, wxr_ref, wy_ref, zc_ref, y_ref, j)

    def conv(j):
        return lambda: _conv(j, seq, rgp_conv_ref, zc_ref, xc_ref, xcb_ref, cpre_ref, convo_ref)

    def gate_project(j):
        return lambda: _gate_project(j, wax_ref, xcb_ref, ri_ref)

    def recur(j):
        return _recur_groups(j, seq, rgp_recur_ref, ri_ref, xc_ref, y_ref, hst_ref, cst_ref, recg_ref, ho_ref)

    def norm_next():
        span = ROW_TILE // N_GROUPS
        return [functools.partial(_norm_rows, x_ref, g_ref[2:3, :], (hb_ref.at[1 - par], hbn_ref),
                                  k * span, (k + 1) * span)
                for k in range(N_GROUPS)]

    @pl.when(s == 0)
    def _():
        @pl.when(i == 0)
        def _():
            cpre_ref[...] = jnp.zeros(cpre_ref.shape, F32)
            hst_ref[...] = jnp.zeros(hst_ref.shape, F32)
            zp_ref[0:POOL_ROWS, :] = jnp.zeros((POOL_ROWS, D_POOL), F32)
            _norm_to_bf16(x_ref, g_ref[2:3, :], hb_ref.at[0], hbn_ref)

        @pl.when(i == n_prompt_tiles)
        def _():
            for j in range(N_RG):
                cs = slice(j * RG_CHUNK, (j + 1) * RG_CHUNK)
                cpre_ref[j] = sconv_ref[:, cs]
                hst_ref[j] = sh_ref[:, cs]
            zp_ref[0:POOL_ROWS, :] = spool_ref[...]

        for piece in project(0):
            piece()

    def stages(pieces, conv_stage, groups, gate_stage):
        conv_stage()
        _interleave(pieces, groups)
        gate_stage()

    @pl.when(s == 1)
    def _():
        stages(project(1), conv(0), norm_next(), gate_project(0))

    for step in range(2, N_RG):
        @pl.when(s == step)
        def _(step=step):
            stages(project(step), conv(step - 1), recur(step - 2), gate_project(step - 1))

    @pl.when(s == N_RG)
    def _():
        stages(_pool_project_pieces(hb_ref, par, wp_ref, zp_ref), conv(N_RG - 1), recur(N_RG - 2),
               gate_project(N_RG - 1))

    @pl.when(s == N_RG + 1)
    def _():
        _interleave(_pool_groups(i, seq, n_prompt_tiles, wpm_ref, ps_ref, zp_ref, pooled_ref, poolo_ref),
                    recur(N_RG - 1))


def _mixer_a(x, gains, w_in, rgp, wax, wpm, pscale, sconv, sh, spool, layer, n_prompt_tiles):
    rows = x.shape[0]
    n_tiles = rows // ROW_TILE
    pool_blk = (2 * D_RNN) // D_POOL

    def chunk(s, lag):
        return jnp.clip(s - lag, 0, N_RG - 1)

    def x_tile(i, s):
        return jnp.minimum(i + (s >= 1).astype(jnp.int32), n_tiles - 1)

    kern = functools.partial(_mixer_a_kernel, n_prompt_tiles=n_prompt_tiles)
    return pl.pallas_call(
        kern,
        out_shape=(
            jax.ShapeDtypeStruct((rows, D_RNN), BF16),
            jax.ShapeDtypeStruct((rows, D_POOL), BF16),
            jax.ShapeDtypeStruct((rows, D_MODEL), BF16),
            jax.ShapeDtypeStruct((2, N_RG, CONV_ROWS, RG_CHUNK), F32),
            jax.ShapeDtypeStruct((2, N_RG, SUBLANES, RG_CHUNK), F32),
            jax.ShapeDtypeStruct((2, POOL_ROWS, D_POOL), F32),
        ),
        grid=(n_tiles, N_STEPS),
        in_specs=[
            pl.BlockSpec((ROW_TILE, D_MODEL), lambda i, s: (x_tile(i, s), 0)),
            pl.BlockSpec((None, 6, D_MODEL), lambda i, s: (layer, 0, 0)),
            pl.BlockSpec((D_MODEL, RG_CHUNK), lambda i, s: (0, chunk(s, 0))),
            pl.BlockSpec((D_MODEL, RG_CHUNK), lambda i, s: (0, N_RG + chunk(s, 0))),
            pl.BlockSpec((D_MODEL, D_POOL), lambda i, s: (0, pool_blk)),
            pl.BlockSpec((None, 8, RG_CHUNK), lambda i, s: (layer, 0, chunk(s, 1))),
            pl.BlockSpec((None, 8, RG_CHUNK), lambda i, s: (layer, 0, chunk(s, 2))),
            pl.BlockSpec((None, RG_HEADS, RG_BLOCK, 2 * RG_BLOCK), lambda i, s: (layer, chunk(s, 1), 0, 0)),
            pl.BlockSpec((None, len(POOL_WINDOWS), POOL_GROUP, POOL_GROUP), lambda i, s: (layer, 0, 0, 0)),
            pl.BlockSpec((None, 1, D_POOL), lambda i, s: (layer, 0, 0)),
            pl.BlockSpec((None, CONV_ROWS, D_RNN), lambda i, s: (layer, 0, 0)),
            pl.BlockSpec((None, SUBLANES, D_RNN), lambda i, s: (layer, 0, 0)),
            pl.BlockSpec((None, POOL_ROWS, D_POOL), lambda i, s: (layer, 0, 0)),
        ],
        out_specs=(
            pl.BlockSpec((ROW_TILE, RG_CHUNK), lambda i, s: (i, chunk(s, 2))),
            pl.BlockSpec((ROW_TILE, D_POOL), lambda i, s: (i, 0)),
            pl.BlockSpec((ROW_TILE, D_MODEL), lambda i, s: (x_tile(i, s), 0)),
            pl.BlockSpec((2, N_RG, CONV_ROWS, RG_CHUNK), lambda i, s: (0, 0, 0, 0)),
            pl.BlockSpec((2, N_RG, SUBLANES, RG_CHUNK), lambda i, s: (0, 0, 0, 0)),
            pl.BlockSpec((2, POOL_ROWS, D_POOL), lambda i, s: (0, 0, 0)),
        ),
        scratch_shapes=[
            pltpu.VMEM((2, ROW_TILE, D_MODEL), BF16),
            pltpu.VMEM((N_RG, CONV_ROWS, RG_CHUNK), F32),
            pltpu.VMEM((N_RG, SUBLANES, RG_CHUNK), F32),
            pltpu.VMEM((CONV_ROWS + ROW_TILE, RG_CHUNK), F32),
            pltpu.VMEM((CONV_ROWS + ROW_TILE, RG_CHUNK), F32),
            pltpu.VMEM((ROW_TILE, RG_CHUNK), F32),
            pltpu.VMEM((ROW_TILE, RG_CHUNK), F32),
            pltpu.VMEM((ROW_TILE, RG_CHUNK), BF16),
            pltpu.VMEM((RG_HEADS, ROW_TILE, 2 * RG_BLOCK), F32),
            pltpu.VMEM((RG_HEADS, ROW_TILE, 2 * RG_BLOCK), F32),
            pltpu.VMEM((ROW_TILE, RG_CHUNK), F32),
            pltpu.VMEM((ROW_TILE, RG_CHUNK), F32),
            pltpu.VMEM((ROW_TILE, RG_CHUNK), F32),
            pltpu.VMEM((3, PAIR, RG_CHUNK), F32),
            pltpu.VMEM((POOL_ROWS + ROW_TILE, D_POOL), F32),
        ],
        compiler_params=pltpu.CompilerParams(
            dimension_semantics=("arbitrary", "arbitrary"), vmem_limit_bytes=VMEM_LIMIT),
        name=f"mixer_a_l{layer}",
    )(x, gains, w_in, w_in, w_in, rgp, rgp, wax, wpm, pscale, sconv, sh, spool)


def _mixer_b_kernel(x_ref, g_ref, hb_ref, recg_ref, pooled_ref, wgr_ref, wgp_ref, wbr_ref, wbp_ref, wo_ref,
                    o_ref, *, n_chunks):
    n = pl.program_id(1)

    def chunk(first):
        hb = hb_ref[...]
        g_rg = _dot(hb, wgr_ref[...])
        g_pool = _dot(hb, wgp_ref[...])
        br = _dot(recg_ref[...], wbr_ref[...])
        bp = _dot(pooled_ref[...], wbp_ref[...])
        mix = (_sigmoid(g_rg) * br + _sigmoid(g_pool) * bp).astype(BF16)
        for c in range(D_MODEL // ACC_COLS):
            cs = slice(c * ACC_COLS, (c + 1) * ACC_COLS)
            part = _dot(mix, wo_ref[:, cs])
            if first:
                o_ref[:, cs] = part
            else:
                o_ref[:, cs] += part

    @pl.when(n == 0)
    def _():
        chunk(True)

    @pl.when(n > 0)
    def _():
        chunk(False)

    @pl.when(n == n_chunks - 1)
    def _():
        _residual_norm(x_ref, g_ref[3:4, :], o_ref)


def _mixer_b(x, gains, hb, recg, pooled, w_in, w_br_rg, w_br_pool, w_out, layer):
    rows = x.shape[0]
    n_chunks = D_MODEL // OUT_CHUNK
    g_rg_blk = (2 * D_RNN + D_POOL) // OUT_CHUNK
    g_pool_blk = g_rg_blk + n_chunks
    kern = functools.partial(_mixer_b_kernel, n_chunks=n_chunks)
    return pl.pallas_call(
        kern,
        out_shape=jax.ShapeDtypeStruct((rows, D_MODEL), F32),
        grid=(rows // ROW_TILE, n_chunks),
        in_specs=[
            pl.BlockSpec((ROW_TILE, D_MODEL), lambda i, n: (i, 0)),
            pl.BlockSpec((None, 6, D_MODEL), lambda i, n: (layer, 0, 0)),
            pl.BlockSpec((ROW_TILE, D_MODEL), lambda i, n: (i, 0)),
            pl.BlockSpec((ROW_TILE, D_RNN), lambda i, n: (i, 0)),
            pl.BlockSpec((ROW_TILE, D_POOL), lambda i, n: (i, 0)),
            pl.BlockSpec((D_MODEL, OUT_CHUNK), lambda i, n: (0, g_rg_blk + n)),
            pl.BlockSpec((D_MODEL, OUT_CHUNK), lambda i, n: (0, g_pool_blk + n)),
            pl.BlockSpec((D_RNN, OUT_CHUNK), lambda i, n: (0, n)),
            pl.BlockSpec((D_POOL, OUT_CHUNK), lambda i, n: (0, n)),
            pl.BlockSpec((OUT_CHUNK, D_MODEL), lambda i, n: (n, 0)),
        ],
        out_specs=pl.BlockSpec((ROW_TILE, D_MODEL), lambda i, n: (i, 0)),
        compiler_params=pltpu.CompilerParams(
            dimension_semantics=("arbitrary", "arbitrary"), vmem_limit_bytes=VMEM_LIMIT),
        name=f"mixer_b_l{layer}",
    )(x, gains, hb, recg, pooled, w_in, w_in, w_br_rg, w_br_pool, w_out)


FRAMES = ROW_TILE // SUBLANES


def _to_time_major_kernel(xp_ref, xs_ref, o_ref, *, n_prompt_tiles):
    i = pl.program_id(0)

    def gather(src_ref):
        for t in range(FRAMES):
            o_ref[t * SUBLANES:(t + 1) * SUBLANES, :] = src_ref[:, t, :]

    @pl.when(i < n_prompt_tiles)
    def _():
        gather(xp_ref)

    @pl.when(i >= n_prompt_tiles)
    def _():
        gather(xs_ref)


def _to_time_major(x_prompt, x_sample, n_prompt_tiles):
    n_tiles = n_prompt_tiles + 1
    return pl.pallas_call(
        functools.partial(_to_time_major_kernel, n_prompt_tiles=n_prompt_tiles),
        out_shape=jax.ShapeDtypeStruct((n_tiles * ROW_TILE, D_MODEL), F32),
        grid=(n_tiles,),
        in_specs=[
            pl.BlockSpec((SUBLANES, FRAMES, D_MODEL), lambda i: (0, jnp.minimum(i, n_prompt_tiles - 1), 0)),
            pl.BlockSpec((SUBLANES, FRAMES, D_MODEL), lambda i: (0, 0, 0)),
        ],
        out_specs=pl.BlockSpec((ROW_TILE, D_MODEL), lambda i: (i, 0)),
        compiler_params=pltpu.CompilerParams(dimension_semantics=("arbitrary",), vmem_limit_bytes=VMEM_LIMIT),
        name="to_time_major",
    )(x_prompt, x_sample)


def _to_batch_major_kernel(x_ref, yp_ref, ys_ref, *, n_prompt_tiles):
    i = pl.program_id(0)

    def scatter(dst_ref):
        for t in range(FRAMES):
            dst_ref[:, t, :] = x_ref[t * SUBLANES:(t + 1) * SUBLANES, :]

    @pl.when(i < n_prompt_tiles)
    def _():
        scatter(yp_ref)

    @pl.when(i >= n_prompt_tiles)
    def _():
        scatter(ys_ref)


def _to_batch_major(x, n_prompt_tiles, seq):
    n_tiles = n_prompt_tiles + 1
    return pl.pallas_call(
        functools.partial(_to_batch_major_kernel, n_prompt_tiles=n_prompt_tiles),
        out_shape=(
            jax.ShapeDtypeStruct((SUBLANES, seq, D_MODEL), F32),
            jax.ShapeDtypeStruct((SUBLANES, FRAMES, D_MODEL), F32),
        ),
        grid=(n_tiles,),
        in_specs=[pl.BlockSpec((ROW_TILE, D_MODEL), lambda i: (i, 0))],
        out_specs=(
            pl.BlockSpec((SUBLANES, FRAMES, D_MODEL), lambda i: (0, jnp.minimum(i, n_prompt_tiles - 1), 0)),
            pl.BlockSpec((SUBLANES, FRAMES, D_MODEL), lambda i: (0, 0, 0)),
        ),
        compiler_params=pltpu.CompilerParams(dimension_semantics=("arbitrary",), vmem_limit_bytes=VMEM_LIMIT),
        name="to_batch_major",
    )(x)


def kernel(x_prompt, x_sample, state_conv, state_h, state_pool, norm_gains, w_ffn_in, w_ffn_out, w_in,
           conv_w, conv_b, w_rg_a, b_rg_a, w_rg_x, b_rg_x, lru_param, w_pool_mix, pool_scale,
           w_br_rg, w_br_pool, w_out):
    depth = norm_gains.shape[0]
    batch, seq, _ = x_prompt.shape
    dec_batch = x_sample.shape[0]
    assert batch == SUBLANES and dec_batch == SUBLANES
    n_prompt_rows = batch * seq
    assert n_prompt_rows % ROW_TILE == 0 and x_sample.shape[1] * dec_batch == ROW_TILE
    n_prompt_tiles = n_prompt_rows // ROW_TILE

    x = _to_time_major(x_prompt, x_sample, n_prompt_tiles)

    ffn_in_b = w_ffn_in[0, 0].astype(BF16)
    ffn_out_b = w_ffn_out[0, 0].astype(BF16)
    wpm_b = w_pool_mix.astype(BF16)
    wax_b = jnp.concatenate([w_rg_a, w_rg_x], axis=-1).astype(BF16)
    rgp = jnp.concatenate(
        [conv_w, conv_b[:, None], b_rg_a[:, None], b_rg_x[:, None], lru_param[:, None]], axis=1)
    pscale = pool_scale[:, None, :]
    sconv = jnp.swapaxes(state_conv, 1, 2).reshape(depth, CONV_ROWS, D_RNN)
    spool = jnp.swapaxes(state_pool, 1, 2).reshape(depth, POOL_ROWS, D_POOL)

    convs, hs, pools = [], [], []
    for l in range(depth):
        x, (w_in_b, w_br_rg_b, w_br_pool_b, w_out_b, ffn_in_b, ffn_out_b) = _ffn(
            x, norm_gains, ffn_in_b, ffn_out_b, l, 0,
            casts=[(w_in, (l,)), (w_br_rg, (l,)), (w_br_pool, (l,)), (w_out, (l,)),
                   (w_ffn_in, (l, 1)), (w_ffn_out, (l, 1))])
        recg, pooled, hb, conv_o, h_o, pool_o = _mixer_a(
            x, norm_gains, w_in_b, rgp, wax_b, wpm_b, pscale, sconv, state_h, spool, l, n_prompt_tiles)
        x = _mixer_b(x, norm_gains, hb, recg, pooled, w_in_b, w_br_rg_b, w_br_pool_b, w_out_b, l)
        next_ffn = [(w_ffn_in, (l + 1, 0)), (w_ffn_out, (l + 1, 0))] if l + 1 < depth else []
        x, next_b = _ffn(x, norm_gains, ffn_in_b, ffn_out_b, l, 1, casts=next_ffn)
        if next_b:
            ffn_in_b, ffn_out_b = next_b
        convs.append(jnp.swapaxes(conv_o, 1, 2).reshape(2, CONV_ROWS, D_RNN))
        hs.append(jnp.swapaxes(h_o, 1, 2).reshape(2, SUBLANES, D_RNN))
        pools.append(pool_o)

    y_prompt, y_sample = _to_batch_major(x, n_prompt_tiles, seq)
    conv_all = jnp.stack(convs)
    h_all = jnp.stack(hs)
    pool_all = jnp.stack(pools)

    def unroll_state(s, which, frames):
        s = s[:, which]
        return jnp.swapaxes(s.reshape(depth, frames, SUBLANES, s.shape[-1]), 1, 2)

    return (
        y_prompt,
        y_sample,
        unroll_state(conv_all, 0, CONV_W - 1),
        h_all[:, 0],
        unroll_state(pool_all, 0, POOL_MAX - 1),
        unroll_state(conv_all, 1, CONV_W - 1),
        h_all[:, 1],
        unroll_state(pool_all, 1, POOL_MAX - 1),
    )
```

```python
import functools
import math

import jax
import jax.numpy as jnp
from jax import lax
from jax.experimental import pallas as pl
from jax.experimental.pallas import tpu as pltpu

F32 = jnp.float32
BF16 = jnp.bfloat16

D_MODEL = 2048
D_FF = 3 * D_MODEL
D_RNN = D_MODEL
D_POOL = D_MODEL // 2
N_RG_HEADS = 16
RG_BLOCK = D_RNN // N_RG_HEADS
CONV_W = 4
LRU_C = 8.0
POOL_WINDOWS = (2, 4, 8, 16)
POOL_GROUP = D_POOL // len(POOL_WINDOWS)
POOL_MAX = 16
EPS = 1e-6

SUBLANES = 8
MXU_COLS = 256
ROW_TILE = 512
FFN_CHUNK = 1024
RG_CHUNK = 512
OUT_CHUNK = 512
NORM_ROWS = 32
RESIDUAL_ROWS = 8
ACC_COLS = 512
VMEM_LIMIT = 56 * 1024 * 1024

CONV_ROWS = (CONV_W - 1) * SUBLANES
POOL_ROWS = (POOL_MAX - 1) * SUBLANES
N_RG = D_RNN // RG_CHUNK
RG_HEADS = RG_CHUNK // RG_BLOCK
PAIR = 2 * SUBLANES
FRAMES = ROW_TILE // SUBLANES
N_GROUPS = 4


def _rms(x, g):
    ms = jnp.mean(x * x, axis=-1, keepdims=True)
    return x * lax.rsqrt(ms + EPS) * g


def _norm_rows(x_ref, g, dst_refs, lo, hi):
    for r in range(lo, hi, NORM_ROWS):
        rows = slice(r, r + NORM_ROWS)
        y = _rms(x_ref[rows, :], g).astype(BF16)
        for dst_ref in dst_refs:
            dst_ref[rows, :] = y


def _norm_to_bf16(x_ref, g, *dst_refs):
    _norm_rows(x_ref, g, dst_refs, 0, x_ref.shape[0])


def _residual_norm(x_ref, g, o_ref):
    for r in range(0, x_ref.shape[0], RESIDUAL_ROWS):
        rows = slice(r, r + RESIDUAL_ROWS)
        o_ref[rows, :] = x_ref[rows, :] + _rms(o_ref[rows, :], g)


def _sigmoid(x):
    return 1.0 / (1.0 + jnp.exp(-x))


def _gelu_tanh(x):
    c = math.sqrt(2.0 / math.pi)
    half = 0.5 * x
    return half + half * jnp.tanh(x * (c + (c * 0.044715) * (x * x)))


def _sqrt_nonneg(z):
    return jnp.where(z > 0.0, z * lax.rsqrt(z), 0.0)


def _dot(a, b):
    return jnp.dot(a, b, preferred_element_type=F32)


def _interleave(matmul_pieces, vector_groups):
    for k in range(max(len(matmul_pieces), len(vector_groups))):
        if k < len(matmul_pieces):
            matmul_pieces[k]()
        if k < len(vector_groups):
            vector_groups[k]()


def _ffn_kernel(x_ref, g_ref, wg_ref, wu_ref, wo_ref, *rest, pre, post, n_chunks, n_casts, n_prompt_tiles):
    cast_src = rest[:n_casts]
    if n_prompt_tiles is None:
        o_ref = rest[n_casts]
        cast_dst = rest[n_casts + 1:2 * n_casts + 1]
        xn_ref = rest[2 * n_casts + 1]
    else:
        yp_ref, ys_ref = rest[n_casts:n_casts + 2]
        cast_dst = rest[n_casts + 2:2 * n_casts + 2]
        xn_ref, o_ref = rest[2 * n_casts + 2:2 * n_casts + 4]
    f = pl.program_id(1)

    for src, dst in zip(cast_src, cast_dst):
        dst[...] = src[...].astype(BF16)

    def chunk(first):
        xn = xn_ref[...]
        gate = _dot(xn, wg_ref[...])
        up = _dot(xn, wu_ref[...])
        hid = (gate * _sigmoid(gate) * up).astype(BF16)
        for n in range(D_MODEL // ACC_COLS):
            cs = slice(n * ACC_COLS, (n + 1) * ACC_COLS)
            part = _dot(hid, wo_ref[:, cs])
            if first:
                o_ref[:, cs] = part
            else:
                o_ref[:, cs] += part

    @pl.when(f == 0)
    def _():
        _norm_to_bf16(x_ref, g_ref[pre:pre + 1, :], xn_ref)
        chunk(True)

    @pl.when(f > 0)
    def _():
        chunk(False)

    if n_prompt_tiles is None:
        @pl.when(f == n_chunks - 1)
        def _():
            _residual_norm(x_ref, 0.5 * g_ref[post:post + 1, :], o_ref)
    else:
        i = pl.program_id(0)

        def emit(dst_ref):
            g = 0.5 * g_ref[post:post + 1, :]
            for t in range(FRAMES):
                rows = slice(t * SUBLANES, (t + 1) * SUBLANES)
                dst_ref[:, t, :] = x_ref[rows, :] + _rms(o_ref[rows, :], g)

        @pl.when((f == n_chunks - 1) & (i < n_prompt_tiles))
        def _():
            emit(yp_ref)

        @pl.when((f == n_chunks - 1) & (i >= n_prompt_tiles))
        def _():
            emit(ys_ref)


BF16_TILE_ROWS = 16


def _ffn(x, gains, w_in, w_out, layer, which, casts=(), final_seq=None):
    rows = x.shape[0]
    n_chunks = D_FF // FFN_CHUNK
    n_tiles = rows // ROW_TILE
    n_steps = n_tiles * n_chunks

    cast_in_specs, cast_out_specs, cast_out_shapes = [], [], []
    for src, lead in casts:
        m_rows, m_cols = src.shape[-2:]
        per_step = -(-m_rows // n_steps)
        blk_rows = -(-per_step // BF16_TILE_ROWS) * BF16_TILE_ROWS
        assert m_rows % blk_rows == 0
        n_blk = m_rows // blk_rows

        def blk(i, f, n_blk=n_blk):
            return jnp.minimum(i * n_chunks + f, n_blk - 1)

        cast_in_specs.append(pl.BlockSpec(
            (None,) * len(lead) + (blk_rows, m_cols), lambda i, f, lead=lead, blk=blk: lead + (blk(i, f), 0)))
        cast_out_specs.append(pl.BlockSpec((blk_rows, m_cols), lambda i, f, blk=blk: (blk(i, f), 0)))
        cast_out_shapes.append(jax.ShapeDtypeStruct((m_rows, m_cols), BF16))

    if final_seq is None:
        n_prompt_tiles = None
        main_shapes = [jax.ShapeDtypeStruct((rows, D_MODEL), F32)]
        main_specs = [pl.BlockSpec((ROW_TILE, D_MODEL), lambda i, f: (i, 0))]
        scratch = [pltpu.VMEM((ROW_TILE, D_MODEL), BF16)]
    else:
        n_prompt_tiles = n_tiles - 1
        main_shapes = [jax.ShapeDtypeStruct((SUBLANES, final_seq, D_MODEL), F32),
                       jax.ShapeDtypeStruct((SUBLANES, FRAMES, D_MODEL), F32)]
        main_specs = [
            pl.BlockSpec((SUBLANES, FRAMES, D_MODEL), lambda i, f: (0, jnp.minimum(i, n_prompt_tiles - 1), 0)),
            pl.BlockSpec((SUBLANES, FRAMES, D_MODEL), lambda i, f: (0, 0, 0)),
        ]
        scratch = [pltpu.VMEM((ROW_TILE, D_MODEL), BF16), pltpu.VMEM((ROW_TILE, D_MODEL), F32)]

    kern = functools.partial(_ffn_kernel, pre=4 * which, post=4 * which + 1, n_chunks=n_chunks,
                             n_casts=len(casts), n_prompt_tiles=n_prompt_tiles)
    outs = pl.pallas_call(
        kern,
        out_shape=main_shapes + cast_out_shapes,
        grid=(n_tiles, n_chunks),
        in_specs=[
            pl.BlockSpec((ROW_TILE, D_MODEL), lambda i, f: (i, 0)),
            pl.BlockSpec((None, 6, D_MODEL), lambda i, f: (layer, 0, 0)),
            pl.BlockSpec((D_MODEL, FFN_CHUNK), lambda i, f: (0, f)),
            pl.BlockSpec((D_MODEL, FFN_CHUNK), lambda i, f: (0, n_chunks + f)),
            pl.BlockSpec((FFN_CHUNK, D_MODEL), lambda i, f: (f, 0)),
        ] + cast_in_specs,
        out_specs=main_specs + cast_out_specs,
        scratch_shapes=scratch,
        compiler_params=pltpu.CompilerParams(
            dimension_semantics=("arbitrary", "arbitrary"), vmem_limit_bytes=VMEM_LIMIT),
        name=f"ffn_l{layer}_{which}",
    )(x, gains, w_in, w_in, w_out, *[src for src, _ in casts])
    n_main = len(main_shapes)
    main = outs[0] if n_main == 1 else tuple(outs[:n_main])
    return main, list(outs[n_main:])


N_STEPS = N_RG + 2


def _project_pieces(hb_ref, par, wxr_ref, wy_ref, zc_ref, y_ref, j):
    def piece(q):
        def run():
            hb = hb_ref[par]
            cs = slice((q % 2) * MXU_COLS, (q % 2 + 1) * MXU_COLS)
            if q < 2:
                zc_ref[j % 2][CONV_ROWS:, cs] = _dot(hb, wxr_ref[:, cs])
            else:
                y_ref[j % 3][:, cs] = _dot(hb, wy_ref[:, cs])
        return run
    return [piece(q) for q in range(2 * RG_CHUNK // MXU_COLS)]


def _pool_project_pieces(hb_ref, par, wp_ref, zp_ref):
    def piece(q):
        def run():
            cs = slice(q * MXU_COLS, (q + 1) * MXU_COLS)
            zp_ref[POOL_ROWS:, cs] = _dot(hb_ref[par], wp_ref[:, cs])
        return run
    return [piece(q) for q in range(D_POOL // MXU_COLS)]


def _conv(j, seq, rgp_ref, zc_ref, xc_ref, xcb_ref, cpre_ref, convo_ref):
    tm = ROW_TILE
    zc = zc_ref[j % 2]
    zc[0:CONV_ROWS, :] = cpre_ref[j]
    xc = rgp_ref[4:5, :]
    for k in range(CONV_W):
        xc = xc + zc[k * SUBLANES:k * SUBLANES + tm, :] * rgp_ref[k:k + 1, :]
    new_pre = zc[tm:tm + CONV_ROWS, :]
    cpre_ref[j] = new_pre
    convo_ref[seq, j] = new_pre
    xc_ref[j % 2][...] = xc
    xcb_ref[...] = xc.astype(BF16)


def _gate_project(j, wax_ref, xcb_ref, ri_ref):
    for hh in range(RG_HEADS):
        sl = slice(hh * RG_BLOCK, (hh + 1) * RG_BLOCK)
        ri_ref[j % 2][hh] = _dot(xcb_ref[:, sl], wax_ref[hh])


def _recur_groups(j, seq, rgp_ref, ri_ref, xc_ref, y_ref, hst_ref, cst_ref, recg_ref, ho_ref):
    tm = ROW_TILE
    carry = {}

    def setup():
        neg_lam = -rgp_ref[7:8, :]
        softplus = jnp.maximum(neg_lam, 0.0) + jnp.log1p(jnp.exp(-jnp.abs(neg_lam)))
        cst_ref[0] = jnp.broadcast_to(rgp_ref[5:6, :], (PAIR, RG_CHUNK))
        cst_ref[1] = jnp.broadcast_to(rgp_ref[6:7, :], (PAIR, RG_CHUNK))
        cst_ref[2] = jnp.broadcast_to(-LRU_C * softplus, (PAIR, RG_CHUNK))
        carry["h"] = hst_ref[j]

    def pair(r0):
        rows = slice(r0, r0 + PAIR)
        a_parts, u_parts = [], []
        for hh in range(RG_HEADS):
            sl = slice(hh * RG_BLOCK, (hh + 1) * RG_BLOCK)
            ri = ri_ref[j % 2][hh, rows, :]
            r = _sigmoid(ri[:, :RG_BLOCK] + cst_ref[0, :, sl])
            ig = _sigmoid(ri[:, RG_BLOCK:] + cst_ref[1, :, sl])
            log_a = cst_ref[2, :, sl] * r
            a = jnp.exp(log_a)
            u_parts.append(_sqrt_nonneg(-jnp.tanh(log_a) * (a * a + 1.0)) * (ig * xc_ref[j % 2][rows, sl]))
            a_parts.append(a)
        a = jnp.concatenate(a_parts, axis=1)
        u = jnp.concatenate(u_parts, axis=1)
        h1 = a[:SUBLANES] * carry["h"] + u[:SUBLANES]
        h2 = a[SUBLANES:] * h1 + u[SUBLANES:]
        carry["h"] = h2
        rec = jnp.concatenate([h1, h2], axis=0)
        recg_ref[rows, :] = (rec * _gelu_tanh(y_ref[j % 3][rows, :])).astype(BF16)

    def group(k):
        def run():
            if k == 0:
                setup()
            span = tm // N_GROUPS
            for r0 in range(k * span, (k + 1) * span, PAIR):
                pair(r0)
            if k == N_GROUPS - 1:
                hst_ref[j] = carry["h"]
                ho_ref[seq, j] = carry["h"]
        return run

    return [group(k) for k in range(N_GROUPS)]


def _pool_groups(i, seq, n_prompt_tiles, wpm_ref, ps_ref, zp_ref, pooled_ref, poolo_ref):
    tm = ROW_TILE
    frames = tm // SUBLANES

    def group(g):
        def run():
            w = POOL_WINDOWS[g]
            frame = lax.shift_right_logical(lax.broadcasted_iota(jnp.int32, (tm, POOL_GROUP), 0), 3)
            seen = jnp.where(i >= n_prompt_tiles, POOL_MAX - 1, i * frames) + 1
            seen = (frame + seen).astype(F32)
            cs = slice(g * POOL_GROUP, (g + 1) * POOL_GROUP)
            s = zp_ref[POOL_ROWS - (w - 1) * SUBLANES:, cs]
            shift = SUBLANES
            while shift < w * SUBLANES:
                s = s[shift:, :] + s[:s.shape[0] - shift, :]
                shift *= 2
            mean = s / jnp.minimum(seen, float(w))
            pooled = (mean - zp_ref[POOL_ROWS:, cs]).astype(BF16)
            mixed = _dot(pooled, wpm_ref[g]) * ps_ref[:, cs]
            pooled_ref[:, cs] = mixed.astype(BF16)
            if g == len(POOL_WINDOWS) - 1:
                new_pre = zp_ref[tm:tm + POOL_ROWS, :]
                zp_ref[0:POOL_ROWS, :] = new_pre
                poolo_ref[seq] = new_pre
        return run

    return [group(g) for g in range(len(POOL_WINDOWS))]


def _mixer_a_kernel(x_ref, g_ref, wxr_ref, wy_ref, wp_ref, rgp_conv_ref, rgp_recur_ref, wax_ref, wpm_ref,
                    ps_ref, sconv_ref, sh_ref, spool_ref,
                    recg_ref, pooled_ref, hbn_ref, convo_ref, ho_ref, poolo_ref,
                    hb_ref, cpre_ref, hst_ref, zc0_ref, zc1_ref, xc0_ref, xc1_ref, xcb_ref, ri0_ref, ri1_ref,
                    y0_ref, y1_ref, y2_ref, cst_ref, zp_ref,
                    *, n_prompt_tiles):
    zc_ref = (zc0_ref, zc1_ref)
    xc_ref = (xc0_ref, xc1_ref)
    ri_ref = (ri0_ref, ri1_ref)
    y_ref = (y0_ref, y1_ref, y2_ref)
    i = pl.program_id(0)
    s = pl.program_id(1)
    seq = (i >= n_prompt_tiles).astype(jnp.int32)
    par = i % 2

    def project(j):
        return _project_pieces(hb_ref, par, wxr_ref, wy_ref, zc_ref, y_ref, j)

    def conv(j):
        return lambda: _conv(j, seq, rgp_conv_ref, zc_ref, xc_ref, xcb_ref, cpre_ref, convo_ref)

    def gate_project(j):
        return lambda: _gate_project(j, wax_ref, xcb_ref, ri_ref)

    def recur(j):
        return _recur_groups(j, seq, rgp_recur_ref, ri_ref, xc_ref, y_ref, hst_ref, cst_ref, recg_ref, ho_ref)

    def norm_next():
        span = ROW_TILE // N_GROUPS
        return [functools.partial(_norm_rows, x_ref, g_ref[2:3, :], (hb_ref.at[1 - par], hbn_ref),
                                  k * span, (k + 1) * span)
                for k in range(N_GROUPS)]

    @pl.when(s == 0)
    def _():
        @pl.when(i == 0)
        def _():
            cpre_ref[...] = jnp.zeros(cpre_ref.shape, F32)
            hst_ref[...] = jnp.zeros(hst_ref.shape, F32)
            zp_ref[0:POOL_ROWS, :] = jnp.zeros((POOL_ROWS, D_POOL), F32)
            _norm_to_bf16(x_ref, g_ref[2:3, :], hb_ref.at[0], hbn_ref)

        @pl.when(i == n_prompt_tiles)
        def _():
            for j in range(N_RG):
                cs = slice(j * RG_CHUNK, (j + 1) * RG_CHUNK)
                cpre_ref[j] = sconv_ref[:, cs]
                hst_ref[j] = sh_ref[:, cs]
            zp_ref[0:POOL_ROWS, :] = spool_ref[...]

        for piece in project(0):
            piece()

    def stages(pieces, conv_stage, groups, gate_stage):
        conv_stage()
        _interleave(pieces, groups)
        gate_stage()

    @pl.when(s == 1)
    def _():
        stages(project(1), conv(0), norm_next(), gate_project(0))

    for step in range(2, N_RG):
        @pl.when(s == step)
        def _(step=step):
            stages(project(step), conv(step - 1), recur(step - 2), gate_project(step - 1))

    @pl.when(s == N_RG)
    def _():
        stages(_pool_project_pieces(hb_ref, par, wp_ref, zp_ref), conv(N_RG - 1), recur(N_RG - 2),
               gate_project(N_RG - 1))

    @pl.when(s == N_RG + 1)
    def _():
        _interleave(_pool_groups(i, seq, n_prompt_tiles, wpm_ref, ps_ref, zp_ref, pooled_ref, poolo_ref),
                    recur(N_RG - 1))


def _mixer_a(x, gains, w_in, rgp, wax, wpm, pscale, sconv, sh, spool, layer, n_prompt_tiles):
    rows = x.shape[0]
    n_tiles = rows // ROW_TILE
    pool_blk = (2 * D_RNN) // D_POOL

    def chunk(s, lag):
        return jnp.clip(s - lag, 0, N_RG - 1)

    def x_tile(i, s):
        return jnp.minimum(i + (s >= 1).astype(jnp.int32), n_tiles - 1)

    kern = functools.partial(_mixer_a_kernel, n_prompt_tiles=n_prompt_tiles)
    return pl.pallas_call(
        kern,
        out_shape=(
            jax.ShapeDtypeStruct((rows, D_RNN), BF16),
            jax.ShapeDtypeStruct((rows, D_POOL), BF16),
            jax.ShapeDtypeStruct((rows, D_MODEL), BF16),
            jax.ShapeDtypeStruct((2, N_RG, CONV_ROWS, RG_CHUNK), F32),
            jax.ShapeDtypeStruct((2, N_RG, SUBLANES, RG_CHUNK), F32),
            jax.ShapeDtypeStruct((2, POOL_ROWS, D_POOL), F32),
        ),
        grid=(n_tiles, N_STEPS),
        in_specs=[
            pl.BlockSpec((ROW_TILE, D_MODEL), lambda i, s: (x_tile(i, s), 0)),
            pl.BlockSpec((None, 6, D_MODEL), lambda i, s: (layer, 0, 0)),
            pl.BlockSpec((D_MODEL, RG_CHUNK), lambda i, s: (0, chunk(s, 0))),
            pl.BlockSpec((D_MODEL, RG_CHUNK), lambda i, s: (0, N_RG + chunk(s, 0))),
            pl.BlockSpec((D_MODEL, D_POOL), lambda i, s: (0, pool_blk)),
            pl.BlockSpec((None, 8, RG_CHUNK), lambda i, s: (layer, 0, chunk(s, 1))),
            pl.BlockSpec((None, 8, RG_CHUNK), lambda i, s: (layer, 0, chunk(s, 2))),
            pl.BlockSpec((None, RG_HEADS, RG_BLOCK, 2 * RG_BLOCK), lambda i, s: (layer, chunk(s, 1), 0, 0)),
            pl.BlockSpec((None, len(POOL_WINDOWS), POOL_GROUP, POOL_GROUP), lambda i, s: (layer, 0, 0, 0)),
            pl.BlockSpec((None, 1, D_POOL), lambda i, s: (layer, 0, 0)),
            pl.BlockSpec((None, CONV_ROWS, D_RNN), lambda i, s: (layer, 0, 0)),
            pl.BlockSpec((None, SUBLANES, D_RNN), lambda i, s: (layer, 0, 0)),
            pl.BlockSpec((None, POOL_ROWS, D_POOL), lambda i, s: (layer, 0, 0)),
        ],
        out_specs=(
            pl.BlockSpec((ROW_TILE, RG_CHUNK), lambda i, s: (i, chunk(s, 2))),
            pl.BlockSpec((ROW_TILE, D_POOL), lambda i, s: (i, 0)),
            pl.BlockSpec((ROW_TILE, D_MODEL), lambda i, s: (x_tile(i, s), 0)),
            pl.BlockSpec((2, N_RG, CONV_ROWS, RG_CHUNK), lambda i, s: (0, 0, 0, 0)),
            pl.BlockSpec((2, N_RG, SUBLANES, RG_CHUNK), lambda i, s: (0, 0, 0, 0)),
            pl.BlockSpec((2, POOL_ROWS, D_POOL), lambda i, s: (0, 0, 0)),
        ),
        scratch_shapes=[
            pltpu.VMEM((2, ROW_TILE, D_MODEL), BF16),
            pltpu.VMEM((N_RG, CONV_ROWS, RG_CHUNK), F32),
            pltpu.VMEM((N_RG, SUBLANES, RG_CHUNK), F32),
            pltpu.VMEM((CONV_ROWS + ROW_TILE, RG_CHUNK), F32),
            pltpu.VMEM((CONV_ROWS + ROW_TILE, RG_CHUNK), F32),
            pltpu.VMEM((ROW_TILE, RG_CHUNK), F32),
            pltpu.VMEM((ROW_TILE, RG_CHUNK), F32),
            pltpu.VMEM((ROW_TILE, RG_CHUNK), BF16),
            pltpu.VMEM((RG_HEADS, ROW_TILE, 2 * RG_BLOCK), F32),
            pltpu.VMEM((RG_HEADS, ROW_TILE, 2 * RG_BLOCK), F32),
            pltpu.VMEM((ROW_TILE, RG_CHUNK), F32),
            pltpu.VMEM((ROW_TILE, RG_CHUNK), F32),
            pltpu.VMEM((ROW_TILE, RG_CHUNK), F32),
            pltpu.VMEM((3, PAIR, RG_CHUNK), F32),
            pltpu.VMEM((POOL_ROWS + ROW_TILE, D_POOL), F32),
        ],
        compiler_params=pltpu.CompilerParams(
            dimension_semantics=("arbitrary", "arbitrary"), vmem_limit_bytes=VMEM_LIMIT),
        name=f"mixer_a_l{layer}",
    )(x, gains, w_in, w_in, w_in, rgp, rgp, wax, wpm, pscale, sconv, sh, spool)


def _mixer_b_kernel(x_ref, g_ref, hb_ref, recg_ref, pooled_ref, wgr_ref, wgp_ref, wbr_ref, wbp_ref, wo_ref,
                    o_ref, *, n_chunks):
    n = pl.program_id(1)

    def chunk(first):
        hb = hb_ref[...]
        g_rg = _dot(hb, wgr_ref[...])
        g_pool = _dot(hb, wgp_ref[...])
        br = _dot(recg_ref[...], wbr_ref[...])
        bp = _dot(pooled_ref[...], wbp_ref[...])
        mix = (_sigmoid(g_rg) * br + _sigmoid(g_pool) * bp).astype(BF16)
        for c in range(D_MODEL // ACC_COLS):
            cs = slice(c * ACC_COLS, (c + 1) * ACC_COLS)
            part = _dot(mix, wo_ref[:, cs])
            if first:
                o_ref[:, cs] = part
            else:
                o_ref[:, cs] += part

    @pl.when(n == 0)
    def _():
        chunk(True)

    @pl.when(n > 0)
    def _():
        chunk(False)

    @pl.when(n == n_chunks - 1)
    def _():
        _residual_norm(x_ref, g_ref[3:4, :], o_ref)


def _mixer_b(x, gains, hb, recg, pooled, w_in, w_br_rg, w_br_pool, w_out, layer):
    rows = x.shape[0]
    n_chunks = D_MODEL // OUT_CHUNK
    g_rg_blk = (2 * D_RNN + D_POOL) // OUT_CHUNK
    g_pool_blk = g_rg_blk + n_chunks
    kern = functools.partial(_mixer_b_kernel, n_chunks=n_chunks)
    return pl.pallas_call(
        kern,
        out_shape=jax.ShapeDtypeStruct((rows, D_MODEL), F32),
        grid=(rows // ROW_TILE, n_chunks),
        in_specs=[
            pl.BlockSpec((ROW_TILE, D_MODEL), lambda i, n: (i, 0)),
            pl.BlockSpec((None, 6, D_MODEL), lambda i, n: (layer, 0, 0)),
            pl.BlockSpec((ROW_TILE, D_MODEL), lambda i, n: (i, 0)),
            pl.BlockSpec((ROW_TILE, D_RNN), lambda i, n: (i, 0)),
            pl.BlockSpec((ROW_TILE, D_POOL), lambda i, n: (i, 0)),
            pl.BlockSpec((D_MODEL, OUT_CHUNK), lambda i, n: (0, g_rg_blk + n)),
            pl.BlockSpec((D_MODEL, OUT_CHUNK), lambda i, n: (0, g_pool_blk + n)),
            pl.BlockSpec((D_RNN, OUT_CHUNK), lambda i, n: (0, n)),
            pl.BlockSpec((D_POOL, OUT_CHUNK), lambda i, n: (0, n)),
            pl.BlockSpec((OUT_CHUNK, D_MODEL), lambda i, n: (n, 0)),
        ],
        out_specs=pl.BlockSpec((ROW_TILE, D_MODEL), lambda i, n: (i, 0)),
        compiler_params=pltpu.CompilerParams(
            dimension_semantics=("arbitrary", "arbitrary"), vmem_limit_bytes=VMEM_LIMIT),
        name=f"mixer_b_l{layer}",
    )(x, gains, hb, recg, pooled, w_in, w_in, w_br_rg, w_br_pool, w_out)


def _to_time_major_kernel(xp_ref, xs_ref, o_ref, *, n_prompt_tiles):
    i = pl.program_id(0)

    def gather(src_ref):
        for t in range(FRAMES):
            o_ref[t * SUBLANES:(t + 1) * SUBLANES, :] = src_ref[:, t, :]

    @pl.when(i < n_prompt_tiles)
    def _():
        gather(xp_ref)

    @pl.when(i >= n_prompt_tiles)
    def _():
        gather(xs_ref)


def _to_time_major(x_prompt, x_sample, n_prompt_tiles):
    n_tiles = n_prompt_tiles + 1
    return pl.pallas_call(
        functools.partial(_to_time_major_kernel, n_prompt_tiles=n_prompt_tiles),
        out_shape=jax.ShapeDtypeStruct((n_tiles * ROW_TILE, D_MODEL), F32),
        grid=(n_tiles,),
        in_specs=[
            pl.BlockSpec((SUBLANES, FRAMES, D_MODEL), lambda i: (0, jnp.minimum(i, n_prompt_tiles - 1), 0)),
            pl.BlockSpec((SUBLANES, FRAMES, D_MODEL), lambda i: (0, 0, 0)),
        ],
        out_specs=pl.BlockSpec((ROW_TILE, D_MODEL), lambda i: (i, 0)),
        compiler_params=pltpu.CompilerParams(dimension_semantics=("arbitrary",), vmem_limit_bytes=VMEM_LIMIT),
        name="to_time_major",
    )(x_prompt, x_sample)


def kernel(x_prompt, x_sample, state_conv, state_h, state_pool, norm_gains, w_ffn_in, w_ffn_out, w_in,
           conv_w, conv_b, w_rg_a, b_rg_a, w_rg_x, b_rg_x, lru_param, w_pool_mix, pool_scale,
           w_br_rg, w_br_pool, w_out):
    depth = norm_gains.shape[0]
    batch, seq, _ = x_prompt.shape
    dec_batch = x_sample.shape[0]
    assert batch == SUBLANES and dec_batch == SUBLANES
    n_prompt_rows = batch * seq
    assert n_prompt_rows % ROW_TILE == 0 and x_sample.shape[1] * dec_batch == ROW_TILE
    n_prompt_tiles = n_prompt_rows // ROW_TILE

    x = _to_time_major(x_prompt, x_sample, n_prompt_tiles)

    ffn_in_b = w_ffn_in[0, 0].astype(BF16)
    ffn_out_b = w_ffn_out[0, 0].astype(BF16)
    wpm_b = w_pool_mix.astype(BF16)
    wax_b = jnp.concatenate([w_rg_a, w_rg_x], axis=-1).astype(BF16)
    rgp = jnp.concatenate(
        [conv_w, conv_b[:, None], b_rg_a[:, None], b_rg_x[:, None], lru_param[:, None]], axis=1)
    pscale = pool_scale[:, None, :]
    sconv = jnp.swapaxes(state_conv, 1, 2).reshape(depth, CONV_ROWS, D_RNN)
    spool = jnp.swapaxes(state_pool, 1, 2).reshape(depth, POOL_ROWS, D_POOL)

    convs, hs, pools = [], [], []
    for l in range(depth):
        x, (w_in_b, w_br_rg_b, w_br_pool_b, w_out_b, ffn_in_b, ffn_out_b) = _ffn(
            x, norm_gains, ffn_in_b, ffn_out_b, l, 0,
            casts=[(w_in, (l,)), (w_br_rg, (l,)), (w_br_pool, (l,)), (w_out, (l,)),
                   (w_ffn_in, (l, 1)), (w_ffn_out, (l, 1))])
        recg, pooled, hb, conv_o, h_o, pool_o = _mixer_a(
            x, norm_gains, w_in_b, rgp, wax_b, wpm_b, pscale, sconv, state_h, spool, l, n_prompt_tiles)
        x = _mixer_b(x, norm_gains, hb, recg, pooled, w_in_b, w_br_rg_b, w_br_pool_b, w_out_b, l)
        next_ffn = [(w_ffn_in, (l + 1, 0)), (w_ffn_out, (l + 1, 0))] if l + 1 < depth else []
        x, next_b = _ffn(x, norm_gains, ffn_in_b, ffn_out_b, l, 1, casts=next_ffn,
                         final_seq=seq if l + 1 == depth else None)
        if next_b:
            ffn_in_b, ffn_out_b = next_b
        convs.append(jnp.swapaxes(conv_o, 1, 2).reshape(2, CONV_ROWS, D_RNN))
        hs.append(jnp.swapaxes(h_o, 1, 2).reshape(2, SUBLANES, D_RNN))
        pools.append(pool_o)

    y_prompt, y_sample = x
    conv_all = jnp.stack(convs)
    h_all = jnp.stack(hs)
    pool_all = jnp.stack(pools)

    def unroll_state(s, which, frames):
        s = s[:, which]
        return jnp.swapaxes(s.reshape(depth, frames, SUBLANES, s.shape[-1]), 1, 2)

    return (
        y_prompt,
        y_sample,
        unroll_state(conv_all, 0, CONV_W - 1),
        h_all[:, 0],
        unroll_state(pool_all, 0, POOL_MAX - 1),
        unroll_state(conv_all, 1, CONV_W - 1),
        h_all[:, 1],
        unroll_state(pool_all, 1, POOL_MAX - 1),
    )
```

```python
import functools
import math

import jax
import jax.numpy as jnp
from jax import lax
from jax.experimental import pallas as pl
from jax.experimental.pallas import tpu as pltpu

F32 = jnp.float32
BF16 = jnp.bfloat16

D_MODEL = 2048
D_FF = 3 * D_MODEL
D_RNN = D_MODEL
D_POOL = D_MODEL // 2
N_RG_HEADS = 16
RG_BLOCK = D_RNN // N_RG_HEADS
CONV_W = 4
LRU_C = 8.0
POOL_WINDOWS = (2, 4, 8, 16)
POOL_GROUP = D_POOL // len(POOL_WINDOWS)
POOL_MAX = 16
EPS = 1e-6

SUBLANES = 8
MXU_COLS = 256
ROW_TILE = 512
FFN_CHUNK = 1024
RG_CHUNK = 512
OUT_CHUNK = 512
NORM_ROWS = 32
RESIDUAL_ROWS = 8
ACC_COLS = 512
VMEM_LIMIT = 56 * 1024 * 1024
FIRST_FFN_VMEM_LIMIT = 62 * 1024 * 1024

CONV_ROWS = (CONV_W - 1) * SUBLANES
POOL_ROWS = (POOL_MAX - 1) * SUBLANES
N_RG = D_RNN // RG_CHUNK
RG_HEADS = RG_CHUNK // RG_BLOCK
PAIR = 2 * SUBLANES
FRAMES = ROW_TILE // SUBLANES
N_GROUPS = 4


def _rms(x, g):
    ms = jnp.mean(x * x, axis=-1, keepdims=True)
    return x * lax.rsqrt(ms + EPS) * g


def _norm_rows(x_ref, g, dst_refs, lo, hi):
    for r in range(lo, hi, NORM_ROWS):
        rows = slice(r, r + NORM_ROWS)
        y = _rms(x_ref[rows, :], g).astype(BF16)
        for dst_ref in dst_refs:
            dst_ref[rows, :] = y


def _norm_to_bf16(x_ref, g, *dst_refs):
    _norm_rows(x_ref, g, dst_refs, 0, x_ref.shape[0])


def _residual_norm(x_ref, g, o_ref):
    for r in range(0, x_ref.shape[0], RESIDUAL_ROWS):
        rows = slice(r, r + RESIDUAL_ROWS)
        o_ref[rows, :] = x_ref[rows, :] + _rms(o_ref[rows, :], g)


def _sigmoid(x):
    return 1.0 / (1.0 + jnp.exp(-x))


def _gelu_tanh(x):
    c = math.sqrt(2.0 / math.pi)
    half = 0.5 * x
    return half + half * jnp.tanh(x * (c + (c * 0.044715) * (x * x)))


def _sqrt_nonneg(z):
    return jnp.where(z > 0.0, z * lax.rsqrt(z), 0.0)


def _dot(a, b):
    return jnp.dot(a, b, preferred_element_type=F32)


def _interleave(matmul_pieces, vector_groups):
    for k in range(max(len(matmul_pieces), len(vector_groups))):
        if k < len(matmul_pieces):
            matmul_pieces[k]()
        if k < len(vector_groups):
            vector_groups[k]()


def _ffn_kernel(*refs, pre, post, n_chunks, n_casts, n_in_tiles, n_prompt_tiles):
    if n_in_tiles is None:
        x_ref, g_ref, wg_ref, wu_ref, wo_ref = refs[:5]
        rest = refs[5:]
    else:
        xp_ref, xs_ref, g_ref, wg_ref, wu_ref, wo_ref = refs[:6]
        rest = refs[6:-1]
        x_ref = refs[-1]
    cast_src = rest[:n_casts]
    if n_prompt_tiles is None:
        o_ref = rest[n_casts]
        cast_dst = rest[n_casts + 1:2 * n_casts + 1]
        xn_ref = rest[2 * n_casts + 1]
    else:
        yp_ref, ys_ref = rest[n_casts:n_casts + 2]
        cast_dst = rest[n_casts + 2:2 * n_casts + 2]
        xn_ref, o_ref = rest[2 * n_casts + 2:2 * n_casts + 4]
    f = pl.program_id(1)

    for src, dst in zip(cast_src, cast_dst):
        dst[...] = src[...].astype(BF16)

    def chunk(first):
        xn = xn_ref[...]
        gate = _dot(xn, wg_ref[...])
        up = _dot(xn, wu_ref[...])
        hid = (gate * _sigmoid(gate) * up).astype(BF16)
        for n in range(D_MODEL // ACC_COLS):
            cs = slice(n * ACC_COLS, (n + 1) * ACC_COLS)
            part = _dot(hid, wo_ref[:, cs])
            if first:
                o_ref[:, cs] = part
            else:
                o_ref[:, cs] += part

    i = pl.program_id(0)

    if n_in_tiles is not None:
        def gather(src_ref):
            for t in range(FRAMES):
                x_ref[t * SUBLANES:(t + 1) * SUBLANES, :] = src_ref[:, t, :]

        @pl.when((f == 0) & (i < n_in_tiles))
        def _():
            gather(xp_ref)

        @pl.when((f == 0) & (i >= n_in_tiles))
        def _():
            gather(xs_ref)

    @pl.when(f == 0)
    def _():
        _norm_to_bf16(x_ref, g_ref[pre:pre + 1, :], xn_ref)
        chunk(True)

    @pl.when(f > 0)
    def _():
        chunk(False)

    if n_prompt_tiles is None:
        @pl.when(f == n_chunks - 1)
        def _():
            _residual_norm(x_ref, 0.5 * g_ref[post:post + 1, :], o_ref)
    else:

        def emit(dst_ref):
            g = 0.5 * g_ref[post:post + 1, :]
            for t in range(FRAMES):
                rows = slice(t * SUBLANES, (t + 1) * SUBLANES)
                dst_ref[:, t, :] = x_ref[rows, :] + _rms(o_ref[rows, :], g)

        @pl.when((f == n_chunks - 1) & (i < n_prompt_tiles))
        def _():
            emit(yp_ref)

        @pl.when((f == n_chunks - 1) & (i >= n_prompt_tiles))
        def _():
            emit(ys_ref)


BF16_TILE_ROWS = 16


def _ffn(x, gains, w_in, w_out, layer, which, casts=(), final_seq=None):
    from_streams = isinstance(x, tuple)
    if from_streams:
        x_prompt, x_sample = x
        n_in_tiles = x_prompt.shape[1] // FRAMES
        rows = (n_in_tiles + 1) * ROW_TILE
        x_args = [x_prompt, x_sample]
        x_specs = [
            pl.BlockSpec((SUBLANES, FRAMES, D_MODEL), lambda i, f: (0, jnp.minimum(i, n_in_tiles - 1), 0)),
            pl.BlockSpec((SUBLANES, FRAMES, D_MODEL), lambda i, f: (0, 0, 0)),
        ]
        x_scratch = [pltpu.VMEM((ROW_TILE, D_MODEL), F32)]
        vmem_limit = FIRST_FFN_VMEM_LIMIT
    else:
        n_in_tiles = None
        rows = x.shape[0]
        x_args = [x]
        x_specs = [pl.BlockSpec((ROW_TILE, D_MODEL), lambda i, f: (i, 0))]
        x_scratch = []
        vmem_limit = VMEM_LIMIT
    n_chunks = D_FF // FFN_CHUNK
    n_tiles = rows // ROW_TILE
    n_steps = n_tiles * n_chunks

    cast_in_specs, cast_out_specs, cast_out_shapes = [], [], []
    for src, lead in casts:
        m_rows, m_cols = src.shape[-2:]
        per_step = -(-m_rows // n_steps)
        blk_rows = -(-per_step // BF16_TILE_ROWS) * BF16_TILE_ROWS
        assert m_rows % blk_rows == 0
        n_blk = m_rows // blk_rows

        def blk(i, f, n_blk=n_blk):
            return jnp.minimum(i * n_chunks + f, n_blk - 1)

        cast_in_specs.append(pl.BlockSpec(
            (None,) * len(lead) + (blk_rows, m_cols), lambda i, f, lead=lead, blk=blk: lead + (blk(i, f), 0)))
        cast_out_specs.append(pl.BlockSpec((blk_rows, m_cols), lambda i, f, blk=blk: (blk(i, f), 0)))
        cast_out_shapes.append(jax.ShapeDtypeStruct((m_rows, m_cols), BF16))

    if final_seq is None:
        n_prompt_tiles = None
        main_shapes = [jax.ShapeDtypeStruct((rows, D_MODEL), F32)]
        main_specs = [pl.BlockSpec((ROW_TILE, D_MODEL), lambda i, f: (i, 0))]
        scratch = [pltpu.VMEM((ROW_TILE, D_MODEL), BF16)]
    else:
        n_prompt_tiles = n_tiles - 1
        main_shapes = [jax.ShapeDtypeStruct((SUBLANES, final_seq, D_MODEL), F32),
                       jax.ShapeDtypeStruct((SUBLANES, FRAMES, D_MODEL), F32)]
        main_specs = [
            pl.BlockSpec((SUBLANES, FRAMES, D_MODEL), lambda i, f: (0, jnp.minimum(i, n_prompt_tiles - 1), 0)),
            pl.BlockSpec((SUBLANES, FRAMES, D_MODEL), lambda i, f: (0, 0, 0)),
        ]
        scratch = [pltpu.VMEM((ROW_TILE, D_MODEL), BF16), pltpu.VMEM((ROW_TILE, D_MODEL), F32)]

    kern = functools.partial(_ffn_kernel, pre=4 * which, post=4 * which + 1, n_chunks=n_chunks,
                             n_casts=len(casts), n_in_tiles=n_in_tiles, n_prompt_tiles=n_prompt_tiles)
    outs = pl.pallas_call(
        kern,
        out_shape=main_shapes + cast_out_shapes,
        grid=(n_tiles, n_chunks),
        in_specs=x_specs + [
            pl.BlockSpec((None, 6, D_MODEL), lambda i, f: (layer, 0, 0)),
            pl.BlockSpec((D_MODEL, FFN_CHUNK), lambda i, f: (0, f)),
            pl.BlockSpec((D_MODEL, FFN_CHUNK), lambda i, f: (0, n_chunks + f)),
            pl.BlockSpec((FFN_CHUNK, D_MODEL), lambda i, f: (f, 0)),
        ] + cast_in_specs,
        out_specs=main_specs + cast_out_specs,
        scratch_shapes=scratch + x_scratch,
        compiler_params=pltpu.CompilerParams(
            dimension_semantics=("arbitrary", "arbitrary"), vmem_limit_bytes=vmem_limit),
        name=f"ffn_l{layer}_{which}",
    )(*x_args, gains, w_in, w_in, w_out, *[src for src, _ in casts])
    n_main = len(main_shapes)
    main = outs[0] if n_main == 1 else tuple(outs[:n_main])
    return main, list(outs[n_main:])


N_STEPS = N_RG + 2


def _project_pieces(hb_ref, par, wxr_ref, wy_ref, zc_ref, y_ref, j):
    def piece(q):
        def run():
            hb = hb_ref[par]
            cs = slice((q % 2) * MXU_COLS, (q % 2 + 1) * MXU_COLS)
            if q < 2:
                zc_ref[j % 2][CONV_ROWS:, cs] = _dot(hb, wxr_ref[:, cs])
            else:
                y_ref[j % 3][:, cs] = _dot(hb, wy_ref[:, cs])
        return run
    return [piece(q) for q in range(2 * RG_CHUNK // MXU_COLS)]


def _pool_project_pieces(hb_ref, par, wp_ref, zp_ref):
    def piece(q):
        def run():
            cs = slice(q * MXU_COLS, (q + 1) * MXU_COLS)
            zp_ref[POOL_ROWS:, cs] = _dot(hb_ref[par], wp_ref[:, cs])
        return run
    return [piece(q) for q in range(D_POOL // MXU_COLS)]


def _conv(j, seq, rgp_ref, zc_ref, xc_ref, xcb_ref, cpre_ref, convo_ref):
    tm = ROW_TILE
    zc = zc_ref[j % 2]
    zc[0:CONV_ROWS, :] = cpre_ref[j]
    xc = rgp_ref[4:5, :]
    for k in range(CONV_W):
        xc = xc + zc[k * SUBLANES:k * SUBLANES + tm, :] * rgp_ref[k:k + 1, :]
    new_pre = zc[tm:tm + CONV_ROWS, :]
    cpre_ref[j] = new_pre
    convo_ref[seq, j] = new_pre
    xc_ref[j % 2][...] = xc
    xcb_ref[...] = xc.astype(BF16)


def _gate_project(j, wax_ref, xcb_ref, ri_ref):
    for hh in range(RG_HEADS):
        sl = slice(hh * RG_BLOCK, (hh + 1) * RG_BLOCK)
        ri_ref[j % 2][hh] = _dot(xcb_ref[:, sl], wax_ref[hh])


def _recur_groups(j, seq, rgp_ref, ri_ref, xc_ref, y_ref, hst_ref, cst_ref, recg_ref, ho_ref):
    tm = ROW_TILE
    carry = {}

    def setup():
        neg_lam = -rgp_ref[7:8, :]
        softplus = jnp.maximum(neg_lam, 0.0) + jnp.log1p(jnp.exp(-jnp.abs(neg_lam)))
        cst_ref[0] = jnp.broadcast_to(rgp_ref[5:6, :], (PAIR, RG_CHUNK))
        cst_ref[1] = jnp.broadcast_to(rgp_ref[6:7, :], (PAIR, RG_CHUNK))
        cst_ref[2] = jnp.broadcast_to(-LRU_C * softplus, (PAIR, RG_CHUNK))
        carry["h"] = hst_ref[j]

    def pair(r0):
        rows = slice(r0, r0 + PAIR)
        a_parts, u_parts = [], []
        for hh in range(RG_HEADS):
            sl = slice(hh * RG_BLOCK, (hh + 1) * RG_BLOCK)
            ri = ri_ref[j % 2][hh, rows, :]
            r = _sigmoid(ri[:, :RG_BLOCK] + cst_ref[0, :, sl])
            ig = _sigmoid(ri[:, RG_BLOCK:] + cst_ref[1, :, sl])
            log_a = cst_ref[2, :, sl] * r
            a = jnp.exp(log_a)
            u_parts.append(_sqrt_nonneg(-jnp.tanh(log_a) * (a * a + 1.0)) * (ig * xc_ref[j % 2][rows, sl]))
            a_parts.append(a)
        a = jnp.concatenate(a_parts, axis=1)
        u = jnp.concatenate(u_parts, axis=1)
        h1 = a[:SUBLANES] * carry["h"] + u[:SUBLANES]
        h2 = a[SUBLANES:] * h1 + u[SUBLANES:]
        carry["h"] = h2
        rec = jnp.concatenate([h1, h2], axis=0)
        recg_ref[rows, :] = (rec * _gelu_tanh(y_ref[j % 3][rows, :])).astype(BF16)

    def group(k):
        def run():
            if k == 0:
                setup()
            span = tm // N_GROUPS
            for r0 in range(k * span, (k + 1) * span, PAIR):
                pair(r0)
            if k == N_GROUPS - 1:
                hst_ref[j] = carry["h"]
                ho_ref[seq, j] = carry["h"]
        return run

    return [group(k) for k in range(N_GROUPS)]


def _pool_groups(i, seq, n_prompt_tiles, wpm_ref, ps_ref, zp_ref, pooled_ref, poolo_ref):
    tm = ROW_TILE
    frames = tm // SUBLANES

    def group(g):
        def run():
            w = POOL_WINDOWS[g]
            frame = lax.shift_right_logical(lax.broadcasted_iota(jnp.int32, (tm, POOL_GROUP), 0), 3)
            seen = jnp.where(i >= n_prompt_tiles, POOL_MAX - 1, i * frames) + 1
            seen = (frame + seen).astype(F32)
            cs = slice(g * POOL_GROUP, (g + 1) * POOL_GROUP)
            s = zp_ref[POOL_ROWS - (w - 1) * SUBLANES:, cs]
            shift = SUBLANES
            while shift < w * SUBLANES:
                s = s[shift:, :] + s[:s.shape[0] - shift, :]
                shift *= 2
            mean = s / jnp.minimum(seen, float(w))
            pooled = (mean - zp_ref[POOL_ROWS:, cs]).astype(BF16)
            mixed = _dot(pooled, wpm_ref[g]) * ps_ref[:, cs]
            pooled_ref[:, cs] = mixed.astype(BF16)
            if g == len(POOL_WINDOWS) - 1:
                new_pre = zp_ref[tm:tm + POOL_ROWS, :]
                zp_ref[0:POOL_ROWS, :] = new_pre
                poolo_ref[seq] = new_pre
        return run

    return [group(g) for g in range(len(POOL_WINDOWS))]


def _mixer_a_kernel(x_ref, g_ref, wxr_ref, wy_ref, wp_ref, rgp_conv_ref, rgp_recur_ref, wax_ref, wpm_ref,
                    ps_ref, sconv_ref, sh_ref, spool_ref,
                    recg_ref, pooled_ref, hbn_ref, convo_ref, ho_ref, poolo_ref,
                    hb_ref, cpre_ref, hst_ref, zc0_ref, zc1_ref, xc0_ref, xc1_ref, xcb_ref, ri0_ref, ri1_ref,
                    y0_ref, y1_ref, y2_ref, cst_ref, zp_ref,
                    *, n_prompt_tiles):
    zc_ref = (zc0_ref, zc1_ref)
    xc_ref = (xc0_ref, xc1_ref)
    ri_ref = (ri0_ref, ri1_ref)
    y_ref = (y0_ref, y1_ref, y2_ref)
    i = pl.program_id(0)
    s = pl.program_id(1)
    seq = (i >= n_prompt_tiles).astype(jnp.int32)
    par = i % 2

    def project(j):
        return _project_pieces(hb_ref, par, wxr_ref, wy_ref, zc_ref, y_ref, j)

    def conv(j):
        return lambda: _conv(j, seq, rgp_conv_ref, zc_ref, xc_ref, xcb_ref, cpre_ref, convo_ref)

    def gate_project(j):
        return lambda: _gate_project(j, wax_ref, xcb_ref, ri_ref)

    def recur(j):
        return _recur_groups(j, seq, rgp_recur_ref, ri_ref, xc_ref, y_ref, hst_ref, cst_ref, recg_ref, ho_ref)

    def norm_next():
        span = ROW_TILE // N_GROUPS
        return [functools.partial(_norm_rows, x_ref, g_ref[2:3, :], (hb_ref.at[1 - par], hbn_ref),
                                  k * span, (k + 1) * span)
                for k in range(N_GROUPS)]

    @pl.when(s == 0)
    def _():
        @pl.when(i == 0)
        def _():
            cpre_ref[...] = jnp.zeros(cpre_ref.shape, F32)
            hst_ref[...] = jnp.zeros(hst_ref.shape, F32)
            zp_ref[0:POOL_ROWS, :] = jnp.zeros((POOL_ROWS, D_POOL), F32)
            _norm_to_bf16(x_ref, g_ref[2:3, :], hb_ref.at[0], hbn_ref)

        @pl.when(i == n_prompt_tiles)
        def _():
            for j in range(N_RG):
                cs = slice(j * RG_CHUNK, (j + 1) * RG_CHUNK)
                cpre_ref[j] = sconv_ref[:, cs]
                hst_ref[j] = sh_ref[:, cs]
            zp_ref[0:POOL_ROWS, :] = spool_ref[...]

        for piece in project(0):
            piece()

    def stages(pieces, conv_stage, groups, gate_stage):
        conv_stage()
        _interleave(pieces, groups)
        gate_stage()

    @pl.when(s == 1)
    def _():
        stages(project(1), conv(0), norm_next(), gate_project(0))

    for step in range(2, N_RG):
        @pl.when(s == step)
        def _(step=step):
            stages(project(step), conv(step - 1), recur(step - 2), gate_project(step - 1))

    @pl.when(s == N_RG)
    def _():
        stages(_pool_project_pieces(hb_ref, par, wp_ref, zp_ref), conv(N_RG - 1), recur(N_RG - 2),
               gate_project(N_RG - 1))

    @pl.when(s == N_RG + 1)
    def _():
        _interleave(_pool_groups(i, seq, n_prompt_tiles, wpm_ref, ps_ref, zp_ref, pooled_ref, poolo_ref),
                    recur(N_RG - 1))


def _mixer_a(x, gains, w_in, rgp, wax, wpm, pscale, sconv, sh, spool, layer, n_prompt_tiles):
    rows = x.shape[0]
    n_tiles = rows // ROW_TILE
    pool_blk = (2 * D_RNN) // D_POOL

    def chunk(s, lag):
        return jnp.clip(s - lag, 0, N_RG - 1)

    def x_tile(i, s):
        return jnp.minimum(i + (s >= 1).astype(jnp.int32), n_tiles - 1)

    kern = functools.partial(_mixer_a_kernel, n_prompt_tiles=n_prompt_tiles)
    return pl.pallas_call(
        kern,
        out_shape=(
            jax.ShapeDtypeStruct((rows, D_RNN), BF16),
            jax.ShapeDtypeStruct((rows, D_POOL), BF16),
            jax.ShapeDtypeStruct((rows, D_MODEL), BF16),
            jax.ShapeDtypeStruct((2, N_RG, CONV_ROWS, RG_CHUNK), F32),
            jax.ShapeDtypeStruct((2, N_RG, SUBLANES, RG_CHUNK), F32),
            jax.ShapeDtypeStruct((2, POOL_ROWS, D_POOL), F32),
        ),
        grid=(n_tiles, N_STEPS),
        in_specs=[
            pl.BlockSpec((ROW_TILE, D_MODEL), lambda i, s: (x_tile(i, s), 0)),
            pl.BlockSpec((None, 6, D_MODEL), lambda i, s: (layer, 0, 0)),
            pl.BlockSpec((D_MODEL, RG_CHUNK), lambda i, s: (0, chunk(s, 0))),
            pl.BlockSpec((D_MODEL, RG_CHUNK), lambda i, s: (0, N_RG + chunk(s, 0))),
            pl.BlockSpec((D_MODEL, D_POOL), lambda i, s: (0, pool_blk)),
            pl.BlockSpec((None, 8, RG_CHUNK), lambda i, s: (layer, 0, chunk(s, 1))),
            pl.BlockSpec((None, 8, RG_CHUNK), lambda i, s: (layer, 0, chunk(s, 2))),
            pl.BlockSpec((None, RG_HEADS, RG_BLOCK, 2 * RG_BLOCK), lambda i, s: (layer, chunk(s, 1), 0, 0)),
            pl.BlockSpec((None, len(POOL_WINDOWS), POOL_GROUP, POOL_GROUP), lambda i, s: (layer, 0, 0, 0)),
            pl.BlockSpec((None, 1, D_POOL), lambda i, s: (layer, 0, 0)),
            pl.BlockSpec((None, CONV_ROWS, D_RNN), lambda i, s: (layer, 0, 0)),
            pl.BlockSpec((None, SUBLANES, D_RNN), lambda i, s: (layer, 0, 0)),
            pl.BlockSpec((None, POOL_ROWS, D_POOL), lambda i, s: (layer, 0, 0)),
        ],
        out_specs=(
            pl.BlockSpec((ROW_TILE, RG_CHUNK), lambda i, s: (i, chunk(s, 2))),
            pl.BlockSpec((ROW_TILE, D_POOL), lambda i, s: (i, 0)),
            pl.BlockSpec((ROW_TILE, D_MODEL), lambda i, s: (x_tile(i, s), 0)),
            pl.BlockSpec((2, N_RG, CONV_ROWS, RG_CHUNK), lambda i, s: (0, 0, 0, 0)),
            pl.BlockSpec((2, N_RG, SUBLANES, RG_CHUNK), lambda i, s: (0, 0, 0, 0)),
            pl.BlockSpec((2, POOL_ROWS, D_POOL), lambda i, s: (0, 0, 0)),
        ),
        scratch_shapes=[
            pltpu.VMEM((2, ROW_TILE, D_MODEL), BF16),
            pltpu.VMEM((N_RG, CONV_ROWS, RG_CHUNK), F32),
            pltpu.VMEM((N_RG, SUBLANES, RG_CHUNK), F32),
            pltpu.VMEM((CONV_ROWS + ROW_TILE, RG_CHUNK), F32),
            pltpu.VMEM((CONV_ROWS + ROW_TILE, RG_CHUNK), F32),
            pltpu.VMEM((ROW_TILE, RG_CHUNK), F32),
            pltpu.VMEM((ROW_TILE, RG_CHUNK), F32),
            pltpu.VMEM((ROW_TILE, RG_CHUNK), BF16),
            pltpu.VMEM((RG_HEADS, ROW_TILE, 2 * RG_BLOCK), F32),
            pltpu.VMEM((RG_HEADS, ROW_TILE, 2 * RG_BLOCK), F32),
            pltpu.VMEM((ROW_TILE, RG_CHUNK), F32),
            pltpu.VMEM((ROW_TILE, RG_CHUNK), F32),
            pltpu.VMEM((ROW_TILE, RG_CHUNK), F32),
            pltpu.VMEM((3, PAIR, RG_CHUNK), F32),
            pltpu.VMEM((POOL_ROWS + ROW_TILE, D_POOL), F32),
        ],
        compiler_params=pltpu.CompilerParams(
            dimension_semantics=("arbitrary", "arbitrary"), vmem_limit_bytes=VMEM_LIMIT),
        name=f"mixer_a_l{layer}",
    )(x, gains, w_in, w_in, w_in, rgp, rgp, wax, wpm, pscale, sconv, sh, spool)


def _mixer_b_kernel(x_ref, g_ref, hb_ref, recg_ref, pooled_ref, wgr_ref, wgp_ref, wbr_ref, wbp_ref, wo_ref,
                    o_ref, *, n_chunks):
    n = pl.program_id(1)

    def chunk(first):
        hb = hb_ref[...]
        g_rg = _dot(hb, wgr_ref[...])
        g_pool = _dot(hb, wgp_ref[...])
        br = _dot(recg_ref[...], wbr_ref[...])
        bp = _dot(pooled_ref[...], wbp_ref[...])
        mix = (_sigmoid(g_rg) * br + _sigmoid(g_pool) * bp).astype(BF16)
        for c in range(D_MODEL // ACC_COLS):
            cs = slice(c * ACC_COLS, (c + 1) * ACC_COLS)
            part = _dot(mix, wo_ref[:, cs])
            if first:
                o_ref[:, cs] = part
            else:
                o_ref[:, cs] += part

    @pl.when(n == 0)
    def _():
        chunk(True)

    @pl.when(n > 0)
    def _():
        chunk(False)

    @pl.when(n == n_chunks - 1)
    def _():
        _residual_norm(x_ref, g_ref[3:4, :], o_ref)


def _mixer_b(x, gains, hb, recg, pooled, w_in, w_br_rg, w_br_pool, w_out, layer):
    rows = x.shape[0]
    n_chunks = D_MODEL // OUT_CHUNK
    g_rg_blk = (2 * D_RNN + D_POOL) // OUT_CHUNK
    g_pool_blk = g_rg_blk + n_chunks
    kern = functools.partial(_mixer_b_kernel, n_chunks=n_chunks)
    return pl.pallas_call(
        kern,
        out_shape=jax.ShapeDtypeStruct((rows, D_MODEL), F32),
        grid=(rows // ROW_TILE, n_chunks),
        in_specs=[
            pl.BlockSpec((ROW_TILE, D_MODEL), lambda i, n: (i, 0)),
            pl.BlockSpec((None, 6, D_MODEL), lambda i, n: (layer, 0, 0)),
            pl.BlockSpec((ROW_TILE, D_MODEL), lambda i, n: (i, 0)),
            pl.BlockSpec((ROW_TILE, D_RNN), lambda i, n: (i, 0)),
            pl.BlockSpec((ROW_TILE, D_POOL), lambda i, n: (i, 0)),
            pl.BlockSpec((D_MODEL, OUT_CHUNK), lambda i, n: (0, g_rg_blk + n)),
            pl.BlockSpec((D_MODEL, OUT_CHUNK), lambda i, n: (0, g_pool_blk + n)),
            pl.BlockSpec((D_RNN, OUT_CHUNK), lambda i, n: (0, n)),
            pl.BlockSpec((D_POOL, OUT_CHUNK), lambda i, n: (0, n)),
            pl.BlockSpec((OUT_CHUNK, D_MODEL), lambda i, n: (n, 0)),
        ],
        out_specs=pl.BlockSpec((ROW_TILE, D_MODEL), lambda i, n: (i, 0)),
        compiler_params=pltpu.CompilerParams(
            dimension_semantics=("arbitrary", "arbitrary"), vmem_limit_bytes=VMEM_LIMIT),
        name=f"mixer_b_l{layer}",
    )(x, gains, hb, recg, pooled, w_in, w_in, w_br_rg, w_br_pool, w_out)


def kernel(x_prompt, x_sample, state_conv, state_h, state_pool, norm_gains, w_ffn_in, w_ffn_out, w_in,
           conv_w, conv_b, w_rg_a, b_rg_a, w_rg_x, b_rg_x, lru_param, w_pool_mix, pool_scale,
           w_br_rg, w_br_pool, w_out):
    depth = norm_gains.shape[0]
    batch, seq, _ = x_prompt.shape
    dec_batch = x_sample.shape[0]
    assert batch == SUBLANES and dec_batch == SUBLANES
    n_prompt_rows = batch * seq
    assert n_prompt_rows % ROW_TILE == 0 and x_sample.shape[1] * dec_batch == ROW_TILE
    n_prompt_tiles = n_prompt_rows // ROW_TILE

    x = (x_prompt, x_sample)

    ffn_in_b = w_ffn_in[0, 0].astype(BF16)
    ffn_out_b = w_ffn_out[0, 0].astype(BF16)
    wpm_b = w_pool_mix.astype(BF16)
    wax_b = jnp.concatenate([w_rg_a, w_rg_x], axis=-1).astype(BF16)
    rgp = jnp.concatenate(
        [conv_w, conv_b[:, None], b_rg_a[:, None], b_rg_x[:, None], lru_param[:, None]], axis=1)
    pscale = pool_scale[:, None, :]
    sconv = jnp.swapaxes(state_conv, 1, 2).reshape(depth, CONV_ROWS, D_RNN)
    spool = jnp.swapaxes(state_pool, 1, 2).reshape(depth, POOL_ROWS, D_POOL)

    convs, hs, pools = [], [], []
    for l in range(depth):
        x, (w_in_b, w_br_rg_b, w_br_pool_b, w_out_b, ffn_in_b, ffn_out_b) = _ffn(
            x, norm_gains, ffn_in_b, ffn_out_b, l, 0,
            casts=[(w_in, (l,)), (w_br_rg, (l,)), (w_br_pool, (l,)), (w_out, (l,)),
                   (w_ffn_in, (l, 1)), (w_ffn_out, (l, 1))])
        recg, pooled, hb, conv_o, h_o, pool_o = _mixer_a(
            x, norm_gains, w_in_b, rgp, wax_b, wpm_b, pscale, sconv, state_h, spool, l, n_prompt_tiles)
        x = _mixer_b(x, norm_gains, hb, recg, pooled, w_in_b, w_br_rg_b, w_br_pool_b, w_out_b, l)
        next_ffn = [(w_ffn_in, (l + 1, 0)), (w_ffn_out, (l + 1, 0))] if l + 1 < depth else []
        x, next_b = _ffn(x, norm_gains, ffn_in_b, ffn_out_b, l, 1, casts=next_ffn,
                         final_seq=seq if l + 1 == depth else None)
        if next_b:
            ffn_in_b, ffn_out_b = next_b
        convs.append(jnp.swapaxes(conv_o, 1, 2).reshape(2, CONV_ROWS, D_RNN))
        hs.append(jnp.swapaxes(h_o, 1, 2).reshape(2, SUBLANES, D_RNN))
        pools.append(pool_o)

    y_prompt, y_sample = x
    conv_all = jnp.stack(convs)
    h_all = jnp.stack(hs)
    pool_all = jnp.stack(pools)

    def unroll_state(s, which, frames):
        s = s[:, which]
        return jnp.swapaxes(s.reshape(depth, frames, SUBLANES, s.shape[-1]), 1, 2)

    return (
        y_prompt,
        y_sample,
        unroll_state(conv_all, 0, CONV_W - 1),
        h_all[:, 0],
        unroll_state(pool_all, 0, POOL_MAX - 1),
        unroll_state(conv_all, 1, CONV_W - 1),
        h_all[:, 1],
        unroll_state(pool_all, 1, POOL_MAX - 1),
    )
```

```python
import functools
import math

import jax
import jax.numpy as jnp
from jax import lax
from jax.experimental import pallas as pl
from jax.experimental.pallas import tpu as pltpu

F32 = jnp.float32
BF16 = jnp.bfloat16

D_MODEL = 2048
D_FF = 3 * D_MODEL
D_RNN = D_MODEL
D_POOL = D_MODEL // 2
N_RG_HEADS = 16
RG_BLOCK = D_RNN // N_RG_HEADS
CONV_W = 4
LRU_C = 8.0
POOL_WINDOWS = (2, 4, 8, 16)
POOL_GROUP = D_POOL // len(POOL_WINDOWS)
POOL_MAX = 16
EPS = 1e-6

SUBLANES = 8
MXU_COLS = 256
ROW_TILE = 512
FFN_CHUNK = 1024
RG_CHUNK = 512
OUT_CHUNK = 512
NORM_ROWS = 32
RESIDUAL_ROWS = 8
ACC_COLS = 512
VMEM_LIMIT = 56 * 1024 * 1024
FIRST_FFN_VMEM_LIMIT = 62 * 1024 * 1024

CONV_ROWS = (CONV_W - 1) * SUBLANES
POOL_ROWS = (POOL_MAX - 1) * SUBLANES
N_RG = D_RNN // RG_CHUNK
RG_HEADS = RG_CHUNK // RG_BLOCK
PAIR = 2 * SUBLANES
FRAMES = ROW_TILE // SUBLANES
N_GROUPS = 4
RECUR_UNROLL = 4


def _rms(x, g):
    ms = jnp.mean(x * x, axis=-1, keepdims=True)
    return x * lax.rsqrt(ms + EPS) * g


def _norm_rows(x_ref, g, dst_refs, lo, hi):
    for r in range(lo, hi, NORM_ROWS):
        rows = slice(r, r + NORM_ROWS)
        y = _rms(x_ref[rows, :], g).astype(BF16)
        for dst_ref in dst_refs:
            dst_ref[rows, :] = y


def _norm_to_bf16(x_ref, g, *dst_refs):
    _norm_rows(x_ref, g, dst_refs, 0, x_ref.shape[0])


def _residual_norm(x_ref, g, o_ref):
    for r in range(0, x_ref.shape[0], RESIDUAL_ROWS):
        rows = slice(r, r + RESIDUAL_ROWS)
        o_ref[rows, :] = x_ref[rows, :] + _rms(o_ref[rows, :], g)


def _sigmoid(x):
    return 1.0 / (1.0 + jnp.exp(-x))


def _gelu_tanh(x):
    c = math.sqrt(2.0 / math.pi)
    half = 0.5 * x
    return half + half * jnp.tanh(x * (c + (c * 0.044715) * (x * x)))


def _sqrt_nonneg(z):
    return jnp.where(z > 0.0, z * lax.rsqrt(z), 0.0)


def _dot(a, b):
    return jnp.dot(a, b, preferred_element_type=F32)


def _interleave(matmul_pieces, vector_groups):
    for k in range(max(len(matmul_pieces), len(vector_groups))):
        if k < len(matmul_pieces):
            matmul_pieces[k]()
        if k < len(vector_groups):
            vector_groups[k]()


def _ffn_kernel(*refs, pre, post, n_chunks, n_casts, n_in_tiles, n_prompt_tiles):
    if n_in_tiles is None:
        x_ref, g_ref, wg_ref, wu_ref, wo_ref = refs[:5]
        rest = refs[5:]
    else:
        xp_ref, xs_ref, g_ref, wg_ref, wu_ref, wo_ref = refs[:6]
        rest = refs[6:-1]
        x_ref = refs[-1]
    cast_src = rest[:n_casts]
    if n_prompt_tiles is None:
        o_ref = rest[n_casts]
        cast_dst = rest[n_casts + 1:2 * n_casts + 1]
        xn_ref = rest[2 * n_casts + 1]
    else:
        yp_ref, ys_ref = rest[n_casts:n_casts + 2]
        cast_dst = rest[n_casts + 2:2 * n_casts + 2]
        xn_ref, o_ref = rest[2 * n_casts + 2:2 * n_casts + 4]
    f = pl.program_id(1)

    for src, dst in zip(cast_src, cast_dst):
        dst[...] = src[...].astype(BF16)

    def chunk(first):
        xn = xn_ref[...]
        gate = _dot(xn, wg_ref[...])
        up = _dot(xn, wu_ref[...])
        hid = (gate * _sigmoid(gate) * up).astype(BF16)
        for n in range(D_MODEL // ACC_COLS):
            cs = slice(n * ACC_COLS, (n + 1) * ACC_COLS)
            part = _dot(hid, wo_ref[:, cs])
            if first:
                o_ref[:, cs] = part
            else:
                o_ref[:, cs] += part

    i = pl.program_id(0)

    if n_in_tiles is not None:
        def gather(src_ref):
            for t in range(FRAMES):
                x_ref[t * SUBLANES:(t + 1) * SUBLANES, :] = src_ref[:, t, :]

        @pl.when((f == 0) & (i < n_in_tiles))
        def _():
            gather(xp_ref)

        @pl.when((f == 0) & (i >= n_in_tiles))
        def _():
            gather(xs_ref)

    @pl.when(f == 0)
    def _():
        _norm_to_bf16(x_ref, g_ref[pre:pre + 1, :], xn_ref)
        chunk(True)

    @pl.when(f > 0)
    def _():
        chunk(False)

    if n_prompt_tiles is None:
        @pl.when(f == n_chunks - 1)
        def _():
            _residual_norm(x_ref, 0.5 * g_ref[post:post + 1, :], o_ref)
    else:

        def emit(dst_ref):
            g = 0.5 * g_ref[post:post + 1, :]
            for t in range(FRAMES):
                rows = slice(t * SUBLANES, (t + 1) * SUBLANES)
                dst_ref[:, t, :] = x_ref[rows, :] + _rms(o_ref[rows, :], g)

        @pl.when((f == n_chunks - 1) & (i < n_prompt_tiles))
        def _():
            emit(yp_ref)

        @pl.when((f == n_chunks - 1) & (i >= n_prompt_tiles))
        def _():
            emit(ys_ref)


BF16_TILE_ROWS = 16


def _ffn(x, gains, w_in, w_out, layer, which, casts=(), final_seq=None):
    from_streams = isinstance(x, tuple)
    if from_streams:
        x_prompt, x_sample = x
        n_in_tiles = x_prompt.shape[1] // FRAMES
        rows = (n_in_tiles + 1) * ROW_TILE
        x_args = [x_prompt, x_sample]
        x_specs = [
            pl.BlockSpec((SUBLANES, FRAMES, D_MODEL), lambda i, f: (0, jnp.minimum(i, n_in_tiles - 1), 0)),
            pl.BlockSpec((SUBLANES, FRAMES, D_MODEL), lambda i, f: (0, 0, 0)),
        ]
        x_scratch = [pltpu.VMEM((ROW_TILE, D_MODEL), F32)]
        vmem_limit = FIRST_FFN_VMEM_LIMIT
    else:
        n_in_tiles = None
        rows = x.shape[0]
        x_args = [x]
        x_specs = [pl.BlockSpec((ROW_TILE, D_MODEL), lambda i, f: (i, 0))]
        x_scratch = []
        vmem_limit = VMEM_LIMIT
    n_chunks = D_FF // FFN_CHUNK
    n_tiles = rows // ROW_TILE
    n_steps = n_tiles * n_chunks

    cast_in_specs, cast_out_specs, cast_out_shapes = [], [], []
    for src, lead in casts:
        m_rows, m_cols = src.shape[-2:]
        per_step = -(-m_rows // n_steps)
        blk_rows = -(-per_step // BF16_TILE_ROWS) * BF16_TILE_ROWS
        assert m_rows % blk_rows == 0
        n_blk = m_rows // blk_rows

        def blk(i, f, n_blk=n_blk):
            return jnp.minimum(i * n_chunks + f, n_blk - 1)

        cast_in_specs.append(pl.BlockSpec(
            (None,) * len(lead) + (blk_rows, m_cols), lambda i, f, lead=lead, blk=blk: lead + (blk(i, f), 0)))
        cast_out_specs.append(pl.BlockSpec((blk_rows, m_cols), lambda i, f, blk=blk: (blk(i, f), 0)))
        cast_out_shapes.append(jax.ShapeDtypeStruct((m_rows, m_cols), BF16))

    if final_seq is None:
        n_prompt_tiles = None
        main_shapes = [jax.ShapeDtypeStruct((rows, D_MODEL), F32)]
        main_specs = [pl.BlockSpec((ROW_TILE, D_MODEL), lambda i, f: (i, 0))]
        scratch = [pltpu.VMEM((ROW_TILE, D_MODEL), BF16)]
    else:
        n_prompt_tiles = n_tiles - 1
        main_shapes = [jax.ShapeDtypeStruct((SUBLANES, final_seq, D_MODEL), F32),
                       jax.ShapeDtypeStruct((SUBLANES, FRAMES, D_MODEL), F32)]
        main_specs = [
            pl.BlockSpec((SUBLANES, FRAMES, D_MODEL), lambda i, f: (0, jnp.minimum(i, n_prompt_tiles - 1), 0)),
            pl.BlockSpec((SUBLANES, FRAMES, D_MODEL), lambda i, f: (0, 0, 0)),
        ]
        scratch = [pltpu.VMEM((ROW_TILE, D_MODEL), BF16), pltpu.VMEM((ROW_TILE, D_MODEL), F32)]

    kern = functools.partial(_ffn_kernel, pre=4 * which, post=4 * which + 1, n_chunks=n_chunks,
                             n_casts=len(casts), n_in_tiles=n_in_tiles, n_prompt_tiles=n_prompt_tiles)
    outs = pl.pallas_call(
        kern,
        out_shape=main_shapes + cast_out_shapes,
        grid=(n_tiles, n_chunks),
        in_specs=x_specs + [
            pl.BlockSpec((None, 6, D_MODEL), lambda i, f: (layer, 0, 0)),
            pl.BlockSpec((D_MODEL, FFN_CHUNK), lambda i, f: (0, f)),
            pl.BlockSpec((D_MODEL, FFN_CHUNK), lambda i, f: (0, n_chunks + f)),
            pl.BlockSpec((FFN_CHUNK, D_MODEL), lambda i, f: (f, 0)),
        ] + cast_in_specs,
        out_specs=main_specs + cast_out_specs,
        scratch_shapes=scratch + x_scratch,
        compiler_params=pltpu.CompilerParams(
            dimension_semantics=("arbitrary", "arbitrary"), vmem_limit_bytes=vmem_limit),
        name=f"ffn_l{layer}_{which}",
    )(*x_args, gains, w_in, w_in, w_out, *[src for src, _ in casts])
    n_main = len(main_shapes)
    main = outs[0] if n_main == 1 else tuple(outs[:n_main])
    return main, list(outs[n_main:])


N_STEPS = N_RG + 2


def _project_pieces(hb_ref, par, wxr_ref, wy_ref, zc_ref, y_ref, j):
    def piece(q):
        def run():
            hb = hb_ref[par]
            cs = slice((q % 2) * MXU_COLS, (q % 2 + 1) * MXU_COLS)
            if q < 2:
                zc_ref[j % 2][CONV_ROWS:, cs] = _dot(hb, wxr_ref[:, cs])
            else:
                y_ref[j % 3][:, cs] = _dot(hb, wy_ref[:, cs])
        return run
    return [piece(q) for q in range(2 * RG_CHUNK // MXU_COLS)]


def _pool_project_pieces(hb_ref, par, wp_ref, zp_ref):
    def piece(q):
        def run():
            cs = slice(q * MXU_COLS, (q + 1) * MXU_COLS)
            zp_ref[POOL_ROWS:, cs] = _dot(hb_ref[par], wp_ref[:, cs])
        return run
    return [piece(q) for q in range(D_POOL // MXU_COLS)]


def _conv(j, seq, rgp_ref, zc_ref, xc_ref, xcb_ref, cpre_ref, convo_ref):
    tm = ROW_TILE
    zc = zc_ref[j % 2]
    zc[0:CONV_ROWS, :] = cpre_ref[j]
    xc = rgp_ref[4:5, :]
    for k in range(CONV_W):
        xc = xc + zc[k * SUBLANES:k * SUBLANES + tm, :] * rgp_ref[k:k + 1, :]
    new_pre = zc[tm:tm + CONV_ROWS, :]
    cpre_ref[j] = new_pre
    convo_ref[seq, j] = new_pre
    xc_ref[j % 2][...] = xc
    xcb_ref[...] = xc.astype(BF16)


def _gate_project(j, wax_ref, xcb_ref, ri_ref):
    for hh in range(RG_HEADS):
        sl = slice(hh * RG_BLOCK, (hh + 1) * RG_BLOCK)
        ri_ref[j % 2][hh] = _dot(xcb_ref[:, sl], wax_ref[hh])


def _recur_groups(j, seq, rgp_ref, ri_ref, xc_ref, y_ref, hst_ref, cst_ref, recg_ref, ho_ref):
    tm = ROW_TILE
    carry = {}

    def setup():
        neg_lam = -rgp_ref[7:8, :]
        softplus = jnp.maximum(neg_lam, 0.0) + jnp.log1p(jnp.exp(-jnp.abs(neg_lam)))
        cst_ref[0] = jnp.broadcast_to(rgp_ref[5:6, :], (PAIR, RG_CHUNK))
        cst_ref[1] = jnp.broadcast_to(rgp_ref[6:7, :], (PAIR, RG_CHUNK))
        cst_ref[2] = jnp.broadcast_to(-LRU_C * softplus, (PAIR, RG_CHUNK))
        carry["h"] = hst_ref[j]

    def pair(r0, h):
        rows = pl.ds(r0, PAIR)
        a_parts, u_parts = [], []
        for hh in range(RG_HEADS):
            sl = slice(hh * RG_BLOCK, (hh + 1) * RG_BLOCK)
            ri = ri_ref[j % 2][hh, rows, :]
            r = _sigmoid(ri[:, :RG_BLOCK] + cst_ref[0, :, sl])
            ig = _sigmoid(ri[:, RG_BLOCK:] + cst_ref[1, :, sl])
            log_a = cst_ref[2, :, sl] * r
            a = jnp.exp(log_a)
            u_parts.append(_sqrt_nonneg(-jnp.tanh(log_a) * (a * a + 1.0)) * (ig * xc_ref[j % 2][rows, sl]))
            a_parts.append(a)
        a = jnp.concatenate(a_parts, axis=1)
        u = jnp.concatenate(u_parts, axis=1)
        h1 = a[:SUBLANES] * h + u[:SUBLANES]
        h2 = a[SUBLANES:] * h1 + u[SUBLANES:]
        rec = jnp.concatenate([h1, h2], axis=0)
        recg_ref[rows, :] = (rec * _gelu_tanh(y_ref[j % 3][rows, :])).astype(BF16)
        return h2

    def group(k):
        def run():
            if k == 0:
                setup()
            span = tm // N_GROUPS

            def body(p, h):
                return pair(pl.multiple_of(k * span + p * PAIR, PAIR), h)

            carry["h"] = lax.fori_loop(0, span // PAIR, body, carry["h"], unroll=RECUR_UNROLL)
            if k == N_GROUPS - 1:
                hst_ref[j] = carry["h"]
                ho_ref[seq, j] = carry["h"]
        return run

    return [group(k) for k in range(N_GROUPS)]


def _pool_groups(i, seq, n_prompt_tiles, wpm_ref, ps_ref, zp_ref, pooled_ref, poolo_ref):
    tm = ROW_TILE
    frames = tm // SUBLANES

    def group(g):
        def run():
            w = POOL_WINDOWS[g]
            frame = lax.shift_right_logical(lax.broadcasted_iota(jnp.int32, (tm, POOL_GROUP), 0), 3)
            seen = jnp.where(i >= n_prompt_tiles, POOL_MAX - 1, i * frames) + 1
            seen = (frame + seen).astype(F32)
            cs = slice(g * POOL_GROUP, (g + 1) * POOL_GROUP)
            s = zp_ref[POOL_ROWS - (w - 1) * SUBLANES:, cs]
            shift = SUBLANES
            while shift < w * SUBLANES:
                s = s[shift:, :] + s[:s.shape[0] - shift, :]
                shift *= 2
            mean = s / jnp.minimum(seen, float(w))
            pooled = (mean - zp_ref[POOL_ROWS:, cs]).astype(BF16)
            mixed = _dot(pooled, wpm_ref[g]) * ps_ref[:, cs]
            pooled_ref[:, cs] = mixed.astype(BF16)
            if g == len(POOL_WINDOWS) - 1:
                new_pre = zp_ref[tm:tm + POOL_ROWS, :]
                zp_ref[0:POOL_ROWS, :] = new_pre
                poolo_ref[seq] = new_pre
        return run

    return [group(g) for g in range(len(POOL_WINDOWS))]


def _mixer_a_kernel(x_ref, g_ref, wxr_ref, wy_ref, wp_ref, rgp_conv_ref, rgp_recur_ref, wax_ref, wpm_ref,
                    ps_ref, sconv_ref, sh_ref, spool_ref,
                    recg_ref, pooled_ref, hbn_ref, convo_ref, ho_ref, poolo_ref,
                    hb_ref, cpre_ref, hst_ref, zc0_ref, zc1_ref, xc0_ref, xc1_ref, xcb_ref, ri0_ref, ri1_ref,
                    y0_ref, y1_ref, y2_ref, cst_ref, zp_ref,
                    *, n_prompt_tiles):
    zc_ref = (zc0_ref, zc1_ref)
    xc_ref = (xc0_ref, xc1_ref)
    ri_ref = (ri0_ref, ri1_ref)
    y_ref = (y0_ref, y1_ref, y2_ref)
    i = pl.program_id(0)
    s = pl.program_id(1)
    seq = (i >= n_prompt_tiles).astype(jnp.int32)
    par = i % 2

    def project(j):
        return _project_pieces(hb_ref, par, wxr_ref, wy_ref, zc_ref, y_ref, j)

    def conv(j):
        return lambda: _conv(j, seq, rgp_conv_ref, zc_ref, xc_ref, xcb_ref, cpre_ref, convo_ref)

    def gate_project(j):
        return lambda: _gate_project(j, wax_ref, xcb_ref, ri_ref)

    def recur(j):
        return _recur_groups(j, seq, rgp_recur_ref, ri_ref, xc_ref, y_ref, hst_ref, cst_ref, recg_ref, ho_ref)

    def norm_next():
        span = ROW_TILE // N_GROUPS
        return [functools.partial(_norm_rows, x_ref, g_ref[2:3, :], (hb_ref.at[1 - par], hbn_ref),
                                  k * span, (k + 1) * span)
                for k in range(N_GROUPS)]

    @pl.when(s == 0)
    def _():
        @pl.when(i == 0)
        def _():
            cpre_ref[...] = jnp.zeros(cpre_ref.shape, F32)
            hst_ref[...] = jnp.zeros(hst_ref.shape, F32)
            zp_ref[0:POOL_ROWS, :] = jnp.zeros((POOL_ROWS, D_POOL), F32)
            _norm_to_bf16(x_ref, g_ref[2:3, :], hb_ref.at[0], hbn_ref)

        @pl.when(i == n_prompt_tiles)
        def _():
            for j in range(N_RG):
                cs = slice(j * RG_CHUNK, (j + 1) * RG_CHUNK)
                cpre_ref[j] = sconv_ref[:, cs]
                hst_ref[j] = sh_ref[:, cs]
            zp_ref[0:POOL_ROWS, :] = spool_ref[...]

        for piece in project(0):
            piece()

    def stages(pieces, conv_stage, groups, gate_stage):
        for piece in pieces:
            piece()
        conv_stage()
        gate_stage()
        for group in groups:
            group()

    @pl.when(s == 1)
    def _():
        stages(project(1), conv(0), norm_next(), gate_project(0))

    for step in range(2, N_RG):
        @pl.when(s == step)
        def _(step=step):
            stages(project(step), conv(step - 1), recur(step - 2), gate_project(step - 1))

    @pl.when(s == N_RG)
    def _():
        stages(_pool_project_pieces(hb_ref, par, wp_ref, zp_ref), conv(N_RG - 1), recur(N_RG - 2),
               gate_project(N_RG - 1))

    @pl.when(s == N_RG + 1)
    def _():
        for group in _pool_groups(i, seq, n_prompt_tiles, wpm_ref, ps_ref, zp_ref, pooled_ref, poolo_ref):
            group()
        for group in recur(N_RG - 1):
            group()


def _mixer_a(x, gains, w_in, rgp, wax, wpm, pscale, sconv, sh, spool, layer, n_prompt_tiles):
    rows = x.shape[0]
    n_tiles = rows // ROW_TILE
    pool_blk = (2 * D_RNN) // D_POOL

    def chunk(s, lag):
        return jnp.clip(s - lag, 0, N_RG - 1)

    def x_tile(i, s):
        return jnp.minimum(i + (s >= 1).astype(jnp.int32), n_tiles - 1)

    kern = functools.partial(_mixer_a_kernel, n_prompt_tiles=n_prompt_tiles)
    return pl.pallas_call(
        kern,
        out_shape=(
            jax.ShapeDtypeStruct((rows, D_RNN), BF16),
            jax.ShapeDtypeStruct((rows, D_POOL), BF16),
            jax.ShapeDtypeStruct((rows, D_MODEL), BF16),
            jax.ShapeDtypeStruct((2, N_RG, CONV_ROWS, RG_CHUNK), F32),
            jax.ShapeDtypeStruct((2, N_RG, SUBLANES, RG_CHUNK), F32),
            jax.ShapeDtypeStruct((2, POOL_ROWS, D_POOL), F32),
        ),
        grid=(n_tiles, N_STEPS),
        in_specs=[
            pl.BlockSpec((ROW_TILE, D_MODEL), lambda i, s: (x_tile(i, s), 0)),
            pl.BlockSpec((None, 6, D_MODEL), lambda i, s: (layer, 0, 0)),
            pl.BlockSpec((D_MODEL, RG_CHUNK), lambda i, s: (0, chunk(s, 0))),
            pl.BlockSpec((D_MODEL, RG_CHUNK), lambda i, s: (0, N_RG + chunk(s, 0))),
            pl.BlockSpec((D_MODEL, D_POOL), lambda i, s: (0, pool_blk)),
            pl.BlockSpec((None, 8, RG_CHUNK), lambda i, s: (layer, 0, chunk(s, 1))),
            pl.BlockSpec((None, 8, RG_CHUNK), lambda i, s: (layer, 0, chunk(s, 2))),
            pl.BlockSpec((None, RG_HEADS, RG_BLOCK, 2 * RG_BLOCK), lambda i, s: (layer, chunk(s, 1), 0, 0)),
            pl.BlockSpec((None, len(POOL_WINDOWS), POOL_GROUP, POOL_GROUP), lambda i, s: (layer, 0, 0, 0)),
            pl.BlockSpec((None, 1, D_POOL), lambda i, s: (layer, 0, 0)),
            pl.BlockSpec((None, CONV_ROWS, D_RNN), lambda i, s: (layer, 0, 0)),
            pl.BlockSpec((None, SUBLANES, D_RNN), lambda i, s: (layer, 0, 0)),
            pl.BlockSpec((None, POOL_ROWS, D_POOL), lambda i, s: (layer, 0, 0)),
        ],
        out_specs=(
            pl.BlockSpec((ROW_TILE, RG_CHUNK), lambda i, s: (i, chunk(s, 2))),
            pl.BlockSpec((ROW_TILE, D_POOL), lambda i, s: (i, 0)),
            pl.BlockSpec((ROW_TILE, D_MODEL), lambda i, s: (x_tile(i, s), 0)),
            pl.BlockSpec((2, N_RG, CONV_ROWS, RG_CHUNK), lambda i, s: (0, 0, 0, 0)),
            pl.BlockSpec((2, N_RG, SUBLANES, RG_CHUNK), lambda i, s: (0, 0, 0, 0)),
            pl.BlockSpec((2, POOL_ROWS, D_POOL), lambda i, s: (0, 0, 0)),
        ),
        scratch_shapes=[
            pltpu.VMEM((2, ROW_TILE, D_MODEL), BF16),
            pltpu.VMEM((N_RG, CONV_ROWS, RG_CHUNK), F32),
            pltpu.VMEM((N_RG, SUBLANES, RG_CHUNK), F32),
            pltpu.VMEM((CONV_ROWS + ROW_TILE, RG_CHUNK), F32),
            pltpu.VMEM((CONV_ROWS + ROW_TILE, RG_CHUNK), F32),
            pltpu.VMEM((ROW_TILE, RG_CHUNK), F32),
            pltpu.VMEM((ROW_TILE, RG_CHUNK), F32),
            pltpu.VMEM((ROW_TILE, RG_CHUNK), BF16),
            pltpu.VMEM((RG_HEADS, ROW_TILE, 2 * RG_BLOCK), F32),
            pltpu.VMEM((RG_HEADS, ROW_TILE, 2 * RG_BLOCK), F32),
            pltpu.VMEM((ROW_TILE, RG_CHUNK), F32),
            pltpu.VMEM((ROW_TILE, RG_CHUNK), F32),
            pltpu.VMEM((ROW_TILE, RG_CHUNK), F32),
            pltpu.VMEM((3, PAIR, RG_CHUNK), F32),
            pltpu.VMEM((POOL_ROWS + ROW_TILE, D_POOL), F32),
        ],
        compiler_params=pltpu.CompilerParams(
            dimension_semantics=("arbitrary", "arbitrary"), vmem_limit_bytes=VMEM_LIMIT),
        name=f"mixer_a_l{layer}",
    )(x, gains, w_in, w_in, w_in, rgp, rgp, wax, wpm, pscale, sconv, sh, spool)


def _mixer_b_kernel(x_ref, g_ref, hb_ref, recg_ref, pooled_ref, wgr_ref, wgp_ref, wbr_ref, wbp_ref, wo_ref,
                    o_ref, *, n_chunks):
    n = pl.program_id(1)

    def chunk(first):
        hb = hb_ref[...]
        g_rg = _dot(hb, wgr_ref[...])
        g_pool = _dot(hb, wgp_ref[...])
        br = _dot(recg_ref[...], wbr_ref[...])
        bp = _dot(pooled_ref[...], wbp_ref[...])
        mix = (_sigmoid(g_rg) * br + _sigmoid(g_pool) * bp).astype(BF16)
        for c in range(D_MODEL // ACC_COLS):
            cs = slice(c * ACC_COLS, (c + 1) * ACC_COLS)
            part = _dot(mix, wo_ref[:, cs])
            if first:
                o_ref[:, cs] = part
            else:
                o_ref[:, cs] += part

    @pl.when(n == 0)
    def _():
        chunk(True)

    @pl.when(n > 0)
    def _():
        chunk(False)

    @pl.when(n == n_chunks - 1)
    def _():
        _residual_norm(x_ref, g_ref[3:4, :], o_ref)


def _mixer_b(x, gains, hb, recg, pooled, w_in, w_br_rg, w_br_pool, w_out, layer):
    rows = x.shape[0]
    n_chunks = D_MODEL // OUT_CHUNK
    g_rg_blk = (2 * D_RNN + D_POOL) // OUT_CHUNK
    g_pool_blk = g_rg_blk + n_chunks
    kern = functools.partial(_mixer_b_kernel, n_chunks=n_chunks)
    return pl.pallas_call(
        kern,
        out_shape=jax.ShapeDtypeStruct((rows, D_MODEL), F32),
        grid=(rows // ROW_TILE, n_chunks),
        in_specs=[
            pl.BlockSpec((ROW_TILE, D_MODEL), lambda i, n: (i, 0)),
            pl.BlockSpec((None, 6, D_MODEL), lambda i, n: (layer, 0, 0)),
            pl.BlockSpec((ROW_TILE, D_MODEL), lambda i, n: (i, 0)),
            pl.BlockSpec((ROW_TILE, D_RNN), lambda i, n: (i, 0)),
            pl.BlockSpec((ROW_TILE, D_POOL), lambda i, n: (i, 0)),
            pl.BlockSpec((D_MODEL, OUT_CHUNK), lambda i, n: (0, g_rg_blk + n)),
            pl.BlockSpec((D_MODEL, OUT_CHUNK), lambda i, n: (0, g_pool_blk + n)),
            pl.BlockSpec((D_RNN, OUT_CHUNK), lambda i, n: (0, n)),
            pl.BlockSpec((D_POOL, OUT_CHUNK), lambda i, n: (0, n)),
            pl.BlockSpec((OUT_CHUNK, D_MODEL), lambda i, n: (n, 0)),
        ],
        out_specs=pl.BlockSpec((ROW_TILE, D_MODEL), lambda i, n: (i, 0)),
        compiler_params=pltpu.CompilerParams(
            dimension_semantics=("arbitrary", "arbitrary"), vmem_limit_bytes=VMEM_LIMIT),
        name=f"mixer_b_l{layer}",
    )(x, gains, hb, recg, pooled, w_in, w_in, w_br_rg, w_br_pool, w_out)


def kernel(x_prompt, x_sample, state_conv, state_h, state_pool, norm_gains, w_ffn_in, w_ffn_out, w_in,
           conv_w, conv_b, w_rg_a, b_rg_a, w_rg_x, b_rg_x, lru_param, w_pool_mix, pool_scale,
           w_br_rg, w_br_pool, w_out):
    depth = norm_gains.shape[0]
    batch, seq, _ = x_prompt.shape
    dec_batch = x_sample.shape[0]
    assert batch == SUBLANES and dec_batch == SUBLANES
    n_prompt_rows = batch * seq
    assert n_prompt_rows % ROW_TILE == 0 and x_sample.shape[1] * dec_batch == ROW_TILE
    n_prompt_tiles = n_prompt_rows // ROW_TILE

    x = (x_prompt, x_sample)

    ffn_in_b = w_ffn_in[0, 0].astype(BF16)
    ffn_out_b = w_ffn_out[0, 0].astype(BF16)
    wpm_b = w_pool_mix.astype(BF16)
    wax_b = jnp.concatenate([w_rg_a, w_rg_x], axis=-1).astype(BF16)
    rgp = jnp.concatenate(
        [conv_w, conv_b[:, None], b_rg_a[:, None], b_rg_x[:, None], lru_param[:, None]], axis=1)
    pscale = pool_scale[:, None, :]
    sconv = jnp.swapaxes(state_conv, 1, 2).reshape(depth, CONV_ROWS, D_RNN)
    spool = jnp.swapaxes(state_pool, 1, 2).reshape(depth, POOL_ROWS, D_POOL)

    convs, hs, pools = [], [], []
    for l in range(depth):
        x, (w_in_b, w_br_rg_b, w_br_pool_b, w_out_b, ffn_in_b, ffn_out_b) = _ffn(
            x, norm_gains, ffn_in_b, ffn_out_b, l, 0,
            casts=[(w_in, (l,)), (w_br_rg, (l,)), (w_br_pool, (l,)), (w_out, (l,)),
                   (w_ffn_in, (l, 1)), (w_ffn_out, (l, 1))])
        recg, pooled, hb, conv_o, h_o, pool_o = _mixer_a(
            x, norm_gains, w_in_b, rgp, wax_b, wpm_b, pscale, sconv, state_h, spool, l, n_prompt_tiles)
        x = _mixer_b(x, norm_gains, hb, recg, pooled, w_in_b, w_br_rg_b, w_br_pool_b, w_out_b, l)
        next_ffn = [(w_ffn_in, (l + 1, 0)), (w_ffn_out, (l + 1, 0))] if l + 1 < depth else []
        x, next_b = _ffn(x, norm_gains, ffn_in_b, ffn_out_b, l, 1, casts=next_ffn,
                         final_seq=seq if l + 1 == depth else None)
        if next_b:
            ffn_in_b, ffn_out_b = next_b
        convs.append(jnp.swapaxes(conv_o, 1, 2).reshape(2, CONV_ROWS, D_RNN))
        hs.append(jnp.swapaxes(h_o, 1, 2).reshape(2, SUBLANES, D_RNN))
        pools.append(pool_o)

    y_prompt, y_sample = x
    conv_all = jnp.stack(convs)
    h_all = jnp.stack(hs)
    pool_all = jnp.stack(pools)

    def unroll_state(s, which, frames):
        s = s[:, which]
        return jnp.swapaxes(s.reshape(depth, frames, SUBLANES, s.shape[-1]), 1, 2)

    return (
        y_prompt,
        y_sample,
        unroll_state(conv_all, 0, CONV_W - 1),
        h_all[:, 0],
        unroll_state(pool_all, 0, POOL_MAX - 1),
        unroll_state(conv_all, 1, CONV_W - 1),
        h_all[:, 1],
        unroll_state(pool_all, 1, POOL_MAX - 1),
    )
```

```python
import functools
import math

import jax
import jax.numpy as jnp
from jax import lax
from jax.experimental import pallas as pl
from jax.experimental.pallas import tpu as pltpu

F32 = jnp.float32
BF16 = jnp.bfloat16

D_MODEL = 2048
D_FF = 3 * D_MODEL
D_RNN = D_MODEL
D_POOL = D_MODEL // 2
N_RG_HEADS = 16
RG_BLOCK = D_RNN // N_RG_HEADS
CONV_W = 4
LRU_C = 8.0
POOL_WINDOWS = (2, 4, 8, 16)
POOL_GROUP = D_POOL // len(POOL_WINDOWS)
POOL_MAX = 16
EPS = 1e-6

SUBLANES = 8
MXU_COLS = 256
ROW_TILE = 512
FFN_CHUNK = 1024
RG_CHUNK = 512
OUT_CHUNK = 512
NORM_ROWS = 32
RESIDUAL_ROWS = 8
ACC_COLS = 512
VMEM_LIMIT = 56 * 1024 * 1024
FIRST_FFN_VMEM_LIMIT = 62 * 1024 * 1024

CONV_ROWS = (CONV_W - 1) * SUBLANES
POOL_ROWS = (POOL_MAX - 1) * SUBLANES
N_RG = D_RNN // RG_CHUNK
RG_HEADS = RG_CHUNK // RG_BLOCK
PAIR = 2 * SUBLANES
FRAMES = ROW_TILE // SUBLANES
N_GROUPS = 4
RECUR_UNROLL = 4


def _rms(x, g):
    ms = jnp.mean(x * x, axis=-1, keepdims=True)
    return x * lax.rsqrt(ms + EPS) * g


def _norm_rows(x_ref, g, dst_refs, lo, hi):
    for r in range(lo, hi, NORM_ROWS):
        rows = slice(r, r + NORM_ROWS)
        y = _rms(x_ref[rows, :], g).astype(BF16)
        for dst_ref in dst_refs:
            dst_ref[rows, :] = y


def _norm_to_bf16(x_ref, g, *dst_refs):
    _norm_rows(x_ref, g, dst_refs, 0, x_ref.shape[0])


def _residual_norm(x_ref, g, o_ref):
    for r in range(0, x_ref.shape[0], RESIDUAL_ROWS):
        rows = slice(r, r + RESIDUAL_ROWS)
        o_ref[rows, :] = x_ref[rows, :] + _rms(o_ref[rows, :], g)


def _sigmoid(x):
    return 1.0 / (1.0 + jnp.exp(-x))


def _gelu_tanh(x):
    c = math.sqrt(2.0 / math.pi)
    half = 0.5 * x
    return half + half * jnp.tanh(x * (c + (c * 0.044715) * (x * x)))


def _sqrt_nonneg(z):
    return jnp.where(z > 0.0, z * lax.rsqrt(z), 0.0)


def _dot(a, b):
    return jnp.dot(a, b, preferred_element_type=F32)


def _interleave(matmul_pieces, vector_groups):
    for k in range(max(len(matmul_pieces), len(vector_groups))):
        if k < len(matmul_pieces):
            matmul_pieces[k]()
        if k < len(vector_groups):
            vector_groups[k]()


def _ffn_kernel(*refs, pre, post, n_chunks, n_casts, n_in_tiles, n_prompt_tiles):
    if n_in_tiles is None:
        x_ref, g_ref, wg_ref, wu_ref, wo_ref = refs[:5]
        rest = refs[5:]
    else:
        xp_ref, xs_ref, g_ref, wg_ref, wu_ref, wo_ref = refs[:6]
        rest = refs[6:-1]
        x_ref = refs[-1]
    cast_src = rest[:n_casts]
    if n_prompt_tiles is None:
        o_ref = rest[n_casts]
        cast_dst = rest[n_casts + 1:2 * n_casts + 1]
        xn_ref = rest[2 * n_casts + 1]
    else:
        yp_ref, ys_ref = rest[n_casts:n_casts + 2]
        cast_dst = rest[n_casts + 2:2 * n_casts + 2]
        xn_ref, o_ref = rest[2 * n_casts + 2:2 * n_casts + 4]
    f = pl.program_id(1)

    for src, dst in zip(cast_src, cast_dst):
        dst[...] = src[...].astype(BF16)

    def chunk(first):
        xn = xn_ref[...]
        gate = _dot(xn, wg_ref[...])
        up = _dot(xn, wu_ref[...])
        hid = (gate * _sigmoid(gate) * up).astype(BF16)
        for n in range(D_MODEL // ACC_COLS):
            cs = slice(n * ACC_COLS, (n + 1) * ACC_COLS)
            part = _dot(hid, wo_ref[:, cs])
            if first:
                o_ref[:, cs] = part
            else:
                o_ref[:, cs] += part

    i = pl.program_id(0)

    if n_in_tiles is not None:
        def gather(src_ref):
            for t in range(FRAMES):
                x_ref[t * SUBLANES:(t + 1) * SUBLANES, :] = src_ref[:, t, :]

        @pl.when((f == 0) & (i < n_in_tiles))
        def _():
            gather(xp_ref)

        @pl.when((f == 0) & (i >= n_in_tiles))
        def _():
            gather(xs_ref)

    @pl.when(f == 0)
    def _():
        _norm_to_bf16(x_ref, g_ref[pre:pre + 1, :], xn_ref)
        chunk(True)

    @pl.when(f > 0)
    def _():
        chunk(False)

    if n_prompt_tiles is None:
        @pl.when(f == n_chunks - 1)
        def _():
            _residual_norm(x_ref, 0.5 * g_ref[post:post + 1, :], o_ref)
    else:

        def emit(dst_ref):
            g = 0.5 * g_ref[post:post + 1, :]
            for t in range(FRAMES):
                rows = slice(t * SUBLANES, (t + 1) * SUBLANES)
                dst_ref[:, t, :] = x_ref[rows, :] + _rms(o_ref[rows, :], g)

        @pl.when((f == n_chunks - 1) & (i < n_prompt_tiles))
        def _():
            emit(yp_ref)

        @pl.when((f == n_chunks - 1) & (i >= n_prompt_tiles))
        def _():
            emit(ys_ref)


BF16_TILE_ROWS = 16


def _ffn(x, gains, w_in, w_out, layer, which, casts=(), final_seq=None):
    from_streams = isinstance(x, tuple)
    if from_streams:
        x_prompt, x_sample = x
        n_in_tiles = x_prompt.shape[1] // FRAMES
        rows = (n_in_tiles + 1) * ROW_TILE
        x_args = [x_prompt, x_sample]
        x_specs = [
            pl.BlockSpec((SUBLANES, FRAMES, D_MODEL), lambda i, f: (0, jnp.minimum(i, n_in_tiles - 1), 0)),
            pl.BlockSpec((SUBLANES, FRAMES, D_MODEL), lambda i, f: (0, 0, 0)),
        ]
        x_scratch = [pltpu.VMEM((ROW_TILE, D_MODEL), F32)]
        vmem_limit = FIRST_FFN_VMEM_LIMIT
    else:
        n_in_tiles = None
        rows = x.shape[0]
        x_args = [x]
        x_specs = [pl.BlockSpec((ROW_TILE, D_MODEL), lambda i, f: (i, 0))]
        x_scratch = []
        vmem_limit = VMEM_LIMIT
    n_chunks = D_FF // FFN_CHUNK
    n_tiles = rows // ROW_TILE
    n_steps = n_tiles * n_chunks

    cast_in_specs, cast_out_specs, cast_out_shapes = [], [], []
    for src, lead in casts:
        m_rows, m_cols = src.shape[-2:]
        per_step = -(-m_rows // n_steps)
        blk_rows = -(-per_step // BF16_TILE_ROWS) * BF16_TILE_ROWS
        assert m_rows % blk_rows == 0
        n_blk = m_rows // blk_rows

        def blk(i, f, n_blk=n_blk):
            return jnp.minimum(i * n_chunks + f, n_blk - 1)

        cast_in_specs.append(pl.BlockSpec(
            (None,) * len(lead) + (blk_rows, m_cols), lambda i, f, lead=lead, blk=blk: lead + (blk(i, f), 0)))
        cast_out_specs.append(pl.BlockSpec((blk_rows, m_cols), lambda i, f, blk=blk: (blk(i, f), 0)))
        cast_out_shapes.append(jax.ShapeDtypeStruct((m_rows, m_cols), BF16))

    if final_seq is None:
        n_prompt_tiles = None
        main_shapes = [jax.ShapeDtypeStruct((rows, D_MODEL), F32)]
        main_specs = [pl.BlockSpec((ROW_TILE, D_MODEL), lambda i, f: (i, 0))]
        scratch = [pltpu.VMEM((ROW_TILE, D_MODEL), BF16)]
    else:
        n_prompt_tiles = n_tiles - 1
        main_shapes = [jax.ShapeDtypeStruct((SUBLANES, final_seq, D_MODEL), F32),
                       jax.ShapeDtypeStruct((SUBLANES, FRAMES, D_MODEL), F32)]
        main_specs = [
            pl.BlockSpec((SUBLANES, FRAMES, D_MODEL), lambda i, f: (0, jnp.minimum(i, n_prompt_tiles - 1), 0)),
            pl.BlockSpec((SUBLANES, FRAMES, D_MODEL), lambda i, f: (0, 0, 0)),
        ]
        scratch = [pltpu.VMEM((ROW_TILE, D_MODEL), BF16), pltpu.VMEM((ROW_TILE, D_MODEL), F32)]

    kern = functools.partial(_ffn_kernel, pre=4 * which, post=4 * which + 1, n_chunks=n_chunks,
                             n_casts=len(casts), n_in_tiles=n_in_tiles, n_prompt_tiles=n_prompt_tiles)
    outs = pl.pallas_call(
        kern,
        out_shape=main_shapes + cast_out_shapes,
        grid=(n_tiles, n_chunks),
        in_specs=x_specs + [
            pl.BlockSpec((None, 6, D_MODEL), lambda i, f: (layer, 0, 0)),
            pl.BlockSpec((D_MODEL, FFN_CHUNK), lambda i, f: (0, f)),
            pl.BlockSpec((D_MODEL, FFN_CHUNK), lambda i, f: (0, n_chunks + f)),
            pl.BlockSpec((FFN_CHUNK, D_MODEL), lambda i, f: (f, 0)),
        ] + cast_in_specs,
        out_specs=main_specs + cast_out_specs,
        scratch_shapes=scratch + x_scratch,
        compiler_params=pltpu.CompilerParams(
            dimension_semantics=("arbitrary", "arbitrary"), vmem_limit_bytes=vmem_limit),
        name=f"ffn_l{layer}_{which}",
    )(*x_args, gains, w_in, w_in, w_out, *[src for src, _ in casts])
    n_main = len(main_shapes)
    main = outs[0] if n_main == 1 else tuple(outs[:n_main])
    return main, list(outs[n_main:])


N_STEPS = N_RG + 2


def _project_pieces(hb_ref, par, wxr_ref, wy_ref, zc_ref, y_ref, j):
    def piece(q):
        def run():
            hb = hb_ref[par]
            cs = slice((q % 2) * MXU_COLS, (q % 2 + 1) * MXU_COLS)
            if q < 2:
                zc_ref[j % 2][CONV_ROWS:, cs] = _dot(hb, wxr_ref[:, cs])
            else:
                y_ref[j % 3][:, cs] = _dot(hb, wy_ref[:, cs])
        return run
    return [piece(q) for q in range(2 * RG_CHUNK // MXU_COLS)]


def _pool_project_pieces(hb_ref, par, wp_ref, zp_ref):
    def piece(q):
        def run():
            cs = slice(q * MXU_COLS, (q + 1) * MXU_COLS)
            zp_ref[POOL_ROWS:, cs] = _dot(hb_ref[par], wp_ref[:, cs])
        return run
    return [piece(q) for q in range(D_POOL // MXU_COLS)]


def _conv(j, seq, rgp_ref, zc_ref, xc_ref, xcb_ref, cpre_ref, convo_ref):
    tm = ROW_TILE
    zc = zc_ref[j % 2]
    zc[0:CONV_ROWS, :] = cpre_ref[j]
    xc = rgp_ref[4:5, :]
    for k in range(CONV_W):
        xc = xc + zc[k * SUBLANES:k * SUBLANES + tm, :] * rgp_ref[k:k + 1, :]
    new_pre = zc[tm:tm + CONV_ROWS, :]
    cpre_ref[j] = new_pre
    convo_ref[seq, j] = new_pre
    xc_ref[j % 2][...] = xc
    xcb_ref[...] = xc.astype(BF16)


def _gate_project(j, wax_ref, xcb_ref, ri_ref):
    for hh in range(RG_HEADS):
        sl = slice(hh * RG_BLOCK, (hh + 1) * RG_BLOCK)
        ri_ref[j % 2][hh] = _dot(xcb_ref[:, sl], wax_ref[hh])


def _recur_groups(j, seq, rgp_ref, ri_ref, xc_ref, y_ref, hst_ref, cst_ref, recg_ref, ho_ref):
    tm = ROW_TILE
    carry = {}

    def setup():
        neg_lam = -rgp_ref[7:8, :]
        softplus = jnp.maximum(neg_lam, 0.0) + jnp.log1p(jnp.exp(-jnp.abs(neg_lam)))
        cst_ref[0] = jnp.broadcast_to(0.5 * rgp_ref[5:6, :], (PAIR, RG_CHUNK))
        cst_ref[1] = jnp.broadcast_to(0.5 * rgp_ref[6:7, :], (PAIR, RG_CHUNK))
        cst_ref[2] = jnp.broadcast_to((-0.5 * LRU_C) * softplus, (PAIR, RG_CHUNK))
        carry["h"] = hst_ref[j]

    def pair(r0, h):
        rows = pl.ds(r0, PAIR)
        a_parts, u_parts = [], []
        for hh in range(RG_HEADS):
            sl = slice(hh * RG_BLOCK, (hh + 1) * RG_BLOCK)
            ri = ri_ref[j % 2][hh, rows, :]
            t_r = jnp.tanh(0.5 * ri[:, :RG_BLOCK] + cst_ref[0, :, sl])
            t_i = jnp.tanh(0.5 * ri[:, RG_BLOCK:] + cst_ref[1, :, sl])
            ch = cst_ref[2, :, sl]
            log_a = ch * t_r + ch
            a = jnp.exp(log_a)
            half_xc = 0.5 * xc_ref[j % 2][rows, sl]
            gated = half_xc * t_i + half_xc
            u_parts.append(_sqrt_nonneg(-jnp.tanh(log_a) * (a * a + 1.0)) * gated)
            a_parts.append(a)
        a = jnp.concatenate(a_parts, axis=1)
        u = jnp.concatenate(u_parts, axis=1)
        h1 = a[:SUBLANES] * h + u[:SUBLANES]
        h2 = a[SUBLANES:] * h1 + u[SUBLANES:]
        rec = jnp.concatenate([h1, h2], axis=0)
        recg_ref[rows, :] = (rec * _gelu_tanh(y_ref[j % 3][rows, :])).astype(BF16)
        return h2

    def group(k):
        def run():
            if k == 0:
                setup()
            span = tm // N_GROUPS

            def body(p, h):
                return pair(pl.multiple_of(k * span + p * PAIR, PAIR), h)

            carry["h"] = lax.fori_loop(0, span // PAIR, body, carry["h"], unroll=RECUR_UNROLL)
            if k == N_GROUPS - 1:
                hst_ref[j] = carry["h"]
                ho_ref[seq, j] = carry["h"]
        return run

    return [group(k) for k in range(N_GROUPS)]


def _pool_groups(i, seq, n_prompt_tiles, wpm_ref, ps_ref, zp_ref, pooled_ref, poolo_ref):
    tm = ROW_TILE
    frames = tm // SUBLANES

    def group(g):
        def run():
            w = POOL_WINDOWS[g]
            frame = lax.shift_right_logical(lax.broadcasted_iota(jnp.int32, (tm, POOL_GROUP), 0), 3)
            seen = jnp.where(i >= n_prompt_tiles, POOL_MAX - 1, i * frames) + 1
            seen = (frame + seen).astype(F32)
            cs = slice(g * POOL_GROUP, (g + 1) * POOL_GROUP)
            s = zp_ref[POOL_ROWS - (w - 1) * SUBLANES:, cs]
            shift = SUBLANES
            while shift < w * SUBLANES:
                s = s[shift:, :] + s[:s.shape[0] - shift, :]
                shift *= 2
            mean = s / jnp.minimum(seen, float(w))
            pooled = (mean - zp_ref[POOL_ROWS:, cs]).astype(BF16)
            mixed = _dot(pooled, wpm_ref[g]) * ps_ref[:, cs]
            pooled_ref[:, cs] = mixed.astype(BF16)
            if g == len(POOL_WINDOWS) - 1:
                new_pre = zp_ref[tm:tm + POOL_ROWS, :]
                zp_ref[0:POOL_ROWS, :] = new_pre
                poolo_ref[seq] = new_pre
        return run

    return [group(g) for g in range(len(POOL_WINDOWS))]


def _mixer_a_kernel(x_ref, g_ref, wxr_ref, wy_ref, wp_ref, rgp_conv_ref, rgp_recur_ref, wax_ref, wpm_ref,
                    ps_ref, sconv_ref, sh_ref, spool_ref,
                    recg_ref, pooled_ref, hbn_ref, convo_ref, ho_ref, poolo_ref,
                    hb_ref, cpre_ref, hst_ref, zc0_ref, zc1_ref, xc0_ref, xc1_ref, xcb_ref, ri0_ref, ri1_ref,
                    y0_ref, y1_ref, y2_ref, cst_ref, zp_ref,
                    *, n_prompt_tiles):
    zc_ref = (zc0_ref, zc1_ref)
    xc_ref = (xc0_ref, xc1_ref)
    ri_ref = (ri0_ref, ri1_ref)
    y_ref = (y0_ref, y1_ref, y2_ref)
    i = pl.program_id(0)
    s = pl.program_id(1)
    seq = (i >= n_prompt_tiles).astype(jnp.int32)
    par = i % 2

    def project(j):
        return _project_pieces(hb_ref, par, wxr_ref, wy_ref, zc_ref, y_ref, j)

    def conv(j):
        return lambda: _conv(j, seq, rgp_conv_ref, zc_ref, xc_ref, xcb_ref, cpre_ref, convo_ref)

    def gate_project(j):
        return lambda: _gate_project(j, wax_ref, xcb_ref, ri_ref)

    def recur(j):
        return _recur_groups(j, seq, rgp_recur_ref, ri_ref, xc_ref, y_ref, hst_ref, cst_ref, recg_ref, ho_ref)

    def norm_next():
        span = ROW_TILE // N_GROUPS
        return [functools.partial(_norm_rows, x_ref, g_ref[2:3, :], (hb_ref.at[1 - par], hbn_ref),
                                  k * span, (k + 1) * span)
                for k in range(N_GROUPS)]

    @pl.when(s == 0)
    def _():
        @pl.when(i == 0)
        def _():
            cpre_ref[...] = jnp.zeros(cpre_ref.shape, F32)
            hst_ref[...] = jnp.zeros(hst_ref.shape, F32)
            zp_ref[0:POOL_ROWS, :] = jnp.zeros((POOL_ROWS, D_POOL), F32)
            _norm_to_bf16(x_ref, g_ref[2:3, :], hb_ref.at[0], hbn_ref)

        @pl.when(i == n_prompt_tiles)
        def _():
            for j in range(N_RG):
                cs = slice(j * RG_CHUNK, (j + 1) * RG_CHUNK)
                cpre_ref[j] = sconv_ref[:, cs]
                hst_ref[j] = sh_ref[:, cs]
            zp_ref[0:POOL_ROWS, :] = spool_ref[...]

        for piece in project(0):
            piece()

    def stages(pieces, conv_stage, groups, gate_stage):
        for piece in pieces:
            piece()
        conv_stage()
        gate_stage()
        for group in groups:
            group()

    @pl.when(s == 1)
    def _():
        stages(project(1), conv(0), norm_next(), gate_project(0))

    for step in range(2, N_RG):
        @pl.when(s == step)
        def _(step=step):
            stages(project(step), conv(step - 1), recur(step - 2), gate_project(step - 1))

    @pl.when(s == N_RG)
    def _():
        stages(_pool_project_pieces(hb_ref, par, wp_ref, zp_ref), conv(N_RG - 1), recur(N_RG - 2),
               gate_project(N_RG - 1))

    @pl.when(s == N_RG + 1)
    def _():
        for group in _pool_groups(i, seq, n_prompt_tiles, wpm_ref, ps_ref, zp_ref, pooled_ref, poolo_ref):
            group()
        for group in recur(N_RG - 1):
            group()


def _mixer_a(x, gains, w_in, rgp, wax, wpm, pscale, sconv, sh, spool, layer, n_prompt_tiles):
    rows = x.shape[0]
    n_tiles = rows // ROW_TILE
    pool_blk = (2 * D_RNN) // D_POOL

    def chunk(s, lag):
        return jnp.clip(s - lag, 0, N_RG - 1)

    def x_tile(i, s):
        return jnp.minimum(i + (s >= 1).astype(jnp.int32), n_tiles - 1)

    kern = functools.partial(_mixer_a_kernel, n_prompt_tiles=n_prompt_tiles)
    return pl.pallas_call(
        kern,
        out_shape=(
            jax.ShapeDtypeStruct((rows, D_RNN), BF16),
            jax.ShapeDtypeStruct((rows, D_POOL), BF16),
            jax.ShapeDtypeStruct((rows, D_MODEL), BF16),
            jax.ShapeDtypeStruct((2, N_RG, CONV_ROWS, RG_CHUNK), F32),
            jax.ShapeDtypeStruct((2, N_RG, SUBLANES, RG_CHUNK), F32),
            jax.ShapeDtypeStruct((2, POOL_ROWS, D_POOL), F32),
        ),
        grid=(n_tiles, N_STEPS),
        in_specs=[
            pl.BlockSpec((ROW_TILE, D_MODEL), lambda i, s: (x_tile(i, s), 0)),
            pl.BlockSpec((None, 6, D_MODEL), lambda i, s: (layer, 0, 0)),
            pl.BlockSpec((D_MODEL, RG_CHUNK), lambda i, s: (0, chunk(s, 0))),
            pl.BlockSpec((D_MODEL, RG_CHUNK), lambda i, s: (0, N_RG + chunk(s, 0))),
            pl.BlockSpec((D_MODEL, D_POOL), lambda i, s: (0, pool_blk)),
            pl.BlockSpec((None, 8, RG_CHUNK), lambda i, s: (layer, 0, chunk(s, 1))),
            pl.BlockSpec((None, 8, RG_CHUNK), lambda i, s: (layer, 0, chunk(s, 2))),
            pl.BlockSpec((None, RG_HEADS, RG_BLOCK, 2 * RG_BLOCK), lambda i, s: (layer, chunk(s, 1), 0, 0)),
            pl.BlockSpec((None, len(POOL_WINDOWS), POOL_GROUP, POOL_GROUP), lambda i, s: (layer, 0, 0, 0)),
            pl.BlockSpec((None, 1, D_POOL), lambda i, s: (layer, 0, 0)),
            pl.BlockSpec((None, CONV_ROWS, D_RNN), lambda i, s: (layer, 0, 0)),
            pl.BlockSpec((None, SUBLANES, D_RNN), lambda i, s: (layer, 0, 0)),
            pl.BlockSpec((None, POOL_ROWS, D_POOL), lambda i, s: (layer, 0, 0)),
        ],
        out_specs=(
            pl.BlockSpec((ROW_TILE, RG_CHUNK), lambda i, s: (i, chunk(s, 2))),
            pl.BlockSpec((ROW_TILE, D_POOL), lambda i, s: (i, 0)),
            pl.BlockSpec((ROW_TILE, D_MODEL), lambda i, s: (x_tile(i, s), 0)),
            pl.BlockSpec((2, N_RG, CONV_ROWS, RG_CHUNK), lambda i, s: (0, 0, 0, 0)),
            pl.BlockSpec((2, N_RG, SUBLANES, RG_CHUNK), lambda i, s: (0, 0, 0, 0)),
            pl.BlockSpec((2, POOL_ROWS, D_POOL), lambda i, s: (0, 0, 0)),
        ),
        scratch_shapes=[
            pltpu.VMEM((2, ROW_TILE, D_MODEL), BF16),
            pltpu.VMEM((N_RG, CONV_ROWS, RG_CHUNK), F32),
            pltpu.VMEM((N_RG, SUBLANES, RG_CHUNK), F32),
            pltpu.VMEM((CONV_ROWS + ROW_TILE, RG_CHUNK), F32),
            pltpu.VMEM((CONV_ROWS + ROW_TILE, RG_CHUNK), F32),
            pltpu.VMEM((ROW_TILE, RG_CHUNK), F32),
            pltpu.VMEM((ROW_TILE, RG_CHUNK), F32),
            pltpu.VMEM((ROW_TILE, RG_CHUNK), BF16),
            pltpu.VMEM((RG_HEADS, ROW_TILE, 2 * RG_BLOCK), F32),
            pltpu.VMEM((RG_HEADS, ROW_TILE, 2 * RG_BLOCK), F32),
            pltpu.VMEM((ROW_TILE, RG_CHUNK), F32),
            pltpu.VMEM((ROW_TILE, RG_CHUNK), F32),
            pltpu.VMEM((ROW_TILE, RG_CHUNK), F32),
            pltpu.VMEM((3, PAIR, RG_CHUNK), F32),
            pltpu.VMEM((POOL_ROWS + ROW_TILE, D_POOL), F32),
        ],
        compiler_params=pltpu.CompilerParams(
            dimension_semantics=("arbitrary", "arbitrary"), vmem_limit_bytes=VMEM_LIMIT),
        name=f"mixer_a_l{layer}",
    )(x, gains, w_in, w_in, w_in, rgp, rgp, wax, wpm, pscale, sconv, sh, spool)


def _mixer_b_kernel(x_ref, g_ref, hb_ref, recg_ref, pooled_ref, wgr_ref, wgp_ref, wbr_ref, wbp_ref, wo_ref,
                    o_ref, *, n_chunks):
    n = pl.program_id(1)

    def chunk(first):
        hb = hb_ref[...]
        g_rg = _dot(hb, wgr_ref[...])
        g_pool = _dot(hb, wgp_ref[...])
        br = _dot(recg_ref[...], wbr_ref[...])
        bp = _dot(pooled_ref[...], wbp_ref[...])
        mix = (_sigmoid(g_rg) * br + _sigmoid(g_pool) * bp).astype(BF16)
        for c in range(D_MODEL // ACC_COLS):
            cs = slice(c * ACC_COLS, (c + 1) * ACC_COLS)
            part = _dot(mix, wo_ref[:, cs])
            if first:
                o_ref[:, cs] = part
            else:
                o_ref[:, cs] += part

    @pl.when(n == 0)
    def _():
        chunk(True)

    @pl.when(n > 0)
    def _():
        chunk(False)

    @pl.when(n == n_chunks - 1)
    def _():
        _residual_norm(x_ref, g_ref[3:4, :], o_ref)


def _mixer_b(x, gains, hb, recg, pooled, w_in, w_br_rg, w_br_pool, w_out, layer):
    rows = x.shape[0]
    n_chunks = D_MODEL // OUT_CHUNK
    g_rg_blk = (2 * D_RNN + D_POOL) // OUT_CHUNK
    g_pool_blk = g_rg_blk + n_chunks
    kern = functools.partial(_mixer_b_kernel, n_chunks=n_chunks)
    return pl.pallas_call(
        kern,
        out_shape=jax.ShapeDtypeStruct((rows, D_MODEL), F32),
        grid=(rows // ROW_TILE, n_chunks),
        in_specs=[
            pl.BlockSpec((ROW_TILE, D_MODEL), lambda i, n: (i, 0)),
            pl.BlockSpec((None, 6, D_MODEL), lambda i, n: (layer, 0, 0)),
            pl.BlockSpec((ROW_TILE, D_MODEL), lambda i, n: (i, 0)),
            pl.BlockSpec((ROW_TILE, D_RNN), lambda i, n: (i, 0)),
            pl.BlockSpec((ROW_TILE, D_POOL), lambda i, n: (i, 0)),
            pl.BlockSpec((D_MODEL, OUT_CHUNK), lambda i, n: (0, g_rg_blk + n)),
            pl.BlockSpec((D_MODEL, OUT_CHUNK), lambda i, n: (0, g_pool_blk + n)),
            pl.BlockSpec((D_RNN, OUT_CHUNK), lambda i, n: (0, n)),
            pl.BlockSpec((D_POOL, OUT_CHUNK), lambda i, n: (0, n)),
            pl.BlockSpec((OUT_CHUNK, D_MODEL), lambda i, n: (n, 0)),
        ],
        out_specs=pl.BlockSpec((ROW_TILE, D_MODEL), lambda i, n: (i, 0)),
        compiler_params=pltpu.CompilerParams(
            dimension_semantics=("arbitrary", "arbitrary"), vmem_limit_bytes=VMEM_LIMIT),
        name=f"mixer_b_l{layer}",
    )(x, gains, hb, recg, pooled, w_in, w_in, w_br_rg, w_br_pool, w_out)


def kernel(x_prompt, x_sample, state_conv, state_h, state_pool, norm_gains, w_ffn_in, w_ffn_out, w_in,
           conv_w, conv_b, w_rg_a, b_rg_a, w_rg_x, b_rg_x, lru_param, w_pool_mix, pool_scale,
           w_br_rg, w_br_pool, w_out):
    depth = norm_gains.shape[0]
    batch, seq, _ = x_prompt.shape
    dec_batch = x_sample.shape[0]
    assert batch == SUBLANES and dec_batch == SUBLANES
    n_prompt_rows = batch * seq
    assert n_prompt_rows % ROW_TILE == 0 and x_sample.shape[1] * dec_batch == ROW_TILE
    n_prompt_tiles = n_prompt_rows // ROW_TILE

    x = (x_prompt, x_sample)

    ffn_in_b = w_ffn_in[0, 0].astype(BF16)
    ffn_out_b = w_ffn_out[0, 0].astype(BF16)
    wpm_b = w_pool_mix.astype(BF16)
    wax_b = jnp.concatenate([w_rg_a, w_rg_x], axis=-1).astype(BF16)
    rgp = jnp.concatenate(
        [conv_w, conv_b[:, None], b_rg_a[:, None], b_rg_x[:, None], lru_param[:, None]], axis=1)
    pscale = pool_scale[:, None, :]
    sconv = jnp.swapaxes(state_conv, 1, 2).reshape(depth, CONV_ROWS, D_RNN)
    spool = jnp.swapaxes(state_pool, 1, 2).reshape(depth, POOL_ROWS, D_POOL)

    convs, hs, pools = [], [], []
    for l in range(depth):
        x, (w_in_b, w_br_rg_b, w_br_pool_b, w_out_b, ffn_in_b, ffn_out_b) = _ffn(
            x, norm_gains, ffn_in_b, ffn_out_b, l, 0,
            casts=[(w_in, (l,)), (w_br_rg, (l,)), (w_br_pool, (l,)), (w_out, (l,)),
                   (w_ffn_in, (l, 1)), (w_ffn_out, (l, 1))])
        recg, pooled, hb, conv_o, h_o, pool_o = _mixer_a(
            x, norm_gains, w_in_b, rgp, wax_b, wpm_b, pscale, sconv, state_h, spool, l, n_prompt_tiles)
        x = _mixer_b(x, norm_gains, hb, recg, pooled, w_in_b, w_br_rg_b, w_br_pool_b, w_out_b, l)
        next_ffn = [(w_ffn_in, (l + 1, 0)), (w_ffn_out, (l + 1, 0))] if l + 1 < depth else []
        x, next_b = _ffn(x, norm_gains, ffn_in_b, ffn_out_b, l, 1, casts=next_ffn,
                         final_seq=seq if l + 1 == depth else None)
        if next_b:
            ffn_in_b, ffn_out_b = next_b
        convs.append(jnp.swapaxes(conv_o, 1, 2).reshape(2, CONV_ROWS, D_RNN))
        hs.append(jnp.swapaxes(h_o, 1, 2).reshape(2, SUBLANES, D_RNN))
        pools.append(pool_o)

    y_prompt, y_sample = x
    conv_all = jnp.stack(convs)
    h_all = jnp.stack(hs)
    pool_all = jnp.stack(pools)

    def unroll_state(s, which, frames):
        s = s[:, which]
        return jnp.swapaxes(s.reshape(depth, frames, SUBLANES, s.shape[-1]), 1, 2)

    return (
        y_prompt,
        y_sample,
        unroll_state(conv_all, 0, CONV_W - 1),
        h_all[:, 0],
        unroll_state(pool_all, 0, POOL_MAX - 1),
        unroll_state(conv_all, 1, CONV_W - 1),
        h_all[:, 1],
        unroll_state(pool_all, 1, POOL_MAX - 1),
    )
```

```python
import functools
import math

import jax
import jax.numpy as jnp
from jax import lax
from jax.experimental import pallas as pl
from jax.experimental.pallas import tpu as pltpu

F32 = jnp.float32
BF16 = jnp.bfloat16

D_MODEL = 2048
D_FF = 3 * D_MODEL
D_RNN = D_MODEL
D_POOL = D_MODEL // 2
N_RG_HEADS = 16
RG_BLOCK = D_RNN // N_RG_HEADS
CONV_W = 4
LRU_C = 8.0
POOL_WINDOWS = (2, 4, 8, 16)
POOL_GROUP = D_POOL // len(POOL_WINDOWS)
POOL_MAX = 16
EPS = 1e-6
LOG2_E = 1.4426950408889634

SUBLANES = 8
MXU_COLS = 256
ROW_TILE = 512
FFN_CHUNK = 1024
RG_CHUNK = 512
OUT_CHUNK = 512
NORM_ROWS = 32
RESIDUAL_ROWS = 8
ACC_COLS = 512
VMEM_LIMIT = 56 * 1024 * 1024
FIRST_FFN_VMEM_LIMIT = 62 * 1024 * 1024

CONV_ROWS = (CONV_W - 1) * SUBLANES
POOL_ROWS = (POOL_MAX - 1) * SUBLANES
N_RG = D_RNN // RG_CHUNK
RG_HEADS = RG_CHUNK // RG_BLOCK
PAIR = 2 * SUBLANES
FRAMES = ROW_TILE // SUBLANES
N_GROUPS = 4
RECUR_UNROLL = 4


def _rms(x, g):
    ms = jnp.mean(x * x, axis=-1, keepdims=True)
    return x * lax.rsqrt(ms + EPS) * g


def _norm_rows(x_ref, g, dst_refs, lo, hi):
    for r in range(lo, hi, NORM_ROWS):
        rows = slice(r, r + NORM_ROWS)
        y = _rms(x_ref[rows, :], g).astype(BF16)
        for dst_ref in dst_refs:
            dst_ref[rows, :] = y


def _norm_to_bf16(x_ref, g, *dst_refs):
    _norm_rows(x_ref, g, dst_refs, 0, x_ref.shape[0])


def _residual_norm(x_ref, g, o_ref):
    for r in range(0, x_ref.shape[0], RESIDUAL_ROWS):
        rows = slice(r, r + RESIDUAL_ROWS)
        o_ref[rows, :] = x_ref[rows, :] + _rms(o_ref[rows, :], g)


def _sigmoid(x):
    return 1.0 / (1.0 + jnp.exp(-x))


def _gelu_tanh(x):
    c = math.sqrt(2.0 / math.pi)
    half = 0.5 * x
    return half + half * jnp.tanh(x * (c + (c * 0.044715) * (x * x)))


def _sqrt_nonneg(z):
    return jnp.where(z > 0.0, z * lax.rsqrt(z), 0.0)


def _dot(a, b):
    return jnp.dot(a, b, preferred_element_type=F32)


def _interleave(matmul_pieces, vector_groups):
    for k in range(max(len(matmul_pieces), len(vector_groups))):
        if k < len(matmul_pieces):
            matmul_pieces[k]()
        if k < len(vector_groups):
            vector_groups[k]()


def _ffn_kernel(*refs, pre, post, n_chunks, n_casts, n_in_tiles, n_prompt_tiles):
    if n_in_tiles is None:
        x_ref, g_ref, wg_ref, wu_ref, wo_ref = refs[:5]
        rest = refs[5:]
    else:
        xp_ref, xs_ref, g_ref, wg_ref, wu_ref, wo_ref = refs[:6]
        rest = refs[6:-1]
        x_ref = refs[-1]
    cast_src = rest[:n_casts]
    if n_prompt_tiles is None:
        o_ref = rest[n_casts]
        cast_dst = rest[n_casts + 1:2 * n_casts + 1]
        xn_ref = rest[2 * n_casts + 1]
    else:
        yp_ref, ys_ref = rest[n_casts:n_casts + 2]
        cast_dst = rest[n_casts + 2:2 * n_casts + 2]
        xn_ref, o_ref = rest[2 * n_casts + 2:2 * n_casts + 4]
    f = pl.program_id(1)

    for src, dst in zip(cast_src, cast_dst):
        dst[...] = src[...].astype(BF16)

    def chunk(first):
        xn = xn_ref[...]
        gate = _dot(xn, wg_ref[...])
        up = _dot(xn, wu_ref[...])
        hid = (gate * _sigmoid(gate) * up).astype(BF16)
        for n in range(D_MODEL // ACC_COLS):
            cs = slice(n * ACC_COLS, (n + 1) * ACC_COLS)
            part = _dot(hid, wo_ref[:, cs])
            if first:
                o_ref[:, cs] = part
            else:
                o_ref[:, cs] += part

    i = pl.program_id(0)

    if n_in_tiles is not None:
        def gather(src_ref):
            for t in range(FRAMES):
                x_ref[t * SUBLANES:(t + 1) * SUBLANES, :] = src_ref[:, t, :]

        @pl.when((f == 0) & (i < n_in_tiles))
        def _():
            gather(xp_ref)

        @pl.when((f == 0) & (i >= n_in_tiles))
        def _():
            gather(xs_ref)

    @pl.when(f == 0)
    def _():
        _norm_to_bf16(x_ref, g_ref[pre:pre + 1, :], xn_ref)
        chunk(True)

    @pl.when(f > 0)
    def _():
        chunk(False)

    if n_prompt_tiles is None:
        @pl.when(f == n_chunks - 1)
        def _():
            _residual_norm(x_ref, 0.5 * g_ref[post:post + 1, :], o_ref)
    else:

        def emit(dst_ref):
            g = 0.5 * g_ref[post:post + 1, :]
            for t in range(FRAMES):
                rows = slice(t * SUBLANES, (t + 1) * SUBLANES)
                dst_ref[:, t, :] = x_ref[rows, :] + _rms(o_ref[rows, :], g)

        @pl.when((f == n_chunks - 1) & (i < n_prompt_tiles))
        def _():
            emit(yp_ref)

        @pl.when((f == n_chunks - 1) & (i >= n_prompt_tiles))
        def _():
            emit(ys_ref)


BF16_TILE_ROWS = 16


def _ffn(x, gains, w_in, w_out, layer, which, casts=(), final_seq=None):
    from_streams = isinstance(x, tuple)
    if from_streams:
        x_prompt, x_sample = x
        n_in_tiles = x_prompt.shape[1] // FRAMES
        rows = (n_in_tiles + 1) * ROW_TILE
        x_args = [x_prompt, x_sample]
        x_specs = [
            pl.BlockSpec((SUBLANES, FRAMES, D_MODEL), lambda i, f: (0, jnp.minimum(i, n_in_tiles - 1), 0)),
            pl.BlockSpec((SUBLANES, FRAMES, D_MODEL), lambda i, f: (0, 0, 0)),
        ]
        x_scratch = [pltpu.VMEM((ROW_TILE, D_MODEL), F32)]
        vmem_limit = FIRST_FFN_VMEM_LIMIT
    else:
        n_in_tiles = None
        rows = x.shape[0]
        x_args = [x]
        x_specs = [pl.BlockSpec((ROW_TILE, D_MODEL), lambda i, f: (i, 0))]
        x_scratch = []
        vmem_limit = VMEM_LIMIT
    n_chunks = D_FF // FFN_CHUNK
    n_tiles = rows // ROW_TILE
    n_steps = n_tiles * n_chunks

    cast_in_specs, cast_out_specs, cast_out_shapes = [], [], []
    for src, lead in casts:
        m_rows, m_cols = src.shape[-2:]
        per_step = -(-m_rows // n_steps)
        blk_rows = -(-per_step // BF16_TILE_ROWS) * BF16_TILE_ROWS
        assert m_rows % blk_rows == 0
        n_blk = m_rows // blk_rows

        def blk(i, f, n_blk=n_blk):
            return jnp.minimum(i * n_chunks + f, n_blk - 1)

        cast_in_specs.append(pl.BlockSpec(
            (None,) * len(lead) + (blk_rows, m_cols), lambda i, f, lead=lead, blk=blk: lead + (blk(i, f), 0)))
        cast_out_specs.append(pl.BlockSpec((blk_rows, m_cols), lambda i, f, blk=blk: (blk(i, f), 0)))
        cast_out_shapes.append(jax.ShapeDtypeStruct((m_rows, m_cols), BF16))

    if final_seq is None:
        n_prompt_tiles = None
        main_shapes = [jax.ShapeDtypeStruct((rows, D_MODEL), F32)]
        main_specs = [pl.BlockSpec((ROW_TILE, D_MODEL), lambda i, f: (i, 0))]
        scratch = [pltpu.VMEM((ROW_TILE, D_MODEL), BF16)]
    else:
        n_prompt_tiles = n_tiles - 1
        main_shapes = [jax.ShapeDtypeStruct((SUBLANES, final_seq, D_MODEL), F32),
                       jax.ShapeDtypeStruct((SUBLANES, FRAMES, D_MODEL), F32)]
        main_specs = [
            pl.BlockSpec((SUBLANES, FRAMES, D_MODEL), lambda i, f: (0, jnp.minimum(i, n_prompt_tiles - 1), 0)),
            pl.BlockSpec((SUBLANES, FRAMES, D_MODEL), lambda i, f: (0, 0, 0)),
        ]
        scratch = [pltpu.VMEM((ROW_TILE, D_MODEL), BF16), pltpu.VMEM((ROW_TILE, D_MODEL), F32)]

    kern = functools.partial(_ffn_kernel, pre=4 * which, post=4 * which + 1, n_chunks=n_chunks,
                             n_casts=len(casts), n_in_tiles=n_in_tiles, n_prompt_tiles=n_prompt_tiles)
    outs = pl.pallas_call(
        kern,
        out_shape=main_shapes + cast_out_shapes,
        grid=(n_tiles, n_chunks),
        in_specs=x_specs + [
            pl.BlockSpec((None, 6, D_MODEL), lambda i, f: (layer, 0, 0)),
            pl.BlockSpec((D_MODEL, FFN_CHUNK), lambda i, f: (0, f)),
            pl.BlockSpec((D_MODEL, FFN_CHUNK), lambda i, f: (0, n_chunks + f)),
            pl.BlockSpec((FFN_CHUNK, D_MODEL), lambda i, f: (f, 0)),
        ] + cast_in_specs,
        out_specs=main_specs + cast_out_specs,
        scratch_shapes=scratch + x_scratch,
        compiler_params=pltpu.CompilerParams(
            dimension_semantics=("arbitrary", "arbitrary"), vmem_limit_bytes=vmem_limit),
        name=f"ffn_l{layer}_{which}",
    )(*x_args, gains, w_in, w_in, w_out, *[src for src, _ in casts])
    n_main = len(main_shapes)
    main = outs[0] if n_main == 1 else tuple(outs[:n_main])
    return main, list(outs[n_main:])


N_STEPS = N_RG + 2


def _project_pieces(hb_ref, par, wxr_ref, wy_ref, zc_ref, y_ref, j):
    def piece(q):
        def run():
            hb = hb_ref[par]
            cs = slice((q % 2) * MXU_COLS, (q % 2 + 1) * MXU_COLS)
            if q < 2:
                zc_ref[j % 2][CONV_ROWS:, cs] = _dot(hb, wxr_ref[:, cs])
            else:
                y_ref[j % 3][:, cs] = _dot(hb, wy_ref[:, cs])
        return run
    return [piece(q) for q in range(2 * RG_CHUNK // MXU_COLS)]


def _pool_project_pieces(hb_ref, par, wp_ref, zp_ref):
    def piece(q):
        def run():
            cs = slice(q * MXU_COLS, (q + 1) * MXU_COLS)
            zp_ref[POOL_ROWS:, cs] = _dot(hb_ref[par], wp_ref[:, cs])
        return run
    return [piece(q) for q in range(D_POOL // MXU_COLS)]


def _conv(j, seq, rgp_ref, zc_ref, xc_ref, xcb_ref, cpre_ref, convo_ref):
    tm = ROW_TILE
    zc = zc_ref[j % 2]
    zc[0:CONV_ROWS, :] = cpre_ref[j]
    half_xc = 0.5 * rgp_ref[4:5, :]
    for k in range(CONV_W):
        half_xc = half_xc + zc[k * SUBLANES:k * SUBLANES + tm, :] * (0.5 * rgp_ref[k:k + 1, :])
    new_pre = zc[tm:tm + CONV_ROWS, :]
    cpre_ref[j] = new_pre
    convo_ref[seq, j] = new_pre
    xc_ref[j % 2][...] = half_xc
    xcb_ref[...] = half_xc.astype(BF16)


def _gate_project(j, wax_ref, xcb_ref, ri_ref):
    for hh in range(RG_HEADS):
        sl = slice(hh * RG_BLOCK, (hh + 1) * RG_BLOCK)
        ri_ref[j % 2][hh] = _dot(xcb_ref[:, sl], wax_ref[hh])


def _recur_groups(j, seq, rgp_ref, ri_ref, xc_ref, y_ref, hst_ref, cst_ref, recg_ref, ho_ref):
    tm = ROW_TILE
    carry = {}

    def setup():
        neg_lam = -rgp_ref[7:8, :]
        softplus = jnp.maximum(neg_lam, 0.0) + jnp.log1p(jnp.exp(-jnp.abs(neg_lam)))
        cst_ref[0] = jnp.broadcast_to(0.5 * rgp_ref[5:6, :], (PAIR, RG_CHUNK))
        cst_ref[1] = jnp.broadcast_to(0.5 * rgp_ref[6:7, :], (PAIR, RG_CHUNK))
        cst_ref[2] = jnp.broadcast_to((0.5 * LRU_C) * softplus, (PAIR, RG_CHUNK))
        carry["h"] = hst_ref[j]

    def pair(r0, h):
        rows = pl.ds(r0, PAIR)
        a_parts, u_parts = [], []
        for hh in range(RG_HEADS):
            sl = slice(hh * RG_BLOCK, (hh + 1) * RG_BLOCK)
            ri = ri_ref[j % 2][hh, rows, :]
            t_r = jnp.tanh(ri[:, :RG_BLOCK] + cst_ref[0, :, sl])
            t_i = jnp.tanh(ri[:, RG_BLOCK:] + cst_ref[1, :, sl])
            ch = cst_ref[2, :, sl]
            neg_log_a = ch * t_r + ch
            a = jnp.exp2(neg_log_a * (-LOG2_E))
            half_xc = xc_ref[j % 2][rows, sl]
            gated = half_xc * t_i + half_xc
            u_parts.append(_sqrt_nonneg(jnp.tanh(neg_log_a) * (a * a + 1.0)) * gated)
            a_parts.append(a)
        a = jnp.concatenate(a_parts, axis=1)
        u = jnp.concatenate(u_parts, axis=1)
        h1 = a[:SUBLANES] * h + u[:SUBLANES]
        h2 = a[SUBLANES:] * h1 + u[SUBLANES:]
        rec = jnp.concatenate([h1, h2], axis=0)
        recg_ref[rows, :] = (rec * _gelu_tanh(y_ref[j % 3][rows, :])).astype(BF16)
        return h2

    def group(k):
        def run():
            if k == 0:
                setup()
            span = tm // N_GROUPS

            def body(p, h):
                return pair(pl.multiple_of(k * span + p * PAIR, PAIR), h)

            carry["h"] = lax.fori_loop(0, span // PAIR, body, carry["h"], unroll=RECUR_UNROLL)
            if k == N_GROUPS - 1:
                hst_ref[j] = carry["h"]
                ho_ref[seq, j] = carry["h"]
        return run

    return [group(k) for k in range(N_GROUPS)]


def _pool_groups(i, seq, n_prompt_tiles, wpm_ref, ps_ref, zp_ref, pooled_ref, poolo_ref):
    tm = ROW_TILE
    frames = tm // SUBLANES

    def group(g):
        def run():
            w = POOL_WINDOWS[g]
            frame = lax.shift_right_logical(lax.broadcasted_iota(jnp.int32, (tm, POOL_GROUP), 0), 3)
            seen = jnp.where(i >= n_prompt_tiles, POOL_MAX - 1, i * frames) + 1
            seen = (frame + seen).astype(F32)
            cs = slice(g * POOL_GROUP, (g + 1) * POOL_GROUP)
            s = zp_ref[POOL_ROWS - (w - 1) * SUBLANES:, cs]
            shift = SUBLANES
            while shift < w * SUBLANES:
                s = s[shift:, :] + s[:s.shape[0] - shift, :]
                shift *= 2
            mean = s / jnp.minimum(seen, float(w))
            pooled = (mean - zp_ref[POOL_ROWS:, cs]).astype(BF16)
            mixed = _dot(pooled, wpm_ref[g]) * ps_ref[:, cs]
            pooled_ref[:, cs] = mixed.astype(BF16)
            if g == len(POOL_WINDOWS) - 1:
                new_pre = zp_ref[tm:tm + POOL_ROWS, :]
                zp_ref[0:POOL_ROWS, :] = new_pre
                poolo_ref[seq] = new_pre
        return run

    return [group(g) for g in range(len(POOL_WINDOWS))]


def _mixer_a_kernel(x_ref, g_ref, wxr_ref, wy_ref, wp_ref, rgp_conv_ref, rgp_recur_ref, wax_ref, wpm_ref,
                    ps_ref, sconv_ref, sh_ref, spool_ref,
                    recg_ref, pooled_ref, hbn_ref, convo_ref, ho_ref, poolo_ref,
                    hb_ref, cpre_ref, hst_ref, zc0_ref, zc1_ref, xc0_ref, xc1_ref, xcb_ref, ri0_ref, ri1_ref,
                    y0_ref, y1_ref, y2_ref, cst_ref, zp_ref,
                    *, n_prompt_tiles):
    zc_ref = (zc0_ref, zc1_ref)
    xc_ref = (xc0_ref, xc1_ref)
    ri_ref = (ri0_ref, ri1_ref)
    y_ref = (y0_ref, y1_ref, y2_ref)
    i = pl.program_id(0)
    s = pl.program_id(1)
    seq = (i >= n_prompt_tiles).astype(jnp.int32)
    par = i % 2

    def project(j):
        return _project_pieces(hb_ref, par, wxr_ref, wy_ref, zc_ref, y_ref, j)

    def conv(j):
        return lambda: _conv(j, seq, rgp_conv_ref, zc_ref, xc_ref, xcb_ref, cpre_ref, convo_ref)

    def gate_project(j):
        return lambda: _gate_project(j, wax_ref, xcb_ref, ri_ref)

    def recur(j):
        return _recur_groups(j, seq, rgp_recur_ref, ri_ref, xc_ref, y_ref, hst_ref, cst_ref, recg_ref, ho_ref)

    def norm_next():
        span = ROW_TILE // N_GROUPS
        return [functools.partial(_norm_rows, x_ref, g_ref[2:3, :], (hb_ref.at[1 - par], hbn_ref),
                                  k * span, (k + 1) * span)
                for k in range(N_GROUPS)]

    @pl.when(s == 0)
    def _():
        @pl.when(i == 0)
        def _():
            cpre_ref[...] = jnp.zeros(cpre_ref.shape, F32)
            hst_ref[...] = jnp.zeros(hst_ref.shape, F32)
            zp_ref[0:POOL_ROWS, :] = jnp.zeros((POOL_ROWS, D_POOL), F32)
            _norm_to_bf16(x_ref, g_ref[2:3, :], hb_ref.at[0], hbn_ref)

        @pl.when(i == n_prompt_tiles)
        def _():
            for j in range(N_RG):
                cs = slice(j * RG_CHUNK, (j + 1) * RG_CHUNK)
                cpre_ref[j] = sconv_ref[:, cs]
                hst_ref[j] = sh_ref[:, cs]
            zp_ref[0:POOL_ROWS, :] = spool_ref[...]

        for piece in project(0):
            piece()

    def stages(pieces, conv_stage, groups, gate_stage):
        for piece in pieces:
            piece()
        conv_stage()
        gate_stage()
        for group in groups:
            group()

    @pl.when(s == 1)
    def _():
        stages(project(1), conv(0), norm_next(), gate_project(0))

    for step in range(2, N_RG):
        @pl.when(s == step)
        def _(step=step):
            stages(project(step), conv(step - 1), recur(step - 2), gate_project(step - 1))

    @pl.when(s == N_RG)
    def _():
        stages(_pool_project_pieces(hb_ref, par, wp_ref, zp_ref), conv(N_RG - 1), recur(N_RG - 2),
               gate_project(N_RG - 1))

    @pl.when(s == N_RG + 1)
    def _():
        for group in _pool_groups(i, seq, n_prompt_tiles, wpm_ref, ps_ref, zp_ref, pooled_ref, poolo_ref):
            group()
        for group in recur(N_RG - 1):
            group()


def _mixer_a(x, gains, w_in, rgp, wax, wpm, pscale, sconv, sh, spool, layer, n_prompt_tiles):
    rows = x.shape[0]
    n_tiles = rows // ROW_TILE
    pool_blk = (2 * D_RNN) // D_POOL

    def chunk(s, lag):
        return jnp.clip(s - lag, 0, N_RG - 1)

    def x_tile(i, s):
        return jnp.minimum(i + (s >= 1).astype(jnp.int32), n_tiles - 1)

    kern = functools.partial(_mixer_a_kernel, n_prompt_tiles=n_prompt_tiles)
    return pl.pallas_call(
        kern,
        out_shape=(
            jax.ShapeDtypeStruct((rows, D_RNN), BF16),
            jax.ShapeDtypeStruct((rows, D_POOL), BF16),
            jax.ShapeDtypeStruct((rows, D_MODEL), BF16),
            jax.ShapeDtypeStruct((2, N_RG, CONV_ROWS, RG_CHUNK), F32),
            jax.ShapeDtypeStruct((2, N_RG, SUBLANES, RG_CHUNK), F32),
            jax.ShapeDtypeStruct((2, POOL_ROWS, D_POOL), F32),
        ),
        grid=(n_tiles, N_STEPS),
        in_specs=[
            pl.BlockSpec((ROW_TILE, D_MODEL), lambda i, s: (x_tile(i, s), 0)),
            pl.BlockSpec((None, 6, D_MODEL), lambda i, s: (layer, 0, 0)),
            pl.BlockSpec((D_MODEL, RG_CHUNK), lambda i, s: (0, chunk(s, 0))),
            pl.BlockSpec((D_MODEL, RG_CHUNK), lambda i, s: (0, N_RG + chunk(s, 0))),
            pl.BlockSpec((D_MODEL, D_POOL), lambda i, s: (0, pool_blk)),
            pl.BlockSpec((None, 8, RG_CHUNK), lambda i, s: (layer, 0, chunk(s, 1))),
            pl.BlockSpec((None, 8, RG_CHUNK), lambda i, s: (layer, 0, chunk(s, 2))),
            pl.BlockSpec((None, RG_HEADS, RG_BLOCK, 2 * RG_BLOCK), lambda i, s: (layer, chunk(s, 1), 0, 0)),
            pl.BlockSpec((None, len(POOL_WINDOWS), POOL_GROUP, POOL_GROUP), lambda i, s: (layer, 0, 0, 0)),
            pl.BlockSpec((None, 1, D_POOL), lambda i, s: (layer, 0, 0)),
            pl.BlockSpec((None, CONV_ROWS, D_RNN), lambda i, s: (layer, 0, 0)),
            pl.BlockSpec((None, SUBLANES, D_RNN), lambda i, s: (layer, 0, 0)),
            pl.BlockSpec((None, POOL_ROWS, D_POOL), lambda i, s: (layer, 0, 0)),
        ],
        out_specs=(
            pl.BlockSpec((ROW_TILE, RG_CHUNK), lambda i, s: (i, chunk(s, 2))),
            pl.BlockSpec((ROW_TILE, D_POOL), lambda i, s: (i, 0)),
            pl.BlockSpec((ROW_TILE, D_MODEL), lambda i, s: (x_tile(i, s), 0)),
            pl.BlockSpec((2, N_RG, CONV_ROWS, RG_CHUNK), lambda i, s: (0, 0, 0, 0)),
            pl.BlockSpec((2, N_RG, SUBLANES, RG_CHUNK), lambda i, s: (0, 0, 0, 0)),
            pl.BlockSpec((2, POOL_ROWS, D_POOL), lambda i, s: (0, 0, 0)),
        ),
        scratch_shapes=[
            pltpu.VMEM((2, ROW_TILE, D_MODEL), BF16),
            pltpu.VMEM((N_RG, CONV_ROWS, RG_CHUNK), F32),
            pltpu.VMEM((N_RG, SUBLANES, RG_CHUNK), F32),
            pltpu.VMEM((CONV_ROWS + ROW_TILE, RG_CHUNK), F32),
            pltpu.VMEM((CONV_ROWS + ROW_TILE, RG_CHUNK), F32),
            pltpu.VMEM((ROW_TILE, RG_CHUNK), F32),
            pltpu.VMEM((ROW_TILE, RG_CHUNK), F32),
            pltpu.VMEM((ROW_TILE, RG_CHUNK), BF16),
            pltpu.VMEM((RG_HEADS, ROW_TILE, 2 * RG_BLOCK), F32),
            pltpu.VMEM((RG_HEADS, ROW_TILE, 2 * RG_BLOCK), F32),
            pltpu.VMEM((ROW_TILE, RG_CHUNK), F32),
            pltpu.VMEM((ROW_TILE, RG_CHUNK), F32),
            pltpu.VMEM((ROW_TILE, RG_CHUNK), F32),
            pltpu.VMEM((3, PAIR, RG_CHUNK), F32),
            pltpu.VMEM((POOL_ROWS + ROW_TILE, D_POOL), F32),
        ],
        compiler_params=pltpu.CompilerParams(
            dimension_semantics=("arbitrary", "arbitrary"), vmem_limit_bytes=VMEM_LIMIT),
        name=f"mixer_a_l{layer}",
    )(x, gains, w_in, w_in, w_in, rgp, rgp, wax, wpm, pscale, sconv, sh, spool)


def _mixer_b_kernel(x_ref, g_ref, hb_ref, recg_ref, pooled_ref, wgr_ref, wgp_ref, wbr_ref, wbp_ref, wo_ref,
                    o_ref, *, n_chunks):
    n = pl.program_id(1)

    def chunk(first):
        hb = hb_ref[...]
        g_rg = _dot(hb, wgr_ref[...])
        g_pool = _dot(hb, wgp_ref[...])
        br = _dot(recg_ref[...], wbr_ref[...])
        bp = _dot(pooled_ref[...], wbp_ref[...])
        mix = (_sigmoid(g_rg) * br + _sigmoid(g_pool) * bp).astype(BF16)
        for c in range(D_MODEL // ACC_COLS):
            cs = slice(c * ACC_COLS, (c + 1) * ACC_COLS)
            part = _dot(mix, wo_ref[:, cs])
            if first:
                o_ref[:, cs] = part
            else:
                o_ref[:, cs] += part

    @pl.when(n == 0)
    def _():
        chunk(True)

    @pl.when(n > 0)
    def _():
        chunk(False)

    @pl.when(n == n_chunks - 1)
    def _():
        _residual_norm(x_ref, g_ref[3:4, :], o_ref)


def _mixer_b(x, gains, hb, recg, pooled, w_in, w_br_rg, w_br_pool, w_out, layer):
    rows = x.shape[0]
    n_chunks = D_MODEL // OUT_CHUNK
    g_rg_blk = (2 * D_RNN + D_POOL) // OUT_CHUNK
    g_pool_blk = g_rg_blk + n_chunks
    kern = functools.partial(_mixer_b_kernel, n_chunks=n_chunks)
    return pl.pallas_call(
        kern,
        out_shape=jax.ShapeDtypeStruct((rows, D_MODEL), F32),
        grid=(rows // ROW_TILE, n_chunks),
        in_specs=[
            pl.BlockSpec((ROW_TILE, D_MODEL), lambda i, n: (i, 0)),
            pl.BlockSpec((None, 6, D_MODEL), lambda i, n: (layer, 0, 0)),
            pl.BlockSpec((ROW_TILE, D_MODEL), lambda i, n: (i, 0)),
            pl.BlockSpec((ROW_TILE, D_RNN), lambda i, n: (i, 0)),
            pl.BlockSpec((ROW_TILE, D_POOL), lambda i, n: (i, 0)),
            pl.BlockSpec((D_MODEL, OUT_CHUNK), lambda i, n: (0, g_rg_blk + n)),
            pl.BlockSpec((D_MODEL, OUT_CHUNK), lambda i, n: (0, g_pool_blk + n)),
            pl.BlockSpec((D_RNN, OUT_CHUNK), lambda i, n: (0, n)),
            pl.BlockSpec((D_POOL, OUT_CHUNK), lambda i, n: (0, n)),
            pl.BlockSpec((OUT_CHUNK, D_MODEL), lambda i, n: (n, 0)),
        ],
        out_specs=pl.BlockSpec((ROW_TILE, D_MODEL), lambda i, n: (i, 0)),
        compiler_params=pltpu.CompilerParams(
            dimension_semantics=("arbitrary", "arbitrary"), vmem_limit_bytes=VMEM_LIMIT),
        name=f"mixer_b_l{layer}",
    )(x, gains, hb, recg, pooled, w_in, w_in, w_br_rg, w_br_pool, w_out)


def kernel(x_prompt, x_sample, state_conv, state_h, state_pool, norm_gains, w_ffn_in, w_ffn_out, w_in,
           conv_w, conv_b, w_rg_a, b_rg_a, w_rg_x, b_rg_x, lru_param, w_pool_mix, pool_scale,
           w_br_rg, w_br_pool, w_out):
    depth = norm_gains.shape[0]
    batch, seq, _ = x_prompt.shape
    dec_batch = x_sample.shape[0]
    assert batch == SUBLANES and dec_batch == SUBLANES
    n_prompt_rows = batch * seq
    assert n_prompt_rows % ROW_TILE == 0 and x_sample.shape[1] * dec_batch == ROW_TILE
    n_prompt_tiles = n_prompt_rows // ROW_TILE

    x = (x_prompt, x_sample)

    ffn_in_b = w_ffn_in[0, 0].astype(BF16)
    ffn_out_b = w_ffn_out[0, 0].astype(BF16)
    wpm_b = w_pool_mix.astype(BF16)
    wax_b = jnp.concatenate([w_rg_a, w_rg_x], axis=-1).astype(BF16)
    rgp = jnp.concatenate(
        [conv_w, conv_b[:, None], b_rg_a[:, None], b_rg_x[:, None], lru_param[:, None]], axis=1)
    pscale = pool_scale[:, None, :]
    sconv = jnp.swapaxes(state_conv, 1, 2).reshape(depth, CONV_ROWS, D_RNN)
    spool = jnp.swapaxes(state_pool, 1, 2).reshape(depth, POOL_ROWS, D_POOL)

    convs, hs, pools = [], [], []
    for l in range(depth):
        x, (w_in_b, w_br_rg_b, w_br_pool_b, w_out_b, ffn_in_b, ffn_out_b) = _ffn(
            x, norm_gains, ffn_in_b, ffn_out_b, l, 0,
            casts=[(w_in, (l,)), (w_br_rg, (l,)), (w_br_pool, (l,)), (w_out, (l,)),
                   (w_ffn_in, (l, 1)), (w_ffn_out, (l, 1))])
        recg, pooled, hb, conv_o, h_o, pool_o = _mixer_a(
            x, norm_gains, w_in_b, rgp, wax_b, wpm_b, pscale, sconv, state_h, spool, l, n_prompt_tiles)
        x = _mixer_b(x, norm_gains, hb, recg, pooled, w_in_b, w_br_rg_b, w_br_pool_b, w_out_b, l)
        next_ffn = [(w_ffn_in, (l + 1, 0)), (w_ffn_out, (l + 1, 0))] if l + 1 < depth else []
        x, next_b = _ffn(x, norm_gains, ffn_in_b, ffn_out_b, l, 1, casts=next_ffn,
                         final_seq=seq if l + 1 == depth else None)
        if next_b:
            ffn_in_b, ffn_out_b = next_b
        convs.append(jnp.swapaxes(conv_o, 1, 2).reshape(2, CONV_ROWS, D_RNN))
        hs.append(jnp.swapaxes(h_o, 1, 2).reshape(2, SUBLANES, D_RNN))
        pools.append(pool_o)

    y_prompt, y_sample = x
    conv_all = jnp.stack(convs)
    h_all = jnp.stack(hs)
    pool_all = jnp.stack(pools)

    def unroll_state(s, which, frames):
        s = s[:, which]
        return jnp.swapaxes(s.reshape(depth, frames, SUBLANES, s.shape[-1]), 1, 2)

    return (
        y_prompt,
        y_sample,
        unroll_state(conv_all, 0, CONV_W - 1),
        h_all[:, 0],
        unroll_state(pool_all, 0, POOL_MAX - 1),
        unroll_state(conv_all, 1, CONV_W - 1),
        h_all[:, 1],
        unroll_state(pool_all, 1, POOL_MAX - 1),
    )
```

```python
import functools
import math

import jax
import jax.numpy as jnp
from jax import lax
from jax.experimental import pallas as pl
from jax.experimental.pallas import tpu as pltpu

F32 = jnp.float32
BF16 = jnp.bfloat16

D_MODEL = 2048
D_FF = 3 * D_MODEL
D_RNN = D_MODEL
D_POOL = D_MODEL // 2
N_RG_HEADS = 16
RG_BLOCK = D_RNN // N_RG_HEADS
CONV_W = 4
LRU_C = 8.0
POOL_WINDOWS = (2, 4, 8, 16)
POOL_GROUP = D_POOL // len(POOL_WINDOWS)
POOL_MAX = 16
EPS = 1e-6
LOG2_E = 1.4426950408889634

SUBLANES = 8
MXU_COLS = 256
ROW_TILE = 512
FFN_CHUNK = 1024
RG_CHUNK = 512
OUT_CHUNK = 512
NORM_ROWS = 32
RESIDUAL_ROWS = 8
ACC_COLS = 512
VMEM_LIMIT = 56 * 1024 * 1024
FIRST_FFN_VMEM_LIMIT = 62 * 1024 * 1024

CONV_ROWS = (CONV_W - 1) * SUBLANES
POOL_ROWS = (POOL_MAX - 1) * SUBLANES
N_RG = D_RNN // RG_CHUNK
RG_HEADS = RG_CHUNK // RG_BLOCK
PAIR = 2 * SUBLANES
FRAMES = ROW_TILE // SUBLANES
N_GROUPS = 4
RECUR_UNROLL = 4


def _rms(x, g):
    ms = jnp.mean(x * x, axis=-1, keepdims=True)
    return x * lax.rsqrt(ms + EPS) * g


def _norm_rows(x_ref, g, dst_refs, lo, hi):
    for r in range(lo, hi, NORM_ROWS):
        rows = slice(r, r + NORM_ROWS)
        y = _rms(x_ref[rows, :], g).astype(BF16)
        for dst_ref in dst_refs:
            dst_ref[rows, :] = y


def _norm_to_bf16(x_ref, g, *dst_refs):
    _norm_rows(x_ref, g, dst_refs, 0, x_ref.shape[0])


def _residual_norm(x_ref, g, o_ref):
    for r in range(0, x_ref.shape[0], RESIDUAL_ROWS):
        rows = slice(r, r + RESIDUAL_ROWS)
        o_ref[rows, :] = x_ref[rows, :] + _rms(o_ref[rows, :], g)


def _sigmoid(x):
    return 1.0 / (1.0 + jnp.exp(-x))


def _gelu_tanh(x):
    c = math.sqrt(2.0 / math.pi)
    half = 0.5 * x
    return half + half * jnp.tanh(x * (c + (c * 0.044715) * (x * x)))


def _sqrt_nonneg(z):
    return jnp.where(z > 0.0, z * lax.rsqrt(z), 0.0)


def _dot(a, b):
    return jnp.dot(a, b, preferred_element_type=F32)


def _interleave(matmul_pieces, vector_groups):
    for k in range(max(len(matmul_pieces), len(vector_groups))):
        if k < len(matmul_pieces):
            matmul_pieces[k]()
        if k < len(vector_groups):
            vector_groups[k]()


def _ffn_kernel(*refs, pre, post, n_chunks, n_casts, n_in_tiles, n_prompt_tiles):
    if n_in_tiles is None:
        x_ref, g_ref, wg_ref, wu_ref, wo_ref = refs[:5]
        rest = refs[5:]
    else:
        xp_ref, xs_ref, g_ref, wg_ref, wu_ref, wo_ref = refs[:6]
        rest = refs[6:-1]
        x_ref = refs[-1]
    cast_src = rest[:n_casts]
    if n_prompt_tiles is None:
        o_ref = rest[n_casts]
        cast_dst = rest[n_casts + 1:2 * n_casts + 1]
        xn_ref = rest[2 * n_casts + 1]
    else:
        yp_ref, ys_ref = rest[n_casts:n_casts + 2]
        cast_dst = rest[n_casts + 2:2 * n_casts + 2]
        xn_ref, o_ref = rest[2 * n_casts + 2:2 * n_casts + 4]
    f = pl.program_id(1)

    for src, dst in zip(cast_src, cast_dst):
        dst[...] = src[...].astype(BF16)

    def chunk(first):
        xn = xn_ref[...]
        gate = _dot(xn, wg_ref[...])
        up = _dot(xn, wu_ref[...])
        hid = (gate * _sigmoid(gate) * up).astype(BF16)
        for n in range(D_MODEL // ACC_COLS):
            cs = slice(n * ACC_COLS, (n + 1) * ACC_COLS)
            part = _dot(hid, wo_ref[:, cs])
            if first:
                o_ref[:, cs] = part
            else:
                o_ref[:, cs] += part

    i = pl.program_id(0)

    if n_in_tiles is not None:
        def gather(src_ref):
            for t in range(FRAMES):
                x_ref[t * SUBLANES:(t + 1) * SUBLANES, :] = src_ref[:, t, :]

        @pl.when((f == 0) & (i < n_in_tiles))
        def _():
            gather(xp_ref)

        @pl.when((f == 0) & (i >= n_in_tiles))
        def _():
            gather(xs_ref)

    @pl.when(f == 0)
    def _():
        _norm_to_bf16(x_ref, g_ref[pre:pre + 1, :], xn_ref)
        chunk(True)

    @pl.when(f > 0)
    def _():
        chunk(False)

    if n_prompt_tiles is None:
        @pl.when(f == n_chunks - 1)
        def _():
            _residual_norm(x_ref, 0.5 * g_ref[post:post + 1, :], o_ref)
    else:

        def emit(dst_ref):
            g = 0.5 * g_ref[post:post + 1, :]
            for t in range(FRAMES):
                rows = slice(t * SUBLANES, (t + 1) * SUBLANES)
                dst_ref[:, t, :] = x_ref[rows, :] + _rms(o_ref[rows, :], g)

        @pl.when((f == n_chunks - 1) & (i < n_prompt_tiles))
        def _():
            emit(yp_ref)

        @pl.when((f == n_chunks - 1) & (i >= n_prompt_tiles))
        def _():
            emit(ys_ref)


BF16_TILE_ROWS = 16


def _ffn(x, gains, w_in, w_out, layer, which, casts=(), final_seq=None):
    from_streams = isinstance(x, tuple)
    if from_streams:
        x_prompt, x_sample = x
        n_in_tiles = x_prompt.shape[1] // FRAMES
        rows = (n_in_tiles + 1) * ROW_TILE
        x_args = [x_prompt, x_sample]
        x_specs = [
            pl.BlockSpec((SUBLANES, FRAMES, D_MODEL), lambda i, f: (0, jnp.minimum(i, n_in_tiles - 1), 0)),
            pl.BlockSpec((SUBLANES, FRAMES, D_MODEL), lambda i, f: (0, 0, 0)),
        ]
        x_scratch = [pltpu.VMEM((ROW_TILE, D_MODEL), F32)]
        vmem_limit = FIRST_FFN_VMEM_LIMIT
    else:
        n_in_tiles = None
        rows = x.shape[0]
        x_args = [x]
        x_specs = [pl.BlockSpec((ROW_TILE, D_MODEL), lambda i, f: (i, 0))]
        x_scratch = []
        vmem_limit = VMEM_LIMIT
    n_chunks = D_FF // FFN_CHUNK
    n_tiles = rows // ROW_TILE
    n_steps = n_tiles * n_chunks

    cast_in_specs, cast_out_specs, cast_out_shapes = [], [], []
    for src, lead in casts:
        m_rows, m_cols = src.shape[-2:]
        per_step = -(-m_rows // n_steps)
        blk_rows = -(-per_step // BF16_TILE_ROWS) * BF16_TILE_ROWS
        assert m_rows % blk_rows == 0
        n_blk = m_rows // blk_rows

        def blk(i, f, n_blk=n_blk):
            return jnp.minimum(i * n_chunks + f, n_blk - 1)

        cast_in_specs.append(pl.BlockSpec(
            (None,) * len(lead) + (blk_rows, m_cols), lambda i, f, lead=lead, blk=blk: lead + (blk(i, f), 0)))
        cast_out_specs.append(pl.BlockSpec((blk_rows, m_cols), lambda i, f, blk=blk: (blk(i, f), 0)))
        cast_out_shapes.append(jax.ShapeDtypeStruct((m_rows, m_cols), BF16))

    if final_seq is None:
        n_prompt_tiles = None
        main_shapes = [jax.ShapeDtypeStruct((rows, D_MODEL), F32)]
        main_specs = [pl.BlockSpec((ROW_TILE, D_MODEL), lambda i, f: (i, 0))]
        scratch = [pltpu.VMEM((ROW_TILE, D_MODEL), BF16)]
    else:
        n_prompt_tiles = n_tiles - 1
        main_shapes = [jax.ShapeDtypeStruct((SUBLANES, final_seq, D_MODEL), F32),
                       jax.ShapeDtypeStruct((SUBLANES, FRAMES, D_MODEL), F32)]
        main_specs = [
            pl.BlockSpec((SUBLANES, FRAMES, D_MODEL), lambda i, f: (0, jnp.minimum(i, n_prompt_tiles - 1), 0)),
            pl.BlockSpec((SUBLANES, FRAMES, D_MODEL), lambda i, f: (0, 0, 0)),
        ]
        scratch = [pltpu.VMEM((ROW_TILE, D_MODEL), BF16), pltpu.VMEM((ROW_TILE, D_MODEL), F32)]

    kern = functools.partial(_ffn_kernel, pre=4 * which, post=4 * which + 1, n_chunks=n_chunks,
                             n_casts=len(casts), n_in_tiles=n_in_tiles, n_prompt_tiles=n_prompt_tiles)
    outs = pl.pallas_call(
        kern,
        out_shape=main_shapes + cast_out_shapes,
        grid=(n_tiles, n_chunks),
        in_specs=x_specs + [
            pl.BlockSpec((None, 6, D_MODEL), lambda i, f: (layer, 0, 0)),
            pl.BlockSpec((D_MODEL, FFN_CHUNK), lambda i, f: (0, f)),
            pl.BlockSpec((D_MODEL, FFN_CHUNK), lambda i, f: (0, n_chunks + f)),
            pl.BlockSpec((FFN_CHUNK, D_MODEL), lambda i, f: (f, 0)),
        ] + cast_in_specs,
        out_specs=main_specs + cast_out_specs,
        scratch_shapes=scratch + x_scratch,
        compiler_params=pltpu.CompilerParams(
            dimension_semantics=("arbitrary", "arbitrary"), vmem_limit_bytes=vmem_limit),
        name=f"ffn_l{layer}_{which}",
    )(*x_args, gains, w_in, w_in, w_out, *[src for src, _ in casts])
    n_main = len(main_shapes)
    main = outs[0] if n_main == 1 else tuple(outs[:n_main])
    return main, list(outs[n_main:])


N_STEPS = N_RG + 2


def _project_pieces(hb_ref, par, wxr_ref, wy_ref, zc_ref, y_ref, j):
    def run():
        hb = hb_ref[par]
        zc_ref[j % 2][CONV_ROWS:, :] = _dot(hb, wxr_ref[...])
        y_ref[j % 3][...] = _dot(hb, wy_ref[...])
    return [run]


def _pool_project_pieces(hb_ref, par, wp_ref, zp_ref):
    def run():
        zp_ref[POOL_ROWS:, :] = _dot(hb_ref[par], wp_ref[...])
    return [run]


def _conv(j, seq, rgp_ref, zc_ref, xc_ref, xcb_ref, cpre_ref, convo_ref):
    tm = ROW_TILE
    zc = zc_ref[j % 2]
    zc[0:CONV_ROWS, :] = cpre_ref[j]
    half_xc = 0.5 * rgp_ref[4:5, :]
    for k in range(CONV_W):
        half_xc = half_xc + zc[k * SUBLANES:k * SUBLANES + tm, :] * (0.5 * rgp_ref[k:k + 1, :])
    new_pre = zc[tm:tm + CONV_ROWS, :]
    cpre_ref[j] = new_pre
    convo_ref[seq, j] = new_pre
    xc_ref[j % 2][...] = half_xc
    xcb_ref[...] = half_xc.astype(BF16)


def _gate_project(j, wax_ref, xcb_ref, ri_ref):
    for hh in range(RG_HEADS):
        sl = slice(hh * RG_BLOCK, (hh + 1) * RG_BLOCK)
        ri_ref[j % 2][hh] = _dot(xcb_ref[:, sl], wax_ref[hh])


def _recur_groups(j, seq, rgp_ref, ri_ref, xc_ref, y_ref, hst_ref, cst_ref, recg_ref, ho_ref):
    tm = ROW_TILE
    carry = {}

    def setup():
        neg_lam = -rgp_ref[7:8, :]
        softplus = jnp.maximum(neg_lam, 0.0) + jnp.log1p(jnp.exp(-jnp.abs(neg_lam)))
        cst_ref[0] = jnp.broadcast_to(0.5 * rgp_ref[5:6, :], (PAIR, RG_CHUNK))
        cst_ref[1] = jnp.broadcast_to(0.5 * rgp_ref[6:7, :], (PAIR, RG_CHUNK))
        cst_ref[2] = jnp.broadcast_to((0.5 * LRU_C) * softplus, (PAIR, RG_CHUNK))
        carry["h"] = hst_ref[j]

    def pair(r0, h):
        rows = pl.ds(r0, PAIR)
        a_parts, u_parts = [], []
        for hh in range(RG_HEADS):
            sl = slice(hh * RG_BLOCK, (hh + 1) * RG_BLOCK)
            ri = ri_ref[j % 2][hh, rows, :]
            t_r = jnp.tanh(ri[:, :RG_BLOCK] + cst_ref[0, :, sl])
            t_i = jnp.tanh(ri[:, RG_BLOCK:] + cst_ref[1, :, sl])
            ch = cst_ref[2, :, sl]
            neg_log_a = ch * t_r + ch
            a = jnp.exp2(neg_log_a * (-LOG2_E))
            half_xc = xc_ref[j % 2][rows, sl]
            gated = half_xc * t_i + half_xc
            u_parts.append(_sqrt_nonneg(jnp.tanh(neg_log_a) * (a * a + 1.0)) * gated)
            a_parts.append(a)
        a = jnp.concatenate(a_parts, axis=1)
        u = jnp.concatenate(u_parts, axis=1)
        h1 = a[:SUBLANES] * h + u[:SUBLANES]
        h2 = a[SUBLANES:] * h1 + u[SUBLANES:]
        rec = jnp.concatenate([h1, h2], axis=0)
        recg_ref[rows, :] = (rec * _gelu_tanh(y_ref[j % 3][rows, :])).astype(BF16)
        return h2

    def group(k):
        def run():
            if k == 0:
                setup()
            span = tm // N_GROUPS

            def body(p, h):
                return pair(pl.multiple_of(k * span + p * PAIR, PAIR), h)

            carry["h"] = lax.fori_loop(0, span // PAIR, body, carry["h"], unroll=RECUR_UNROLL)
            if k == N_GROUPS - 1:
                hst_ref[j] = carry["h"]
                ho_ref[seq, j] = carry["h"]
        return run

    return [group(k) for k in range(N_GROUPS)]


def _pool_groups(i, seq, n_prompt_tiles, wpm_ref, ps_ref, zp_ref, pooled_ref, poolo_ref):
    tm = ROW_TILE
    frames = tm // SUBLANES

    def group(g):
        def run():
            w = POOL_WINDOWS[g]
            frame = lax.shift_right_logical(lax.broadcasted_iota(jnp.int32, (tm, 1), 0), 3)
            seen = jnp.where(i >= n_prompt_tiles, POOL_MAX - 1, i * frames) + 1
            seen = (frame + seen).astype(F32)
            cs = slice(g * POOL_GROUP, (g + 1) * POOL_GROUP)
            s = zp_ref[POOL_ROWS - (w - 1) * SUBLANES:, cs]
            shift = SUBLANES
            while shift < w * SUBLANES:
                s = s[shift:, :] + s[:s.shape[0] - shift, :]
                shift *= 2
            mean = s * (1.0 / jnp.minimum(seen, float(w)))
            pooled = (mean - zp_ref[POOL_ROWS:, cs]).astype(BF16)
            mixed = _dot(pooled, wpm_ref[g]) * ps_ref[:, cs]
            pooled_ref[:, cs] = mixed.astype(BF16)
            if g == len(POOL_WINDOWS) - 1:
                new_pre = zp_ref[tm:tm + POOL_ROWS, :]
                zp_ref[0:POOL_ROWS, :] = new_pre
                poolo_ref[seq] = new_pre
        return run

    return [group(g) for g in range(len(POOL_WINDOWS))]


def _mixer_a_kernel(x_ref, g_ref, wxr_ref, wy_ref, wp_ref, rgp_conv_ref, rgp_recur_ref, wax_ref, wpm_ref,
                    ps_ref, sconv_ref, sh_ref, spool_ref,
                    recg_ref, pooled_ref, hbn_ref, convo_ref, ho_ref, poolo_ref,
                    hb_ref, cpre_ref, hst_ref, zc0_ref, zc1_ref, xc0_ref, xc1_ref, xcb_ref, ri0_ref, ri1_ref,
                    y0_ref, y1_ref, y2_ref, cst_ref, zp_ref,
                    *, n_prompt_tiles):
    zc_ref = (zc0_ref, zc1_ref)
    xc_ref = (xc0_ref, xc1_ref)
    ri_ref = (ri0_ref, ri1_ref)
    y_ref = (y0_ref, y1_ref, y2_ref)
    i = pl.program_id(0)
    s = pl.program_id(1)
    seq = (i >= n_prompt_tiles).astype(jnp.int32)
    par = i % 2

    def project(j):
        return _project_pieces(hb_ref, par, wxr_ref, wy_ref, zc_ref, y_ref, j)

    def conv(j):
        return lambda: _conv(j, seq, rgp_conv_ref, zc_ref, xc_ref, xcb_ref, cpre_ref, convo_ref)

    def gate_project(j):
        return lambda: _gate_project(j, wax_ref, xcb_ref, ri_ref)

    def recur(j):
        return _recur_groups(j, seq, rgp_recur_ref, ri_ref, xc_ref, y_ref, hst_ref, cst_ref, recg_ref, ho_ref)

    def norm_next():
        span = ROW_TILE // N_GROUPS
        return [functools.partial(_norm_rows, x_ref, g_ref[2:3, :], (hb_ref.at[1 - par], hbn_ref),
                                  k * span, (k + 1) * span)
                for k in range(N_GROUPS)]

    @pl.when(s == 0)
    def _():
        @pl.when(i == 0)
        def _():
            cpre_ref[...] = jnp.zeros(cpre_ref.shape, F32)
            hst_ref[...] = jnp.zeros(hst_ref.shape, F32)
            zp_ref[0:POOL_ROWS, :] = jnp.zeros((POOL_ROWS, D_POOL), F32)
            _norm_to_bf16(x_ref, g_ref[2:3, :], hb_ref.at[0], hbn_ref)

        @pl.when(i == n_prompt_tiles)
        def _():
            for j in range(N_RG):
                cs = slice(j * RG_CHUNK, (j + 1) * RG_CHUNK)
                cpre_ref[j] = sconv_ref[:, cs]
                hst_ref[j] = sh_ref[:, cs]
            zp_ref[0:POOL_ROWS, :] = spool_ref[...]

        for piece in project(0):
            piece()

    def stages(pieces, conv_stage, groups, gate_stage):
        for piece in pieces:
            piece()
        conv_stage()
        gate_stage()
        for group in groups:
            group()

    @pl.when(s == 1)
    def _():
        stages(project(1), conv(0), norm_next(), gate_project(0))

    for step in range(2, N_RG):
        @pl.when(s == step)
        def _(step=step):
            stages(project(step), conv(step - 1), recur(step - 2), gate_project(step - 1))

    @pl.when(s == N_RG)
    def _():
        stages(_pool_project_pieces(hb_ref, par, wp_ref, zp_ref), conv(N_RG - 1), recur(N_RG - 2),
               gate_project(N_RG - 1))

    @pl.when(s == N_RG + 1)
    def _():
        for group in _pool_groups(i, seq, n_prompt_tiles, wpm_ref, ps_ref, zp_ref, pooled_ref, poolo_ref):
            group()
        for group in recur(N_RG - 1):
            group()


def _mixer_a(x, gains, w_in, rgp, wax, wpm, pscale, sconv, sh, spool, layer, n_prompt_tiles):
    rows = x.shape[0]
    n_tiles = rows // ROW_TILE
    pool_blk = (2 * D_RNN) // D_POOL

    def chunk(s, lag):
        return jnp.clip(s - lag, 0, N_RG - 1)

    def x_tile(i, s):
        return jnp.minimum(i + (s >= 1).astype(jnp.int32), n_tiles - 1)

    kern = functools.partial(_mixer_a_kernel, n_prompt_tiles=n_prompt_tiles)
    return pl.pallas_call(
        kern,
        out_shape=(
            jax.ShapeDtypeStruct((rows, D_RNN), BF16),
            jax.ShapeDtypeStruct((rows, D_POOL), BF16),
            jax.ShapeDtypeStruct((rows, D_MODEL), BF16),
            jax.ShapeDtypeStruct((2, N_RG, CONV_ROWS, RG_CHUNK), F32),
            jax.ShapeDtypeStruct((2, N_RG, SUBLANES, RG_CHUNK), F32),
            jax.ShapeDtypeStruct((2, POOL_ROWS, D_POOL), F32),
        ),
        grid=(n_tiles, N_STEPS),
        in_specs=[
            pl.BlockSpec((ROW_TILE, D_MODEL), lambda i, s: (x_tile(i, s), 0)),
            pl.BlockSpec((None, 6, D_MODEL), lambda i, s: (layer, 0, 0)),
            pl.BlockSpec((D_MODEL, RG_CHUNK), lambda i, s: (0, chunk(s, 0))),
            pl.BlockSpec((D_MODEL, RG_CHUNK), lambda i, s: (0, N_RG + chunk(s, 0))),
            pl.BlockSpec((D_MODEL, D_POOL), lambda i, s: (0, pool_blk)),
            pl.BlockSpec((None, 8, RG_CHUNK), lambda i, s: (layer, 0, chunk(s, 1))),
            pl.BlockSpec((None, 8, RG_CHUNK), lambda i, s: (layer, 0, chunk(s, 2))),
            pl.BlockSpec((None, RG_HEADS, RG_BLOCK, 2 * RG_BLOCK), lambda i, s: (layer, chunk(s, 1), 0, 0)),
            pl.BlockSpec((None, len(POOL_WINDOWS), POOL_GROUP, POOL_GROUP), lambda i, s: (layer, 0, 0, 0)),
            pl.BlockSpec((None, 1, D_POOL), lambda i, s: (layer, 0, 0)),
            pl.BlockSpec((None, CONV_ROWS, D_RNN), lambda i, s: (layer, 0, 0)),
            pl.BlockSpec((None, SUBLANES, D_RNN), lambda i, s: (layer, 0, 0)),
            pl.BlockSpec((None, POOL_ROWS, D_POOL), lambda i, s: (layer, 0, 0)),
        ],
        out_specs=(
            pl.BlockSpec((ROW_TILE, RG_CHUNK), lambda i, s: (i, chunk(s, 2))),
            pl.BlockSpec((ROW_TILE, D_POOL), lambda i, s: (i, 0)),
            pl.BlockSpec((ROW_TILE, D_MODEL), lambda i, s: (x_tile(i, s), 0)),
            pl.BlockSpec((2, N_RG, CONV_ROWS, RG_CHUNK), lambda i, s: (0, 0, 0, 0)),
            pl.BlockSpec((2, N_RG, SUBLANES, RG_CHUNK), lambda i, s: (0, 0, 0, 0)),
            pl.BlockSpec((2, POOL_ROWS, D_POOL), lambda i, s: (0, 0, 0)),
        ),
        scratch_shapes=[
            pltpu.VMEM((2, ROW_TILE, D_MODEL), BF16),
            pltpu.VMEM((N_RG, CONV_ROWS, RG_CHUNK), F32),
            pltpu.VMEM((N_RG, SUBLANES, RG_CHUNK), F32),
            pltpu.VMEM((CONV_ROWS + ROW_TILE, RG_CHUNK), F32),
            pltpu.VMEM((CONV_ROWS + ROW_TILE, RG_CHUNK), F32),
            pltpu.VMEM((ROW_TILE, RG_CHUNK), F32),
            pltpu.VMEM((ROW_TILE, RG_CHUNK), F32),
            pltpu.VMEM((ROW_TILE, RG_CHUNK), BF16),
            pltpu.VMEM((RG_HEADS, ROW_TILE, 2 * RG_BLOCK), F32),
            pltpu.VMEM((RG_HEADS, ROW_TILE, 2 * RG_BLOCK), F32),
            pltpu.VMEM((ROW_TILE, RG_CHUNK), F32),
            pltpu.VMEM((ROW_TILE, RG_CHUNK), F32),
            pltpu.VMEM((ROW_TILE, RG_CHUNK), F32),
            pltpu.VMEM((3, PAIR, RG_CHUNK), F32),
            pltpu.VMEM((POOL_ROWS + ROW_TILE, D_POOL), F32),
        ],
        compiler_params=pltpu.CompilerParams(
            dimension_semantics=("arbitrary", "arbitrary"), vmem_limit_bytes=VMEM_LIMIT),
        name=f"mixer_a_l{layer}",
    )(x, gains, w_in, w_in, w_in, rgp, rgp, wax, wpm, pscale, sconv, sh, spool)


def _mixer_b_kernel(x_ref, g_ref, hb_ref, recg_ref, pooled_ref, wgr_ref, wgp_ref, wbr_ref, wbp_ref, wo_ref,
                    o_ref, *, n_chunks):
    n = pl.program_id(1)

    def chunk(first):
        hb = hb_ref[...]
        g_rg = _dot(hb, wgr_ref[...])
        g_pool = _dot(hb, wgp_ref[...])
        br = _dot(recg_ref[...], wbr_ref[...])
        bp = _dot(pooled_ref[...], wbp_ref[...])
        mix = (_sigmoid(g_rg) * br + _sigmoid(g_pool) * bp).astype(BF16)
        for c in range(D_MODEL // ACC_COLS):
            cs = slice(c * ACC_COLS, (c + 1) * ACC_COLS)
            part = _dot(mix, wo_ref[:, cs])
            if first:
                o_ref[:, cs] = part
            else:
                o_ref[:, cs] += part

    @pl.when(n == 0)
    def _():
        chunk(True)

    @pl.when(n > 0)
    def _():
        chunk(False)

    @pl.when(n == n_chunks - 1)
    def _():
        _residual_norm(x_ref, g_ref[3:4, :], o_ref)


def _mixer_b(x, gains, hb, recg, pooled, w_in, w_br_rg, w_br_pool, w_out, layer):
    rows = x.shape[0]
    n_chunks = D_MODEL // OUT_CHUNK
    g_rg_blk = (2 * D_RNN + D_POOL) // OUT_CHUNK
    g_pool_blk = g_rg_blk + n_chunks
    kern = functools.partial(_mixer_b_kernel, n_chunks=n_chunks)

    def x_tile(i, n):
        return jnp.where(n == 0, jnp.maximum(i - 1, 0), i)

    return pl.pallas_call(
        kern,
        out_shape=jax.ShapeDtypeStruct((rows, D_MODEL), F32),
        grid=(rows // ROW_TILE, n_chunks),
        in_specs=[
            pl.BlockSpec((ROW_TILE, D_MODEL), lambda i, n: (x_tile(i, n), 0)),
            pl.BlockSpec((None, 6, D_MODEL), lambda i, n: (layer, 0, 0)),
            pl.BlockSpec((ROW_TILE, D_MODEL), lambda i, n: (i, 0)),
            pl.BlockSpec((ROW_TILE, D_RNN), lambda i, n: (i, 0)),
            pl.BlockSpec((ROW_TILE, D_POOL), lambda i, n: (i, 0)),
            pl.BlockSpec((D_MODEL, OUT_CHUNK), lambda i, n: (0, g_rg_blk + n)),
            pl.BlockSpec((D_MODEL, OUT_CHUNK), lambda i, n: (0, g_pool_blk + n)),
            pl.BlockSpec((D_RNN, OUT_CHUNK), lambda i, n: (0, n)),
            pl.BlockSpec((D_POOL, OUT_CHUNK), lambda i, n: (0, n)),
            pl.BlockSpec((OUT_CHUNK, D_MODEL), lambda i, n: (n, 0)),
        ],
        out_specs=pl.BlockSpec((ROW_TILE, D_MODEL), lambda i, n: (i, 0)),
        compiler_params=pltpu.CompilerParams(
            dimension_semantics=("arbitrary", "arbitrary"), vmem_limit_bytes=VMEM_LIMIT),
        name=f"mixer_b_l{layer}",
    )(x, gains, hb, recg, pooled, w_in, w_in, w_br_rg, w_br_pool, w_out)


def kernel(x_prompt, x_sample, state_conv, state_h, state_pool, norm_gains, w_ffn_in, w_ffn_out, w_in,
           conv_w, conv_b, w_rg_a, b_rg_a, w_rg_x, b_rg_x, lru_param, w_pool_mix, pool_scale,
           w_br_rg, w_br_pool, w_out):
    depth = norm_gains.shape[0]
    batch, seq, _ = x_prompt.shape
    dec_batch = x_sample.shape[0]
    assert batch == SUBLANES and dec_batch == SUBLANES
    n_prompt_rows = batch * seq
    assert n_prompt_rows % ROW_TILE == 0 and x_sample.shape[1] * dec_batch == ROW_TILE
    n_prompt_tiles = n_prompt_rows // ROW_TILE

    x = (x_prompt, x_sample)

    ffn_in_b = w_ffn_in[0, 0].astype(BF16)
    ffn_out_b = w_ffn_out[0, 0].astype(BF16)
    wpm_b = w_pool_mix.astype(BF16)
    wax_b = jnp.concatenate([w_rg_a, w_rg_x], axis=-1).astype(BF16)
    rgp = jnp.concatenate(
        [conv_w, conv_b[:, None], b_rg_a[:, None], b_rg_x[:, None], lru_param[:, None]], axis=1)
    pscale = pool_scale[:, None, :]
    sconv = jnp.swapaxes(state_conv, 1, 2).reshape(depth, CONV_ROWS, D_RNN)
    spool = jnp.swapaxes(state_pool, 1, 2).reshape(depth, POOL_ROWS, D_POOL)

    convs, hs, pools = [], [], []
    for l in range(depth):
        x, (w_in_b, w_br_rg_b, w_br_pool_b, w_out_b, ffn_in_b, ffn_out_b) = _ffn(
            x, norm_gains, ffn_in_b, ffn_out_b, l, 0,
            casts=[(w_in, (l,)), (w_br_rg, (l,)), (w_br_pool, (l,)), (w_out, (l,)),
                   (w_ffn_in, (l, 1)), (w_ffn_out, (l, 1))])
        recg, pooled, hb, conv_o, h_o, pool_o = _mixer_a(
            x, norm_gains, w_in_b, rgp, wax_b, wpm_b, pscale, sconv, state_h, spool, l, n_prompt_tiles)
        x = _mixer_b(x, norm_gains, hb, recg, pooled, w_in_b, w_br_rg_b, w_br_pool_b, w_out_b, l)
        next_ffn = [(w_ffn_in, (l + 1, 0)), (w_ffn_out, (l + 1, 0))] if l + 1 < depth else []
        x, next_b = _ffn(x, norm_gains, ffn_in_b, ffn_out_b, l, 1, casts=next_ffn,
                         final_seq=seq if l + 1 == depth else None)
        if next_b:
            ffn_in_b, ffn_out_b = next_b
        convs.append(jnp.swapaxes(conv_o, 1, 2).reshape(2, CONV_ROWS, D_RNN))
        hs.append(jnp.swapaxes(h_o, 1, 2).reshape(2, SUBLANES, D_RNN))
        pools.append(pool_o)

    y_prompt, y_sample = x
    conv_all = jnp.stack(convs)
    h_all = jnp.stack(hs)
    pool_all = jnp.stack(pools)

    def unroll_state(s, which, frames):
        s = s[:, which]
        return jnp.swapaxes(s.reshape(depth, frames, SUBLANES, s.shape[-1]), 1, 2)

    return (
        y_prompt,
        y_sample,
        unroll_state(conv_all, 0, CONV_W - 1),
        h_all[:, 0],
        unroll_state(pool_all, 0, POOL_MAX - 1),
        unroll_state(conv_all, 1, CONV_W - 1),
        h_all[:, 1],
        unroll_state(pool_all, 1, POOL_MAX - 1),
    )
```

```python
import functools
import math

import jax
import jax.numpy as jnp
from jax import lax
from jax.experimental import pallas as pl
from jax.experimental.pallas import tpu as pltpu

F32 = jnp.float32
BF16 = jnp.bfloat16

D_MODEL = 2048
D_FF = 3 * D_MODEL
D_RNN = D_MODEL
D_POOL = D_MODEL // 2
N_RG_HEADS = 16
RG_BLOCK = D_RNN // N_RG_HEADS
CONV_W = 4
LRU_C = 8.0
POOL_WINDOWS = (2, 4, 8, 16)
POOL_GROUP = D_POOL // len(POOL_WINDOWS)
POOL_MAX = 16
EPS = 1e-6
LOG2_E = 1.4426950408889634

SUBLANES = 8
ROW_TILE = 512
FFN_CHUNK = 1024
RG_CHUNK = 512
OUT_CHUNK = 512
NORM_ROWS = 32
RESIDUAL_ROWS = 8
ACC_COLS = 512
VMEM_LIMIT = 56 * 1024 * 1024
FIRST_FFN_VMEM_LIMIT = 62 * 1024 * 1024

CONV_ROWS = (CONV_W - 1) * SUBLANES
POOL_ROWS = (POOL_MAX - 1) * SUBLANES
N_RG = D_RNN // RG_CHUNK
RG_HEADS = RG_CHUNK // RG_BLOCK
PAIR = 2 * SUBLANES
FRAMES = ROW_TILE // SUBLANES
N_GROUPS = 4
RECUR_UNROLL = 4


def _rms(x, g):
    ms = jnp.mean(x * x, axis=-1, keepdims=True)
    return x * lax.rsqrt(ms + EPS) * g


def _norm_rows(x_ref, g, dst_refs, lo, hi):
    for r in range(lo, hi, NORM_ROWS):
        rows = slice(r, r + NORM_ROWS)
        y = _rms(x_ref[rows, :], g).astype(BF16)
        for dst_ref in dst_refs:
            dst_ref[rows, :] = y


def _norm_to_bf16(x_ref, g, *dst_refs):
    _norm_rows(x_ref, g, dst_refs, 0, x_ref.shape[0])


def _residual_norm(x_ref, g, o_ref):
    for r in range(0, x_ref.shape[0], RESIDUAL_ROWS):
        rows = slice(r, r + RESIDUAL_ROWS)
        o_ref[rows, :] = x_ref[rows, :] + _rms(o_ref[rows, :], g)


def _sigmoid(x):
    return 1.0 / (1.0 + jnp.exp(-x))


def _gelu_tanh(x):
    c = math.sqrt(2.0 / math.pi)
    half = 0.5 * x
    return half + half * jnp.tanh(x * (c + (c * 0.044715) * (x * x)))


def _sqrt_nonneg(z):
    return jnp.where(z > 0.0, z * lax.rsqrt(z), 0.0)


def _dot(a, b):
    return jnp.dot(a, b, preferred_element_type=F32)


def _ffn_kernel(*refs, pre, post, n_chunks, n_casts, n_in_tiles, n_prompt_tiles):
    if n_in_tiles is None:
        x_ref, g_ref, wg_ref, wu_ref, wo_ref = refs[:5]
        rest = refs[5:]
    else:
        xp_ref, xs_ref, g_ref, wg_ref, wu_ref, wo_ref = refs[:6]
        rest = refs[6:-1]
        x_ref = refs[-1]
    cast_src = rest[:n_casts]
    if n_prompt_tiles is None:
        o_ref = rest[n_casts]
        cast_dst = rest[n_casts + 1:2 * n_casts + 1]
        xn_ref = rest[2 * n_casts + 1]
    else:
        yp_ref, ys_ref = rest[n_casts:n_casts + 2]
        cast_dst = rest[n_casts + 2:2 * n_casts + 2]
        xn_ref, o_ref = rest[2 * n_casts + 2:2 * n_casts + 4]
    f = pl.program_id(1)

    for src, dst in zip(cast_src, cast_dst):
        dst[...] = src[...].astype(BF16)

    def chunk(first):
        xn = xn_ref[...]
        gate = _dot(xn, wg_ref[...])
        up = _dot(xn, wu_ref[...])
        hid = (gate * _sigmoid(gate) * up).astype(BF16)
        for n in range(D_MODEL // ACC_COLS):
            cs = slice(n * ACC_COLS, (n + 1) * ACC_COLS)
            part = _dot(hid, wo_ref[:, cs])
            if first:
                o_ref[:, cs] = part
            else:
                o_ref[:, cs] += part

    i = pl.program_id(0)

    if n_in_tiles is not None:
        def gather(src_ref):
            for t in range(FRAMES):
                x_ref[t * SUBLANES:(t + 1) * SUBLANES, :] = src_ref[:, t, :]

        @pl.when((f == 0) & (i < n_in_tiles))
        def _():
            gather(xp_ref)

        @pl.when((f == 0) & (i >= n_in_tiles))
        def _():
            gather(xs_ref)

    @pl.when(f == 0)
    def _():
        _norm_to_bf16(x_ref, g_ref[pre:pre + 1, :], xn_ref)
        chunk(True)

    @pl.when(f > 0)
    def _():
        chunk(False)

    if n_prompt_tiles is None:
        @pl.when(f == n_chunks - 1)
        def _():
            _residual_norm(x_ref, 0.5 * g_ref[post:post + 1, :], o_ref)
    else:

        def emit(dst_ref):
            g = 0.5 * g_ref[post:post + 1, :]
            for t in range(FRAMES):
                rows = slice(t * SUBLANES, (t + 1) * SUBLANES)
                dst_ref[:, t, :] = x_ref[rows, :] + _rms(o_ref[rows, :], g)

        @pl.when((f == n_chunks - 1) & (i < n_prompt_tiles))
        def _():
            emit(yp_ref)

        @pl.when((f == n_chunks - 1) & (i >= n_prompt_tiles))
        def _():
            emit(ys_ref)


BF16_TILE_ROWS = 16


def _ffn(x, gains, w_in, w_out, layer, which, casts=(), final_seq=None):
    from_streams = isinstance(x, tuple)
    if from_streams:
        x_prompt, x_sample = x
        n_in_tiles = x_prompt.shape[1] // FRAMES
        rows = (n_in_tiles + 1) * ROW_TILE
        x_args = [x_prompt, x_sample]
        x_specs = [
            pl.BlockSpec((SUBLANES, FRAMES, D_MODEL), lambda i, f: (0, jnp.minimum(i, n_in_tiles - 1), 0)),
            pl.BlockSpec((SUBLANES, FRAMES, D_MODEL), lambda i, f: (0, 0, 0)),
        ]
        x_scratch = [pltpu.VMEM((ROW_TILE, D_MODEL), F32)]
        vmem_limit = FIRST_FFN_VMEM_LIMIT
    else:
        n_in_tiles = None
        rows = x.shape[0]
        x_args = [x]
        x_specs = [pl.BlockSpec((ROW_TILE, D_MODEL), lambda i, f: (i, 0))]
        x_scratch = []
        vmem_limit = VMEM_LIMIT
    n_chunks = D_FF // FFN_CHUNK
    n_tiles = rows // ROW_TILE
    n_steps = n_tiles * n_chunks

    cast_in_specs, cast_out_specs, cast_out_shapes = [], [], []
    for src, lead in casts:
        m_rows, m_cols = src.shape[-2:]
        per_step = -(-m_rows // n_steps)
        blk_rows = -(-per_step // BF16_TILE_ROWS) * BF16_TILE_ROWS
        assert m_rows % blk_rows == 0
        n_blk = m_rows // blk_rows

        def blk(i, f, n_blk=n_blk):
            return jnp.minimum(i * n_chunks + f, n_blk - 1)

        cast_in_specs.append(pl.BlockSpec(
            (None,) * len(lead) + (blk_rows, m_cols), lambda i, f, lead=lead, blk=blk: lead + (blk(i, f), 0)))
        cast_out_specs.append(pl.BlockSpec((blk_rows, m_cols), lambda i, f, blk=blk: (blk(i, f), 0)))
        cast_out_shapes.append(jax.ShapeDtypeStruct((m_rows, m_cols), BF16))

    if final_seq is None:
        n_prompt_tiles = None
        main_shapes = [jax.ShapeDtypeStruct((rows, D_MODEL), F32)]
        main_specs = [pl.BlockSpec((ROW_TILE, D_MODEL), lambda i, f: (i, 0))]
        scratch = [pltpu.VMEM((ROW_TILE, D_MODEL), BF16)]
    else:
        n_prompt_tiles = n_tiles - 1
        main_shapes = [jax.ShapeDtypeStruct((SUBLANES, final_seq, D_MODEL), F32),
                       jax.ShapeDtypeStruct((SUBLANES, FRAMES, D_MODEL), F32)]
        main_specs = [
            pl.BlockSpec((SUBLANES, FRAMES, D_MODEL), lambda i, f: (0, jnp.minimum(i, n_prompt_tiles - 1), 0)),
            pl.BlockSpec((SUBLANES, FRAMES, D_MODEL), lambda i, f: (0, 0, 0)),
        ]
        scratch = [pltpu.VMEM((ROW_TILE, D_MODEL), BF16), pltpu.VMEM((ROW_TILE, D_MODEL), F32)]

    kern = functools.partial(_ffn_kernel, pre=4 * which, post=4 * which + 1, n_chunks=n_chunks,
                             n_casts=len(casts), n_in_tiles=n_in_tiles, n_prompt_tiles=n_prompt_tiles)
    outs = pl.pallas_call(
        kern,
        out_shape=main_shapes + cast_out_shapes,
        grid=(n_tiles, n_chunks),
        in_specs=x_specs + [
            pl.BlockSpec((None, 6, D_MODEL), lambda i, f: (layer, 0, 0)),
            pl.BlockSpec((D_MODEL, FFN_CHUNK), lambda i, f: (0, f)),
            pl.BlockSpec((D_MODEL, FFN_CHUNK), lambda i, f: (0, n_chunks + f)),
            pl.BlockSpec((FFN_CHUNK, D_MODEL), lambda i, f: (f, 0)),
        ] + cast_in_specs,
        out_specs=main_specs + cast_out_specs,
        scratch_shapes=scratch + x_scratch,
        compiler_params=pltpu.CompilerParams(
            dimension_semantics=("arbitrary", "arbitrary"), vmem_limit_bytes=vmem_limit),
        name=f"ffn_l{layer}_{which}",
    )(*x_args, gains, w_in, w_in, w_out, *[src for src, _ in casts])
    n_main = len(main_shapes)
    main = outs[0] if n_main == 1 else tuple(outs[:n_main])
    return main, list(outs[n_main:])


N_STEPS = N_RG + 2


def _project_pieces(hb_ref, par, wxr_ref, wy_ref, zc_ref, y_ref, j):
    def run():
        hb = hb_ref[par]
        zc_ref[j % 2][CONV_ROWS:, :] = _dot(hb, wxr_ref[...])
        y_ref[j % 3][...] = _dot(hb, wy_ref[...])
    return [run]


def _pool_project_pieces(hb_ref, par, wp_ref, zp_ref):
    def run():
        zp_ref[POOL_ROWS:, :] = _dot(hb_ref[par], wp_ref[...])
    return [run]


def _conv(j, seq, rgp_ref, zc_ref, xc_ref, xcb_ref, cpre_ref, convo_ref):
    tm = ROW_TILE
    zc = zc_ref[j % 2]
    zc[0:CONV_ROWS, :] = cpre_ref[j]
    half_xc = 0.5 * rgp_ref[4:5, :]
    for k in range(CONV_W):
        half_xc = half_xc + zc[k * SUBLANES:k * SUBLANES + tm, :] * (0.5 * rgp_ref[k:k + 1, :])
    new_pre = zc[tm:tm + CONV_ROWS, :]
    cpre_ref[j] = new_pre
    convo_ref[seq, j] = new_pre
    xc_ref[j % 2][...] = half_xc
    xcb_ref[...] = half_xc.astype(BF16)


def _gate_project(j, wax_ref, xcb_ref, ri_ref):
    for hh in range(RG_HEADS):
        sl = slice(hh * RG_BLOCK, (hh + 1) * RG_BLOCK)
        ri_ref[j % 2][hh] = _dot(xcb_ref[:, sl], wax_ref[hh])


def _recur_groups(j, seq, rgp_ref, ri_ref, xc_ref, y_ref, hst_ref, cst_ref, recg_ref, ho_ref):
    tm = ROW_TILE
    carry = {}

    def setup():
        neg_lam = -rgp_ref[7:8, :]
        softplus = jnp.maximum(neg_lam, 0.0) + jnp.log1p(jnp.exp(-jnp.abs(neg_lam)))
        cst_ref[0] = jnp.broadcast_to(0.5 * rgp_ref[5:6, :], (PAIR, RG_CHUNK))
        cst_ref[1] = jnp.broadcast_to(0.5 * rgp_ref[6:7, :], (PAIR, RG_CHUNK))
        cst_ref[2] = jnp.broadcast_to((0.5 * LRU_C) * softplus, (PAIR, RG_CHUNK))
        carry["h"] = hst_ref[j]

    def pair(r0, h):
        rows = pl.ds(r0, PAIR)
        a_parts, u_parts = [], []
        for hh in range(RG_HEADS):
            sl = slice(hh * RG_BLOCK, (hh + 1) * RG_BLOCK)
            ri = ri_ref[j % 2][hh, rows, :]
            t_r = jnp.tanh(ri[:, :RG_BLOCK] + cst_ref[0, :, sl])
            t_i = jnp.tanh(ri[:, RG_BLOCK:] + cst_ref[1, :, sl])
            ch = cst_ref[2, :, sl]
            neg_log_a = ch * t_r + ch
            a = jnp.exp2(neg_log_a * (-LOG2_E))
            half_xc = xc_ref[j % 2][rows, sl]
            gated = half_xc * t_i + half_xc
            u_parts.append(_sqrt_nonneg(jnp.tanh(neg_log_a) * (a * a + 1.0)) * gated)
            a_parts.append(a)
        a = jnp.concatenate(a_parts, axis=1)
        u = jnp.concatenate(u_parts, axis=1)
        h1 = a[:SUBLANES] * h + u[:SUBLANES]
        h2 = a[SUBLANES:] * h1 + u[SUBLANES:]
        rec = jnp.concatenate([h1, h2], axis=0)
        recg_ref[rows, :] = (rec * _gelu_tanh(y_ref[j % 3][rows, :])).astype(BF16)
        return h2

    def group(k):
        def run():
            if k == 0:
                setup()
            span = tm // N_GROUPS

            def body(p, h):
                return pair(pl.multiple_of(k * span + p * PAIR, PAIR), h)

            carry["h"] = lax.fori_loop(0, span // PAIR, body, carry["h"], unroll=RECUR_UNROLL)
            if k == N_GROUPS - 1:
                hst_ref[j] = carry["h"]
                ho_ref[seq, j] = carry["h"]
        return run

    return [group(k) for k in range(N_GROUPS)]


def _pool_groups(i, seq, n_prompt_tiles, wpm_ref, ps_ref, zp_ref, pooled_ref, poolo_ref):
    tm = ROW_TILE
    frames = tm // SUBLANES

    def group(g):
        def run():
            w = POOL_WINDOWS[g]
            frame = lax.shift_right_logical(lax.broadcasted_iota(jnp.int32, (tm, 1), 0), 3)
            seen = jnp.where(i >= n_prompt_tiles, POOL_MAX - 1, i * frames) + 1
            seen = (frame + seen).astype(F32)
            cs = slice(g * POOL_GROUP, (g + 1) * POOL_GROUP)
            s = zp_ref[POOL_ROWS - (w - 1) * SUBLANES:, cs]
            shift = SUBLANES
            while shift < w * SUBLANES:
                s = s[shift:, :] + s[:s.shape[0] - shift, :]
                shift *= 2
            mean = s * (1.0 / jnp.minimum(seen, float(w)))
            pooled = (mean - zp_ref[POOL_ROWS:, cs]).astype(BF16)
            mixed = _dot(pooled, wpm_ref[g]) * ps_ref[:, cs]
            pooled_ref[:, cs] = mixed.astype(BF16)
            if g == len(POOL_WINDOWS) - 1:
                new_pre = zp_ref[tm:tm + POOL_ROWS, :]
                zp_ref[0:POOL_ROWS, :] = new_pre
                poolo_ref[seq] = new_pre
        return run

    return [group(g) for g in range(len(POOL_WINDOWS))]


def _mixer_a_kernel(x_ref, g_ref, wxr_ref, wy_ref, wp_ref, rgp_conv_ref, rgp_recur_ref, wax_ref, wpm_ref,
                    ps_ref, sconv_ref, sh_ref, spool_ref,
                    recg_ref, pooled_ref, hbn_ref, convo_ref, ho_ref, poolo_ref,
                    hb_ref, cpre_ref, hst_ref, zc0_ref, zc1_ref, xc0_ref, xc1_ref, xcb_ref, ri0_ref, ri1_ref,
                    y0_ref, y1_ref, y2_ref, cst_ref, zp_ref,
                    *, n_prompt_tiles):
    zc_ref = (zc0_ref, zc1_ref)
    xc_ref = (xc0_ref, xc1_ref)
    ri_ref = (ri0_ref, ri1_ref)
    y_ref = (y0_ref, y1_ref, y2_ref)
    i = pl.program_id(0)
    s = pl.program_id(1)
    seq = (i >= n_prompt_tiles).astype(jnp.int32)
    par = i % 2

    def project(j):
        return _project_pieces(hb_ref, par, wxr_ref, wy_ref, zc_ref, y_ref, j)

    def conv(j):
        return lambda: _conv(j, seq, rgp_conv_ref, zc_ref, xc_ref, xcb_ref, cpre_ref, convo_ref)

    def gate_project(j):
        return lambda: _gate_project(j, wax_ref, xcb_ref, ri_ref)

    def recur(j):
        return _recur_groups(j, seq, rgp_recur_ref, ri_ref, xc_ref, y_ref, hst_ref, cst_ref, recg_ref, ho_ref)

    def norm_next():
        span = ROW_TILE // N_GROUPS
        return [functools.partial(_norm_rows, x_ref, g_ref[2:3, :], (hb_ref.at[1 - par], hbn_ref),
                                  k * span, (k + 1) * span)
                for k in range(N_GROUPS)]

    @pl.when(s == 0)
    def _():
        @pl.when(i == 0)
        def _():
            cpre_ref[...] = jnp.zeros(cpre_ref.shape, F32)
            hst_ref[...] = jnp.zeros(hst_ref.shape, F32)
            zp_ref[0:POOL_ROWS, :] = jnp.zeros((POOL_ROWS, D_POOL), F32)
            _norm_to_bf16(x_ref, g_ref[2:3, :], hb_ref.at[0], hbn_ref)

        @pl.when(i == n_prompt_tiles)
        def _():
            for j in range(N_RG):
                cs = slice(j * RG_CHUNK, (j + 1) * RG_CHUNK)
                cpre_ref[j] = sconv_ref[:, cs]
                hst_ref[j] = sh_ref[:, cs]
            zp_ref[0:POOL_ROWS, :] = spool_ref[...]

        for piece in project(0):
            piece()

    def stages(pieces, conv_stage, groups, gate_stage):
        for piece in pieces:
            piece()
        conv_stage()
        gate_stage()
        for group in groups:
            group()

    @pl.when(s == 1)
    def _():
        stages(project(1), conv(0), norm_next(), gate_project(0))

    for step in range(2, N_RG):
        @pl.when(s == step)
        def _(step=step):
            stages(project(step), conv(step - 1), recur(step - 2), gate_project(step - 1))

    @pl.when(s == N_RG)
    def _():
        stages(_pool_project_pieces(hb_ref, par, wp_ref, zp_ref), conv(N_RG - 1), recur(N_RG - 2),
               gate_project(N_RG - 1))

    @pl.when(s == N_RG + 1)
    def _():
        for group in _pool_groups(i, seq, n_prompt_tiles, wpm_ref, ps_ref, zp_ref, pooled_ref, poolo_ref):
            group()
        for group in recur(N_RG - 1):
            group()


def _mixer_a(x, gains, w_in, rgp, wax, wpm, pscale, sconv, sh, spool, layer, n_prompt_tiles):
    rows = x.shape[0]
    n_tiles = rows // ROW_TILE
    pool_blk = (2 * D_RNN) // D_POOL

    def chunk(s, lag):
        return jnp.clip(s - lag, 0, N_RG - 1)

    def x_tile(i, s):
        return jnp.minimum(i + (s >= 1).astype(jnp.int32), n_tiles - 1)

    kern = functools.partial(_mixer_a_kernel, n_prompt_tiles=n_prompt_tiles)
    return pl.pallas_call(
        kern,
        out_shape=(
            jax.ShapeDtypeStruct((rows, D_RNN), BF16),
            jax.ShapeDtypeStruct((rows, D_POOL), BF16),
            jax.ShapeDtypeStruct((rows, D_MODEL), BF16),
            jax.ShapeDtypeStruct((2, N_RG, CONV_ROWS, RG_CHUNK), F32),
            jax.ShapeDtypeStruct((2, N_RG, SUBLANES, RG_CHUNK), F32),
            jax.ShapeDtypeStruct((2, POOL_ROWS, D_POOL), F32),
        ),
        grid=(n_tiles, N_STEPS),
        in_specs=[
            pl.BlockSpec((ROW_TILE, D_MODEL), lambda i, s: (x_tile(i, s), 0)),
            pl.BlockSpec((None, 6, D_MODEL), lambda i, s: (layer, 0, 0)),
            pl.BlockSpec((D_MODEL, RG_CHUNK), lambda i, s: (0, chunk(s, 0))),
            pl.BlockSpec((D_MODEL, RG_CHUNK), lambda i, s: (0, N_RG + chunk(s, 0))),
            pl.BlockSpec((D_MODEL, D_POOL), lambda i, s: (0, pool_blk)),
            pl.BlockSpec((None, 8, RG_CHUNK), lambda i, s: (layer, 0, chunk(s, 1))),
            pl.BlockSpec((None, 8, RG_CHUNK), lambda i, s: (layer, 0, chunk(s, 2))),
            pl.BlockSpec((None, RG_HEADS, RG_BLOCK, 2 * RG_BLOCK), lambda i, s: (layer, chunk(s, 1), 0, 0)),
            pl.BlockSpec((None, len(POOL_WINDOWS), POOL_GROUP, POOL_GROUP), lambda i, s: (layer, 0, 0, 0)),
            pl.BlockSpec((None, 1, D_POOL), lambda i, s: (layer, 0, 0)),
            pl.BlockSpec((None, CONV_ROWS, D_RNN), lambda i, s: (layer, 0, 0)),
            pl.BlockSpec((None, SUBLANES, D_RNN), lambda i, s: (layer, 0, 0)),
            pl.BlockSpec((None, POOL_ROWS, D_POOL), lambda i, s: (layer, 0, 0)),
        ],
        out_specs=(
            pl.BlockSpec((ROW_TILE, RG_CHUNK), lambda i, s: (i, chunk(s, 2))),
            pl.BlockSpec((ROW_TILE, D_POOL), lambda i, s: (i, 0)),
            pl.BlockSpec((ROW_TILE, D_MODEL), lambda i, s: (x_tile(i, s), 0)),
            pl.BlockSpec((2, N_RG, CONV_ROWS, RG_CHUNK), lambda i, s: (0, 0, 0, 0)),
            pl.BlockSpec((2, N_RG, SUBLANES, RG_CHUNK), lambda i, s: (0, 0, 0, 0)),
            pl.BlockSpec((2, POOL_ROWS, D_POOL), lambda i, s: (0, 0, 0)),
        ),
        scratch_shapes=[
            pltpu.VMEM((2, ROW_TILE, D_MODEL), BF16),
            pltpu.VMEM((N_RG, CONV_ROWS, RG_CHUNK), F32),
            pltpu.VMEM((N_RG, SUBLANES, RG_CHUNK), F32),
            pltpu.VMEM((CONV_ROWS + ROW_TILE, RG_CHUNK), F32),
            pltpu.VMEM((CONV_ROWS + ROW_TILE, RG_CHUNK), F32),
            pltpu.VMEM((ROW_TILE, RG_CHUNK), F32),
            pltpu.VMEM((ROW_TILE, RG_CHUNK), F32),
            pltpu.VMEM((ROW_TILE, RG_CHUNK), BF16),
            pltpu.VMEM((RG_HEADS, ROW_TILE, 2 * RG_BLOCK), F32),
            pltpu.VMEM((RG_HEADS, ROW_TILE, 2 * RG_BLOCK), F32),
            pltpu.VMEM((ROW_TILE, RG_CHUNK), F32),
            pltpu.VMEM((ROW_TILE, RG_CHUNK), F32),
            pltpu.VMEM((ROW_TILE, RG_CHUNK), F32),
            pltpu.VMEM((3, PAIR, RG_CHUNK), F32),
            pltpu.VMEM((POOL_ROWS + ROW_TILE, D_POOL), F32),
        ],
        compiler_params=pltpu.CompilerParams(
            dimension_semantics=("arbitrary", "arbitrary"), vmem_limit_bytes=VMEM_LIMIT),
        name=f"mixer_a_l{layer}",
    )(x, gains, w_in, w_in, w_in, rgp, rgp, wax, wpm, pscale, sconv, sh, spool)


def _mixer_b_kernel(x_ref, g_ref, hb_ref, recg_ref, pooled_ref, wgr_ref, wgp_ref, wbr_ref, wbp_ref, wo_ref,
                    o_ref, *, n_chunks):
    n = pl.program_id(1)

    def chunk(first):
        hb = hb_ref[...]
        g_rg = _dot(hb, wgr_ref[...])
        g_pool = _dot(hb, wgp_ref[...])
        br = _dot(recg_ref[...], wbr_ref[...])
        bp = _dot(pooled_ref[...], wbp_ref[...])
        mix = (_sigmoid(g_rg) * br + _sigmoid(g_pool) * bp).astype(BF16)
        for c in range(D_MODEL // ACC_COLS):
            cs = slice(c * ACC_COLS, (c + 1) * ACC_COLS)
            part = _dot(mix, wo_ref[:, cs])
            if first:
                o_ref[:, cs] = part
            else:
                o_ref[:, cs] += part

    @pl.when(n == 0)
    def _():
        chunk(True)

    @pl.when(n > 0)
    def _():
        chunk(False)

    @pl.when(n == n_chunks - 1)
    def _():
        _residual_norm(x_ref, g_ref[3:4, :], o_ref)


def _mixer_b(x, gains, hb, recg, pooled, w_in, w_br_rg, w_br_pool, w_out, layer):
    rows = x.shape[0]
    n_chunks = D_MODEL // OUT_CHUNK
    g_rg_blk = (2 * D_RNN + D_POOL) // OUT_CHUNK
    g_pool_blk = g_rg_blk + n_chunks
    kern = functools.partial(_mixer_b_kernel, n_chunks=n_chunks)

    return pl.pallas_call(
        kern,
        out_shape=jax.ShapeDtypeStruct((rows, D_MODEL), F32),
        grid=(rows // ROW_TILE, n_chunks),
        in_specs=[
            pl.BlockSpec((ROW_TILE, D_MODEL), lambda i, n: (i, 0)),
            pl.BlockSpec((None, 6, D_MODEL), lambda i, n: (layer, 0, 0)),
            pl.BlockSpec((ROW_TILE, D_MODEL), lambda i, n: (i, 0)),
            pl.BlockSpec((ROW_TILE, D_RNN), lambda i, n: (i, 0)),
            pl.BlockSpec((ROW_TILE, D_POOL), lambda i, n: (i, 0)),
            pl.BlockSpec((D_MODEL, OUT_CHUNK), lambda i, n: (0, g_rg_blk + n)),
            pl.BlockSpec((D_MODEL, OUT_CHUNK), lambda i, n: (0, g_pool_blk + n)),
            pl.BlockSpec((D_RNN, OUT_CHUNK), lambda i, n: (0, n)),
            pl.BlockSpec((D_POOL, OUT_CHUNK), lambda i, n: (0, n)),
            pl.BlockSpec((OUT_CHUNK, D_MODEL), lambda i, n: (n, 0)),
        ],
        out_specs=pl.BlockSpec((ROW_TILE, D_MODEL), lambda i, n: (i, 0)),
        compiler_params=pltpu.CompilerParams(
            dimension_semantics=("arbitrary", "arbitrary"), vmem_limit_bytes=VMEM_LIMIT),
        name=f"mixer_b_l{layer}",
    )(x, gains, hb, recg, pooled, w_in, w_in, w_br_rg, w_br_pool, w_out)


def kernel(x_prompt, x_sample, state_conv, state_h, state_pool, norm_gains, w_ffn_in, w_ffn_out, w_in,
           conv_w, conv_b, w_rg_a, b_rg_a, w_rg_x, b_rg_x, lru_param, w_pool_mix, pool_scale,
           w_br_rg, w_br_pool, w_out):
    depth = norm_gains.shape[0]
    batch, seq, _ = x_prompt.shape
    dec_batch = x_sample.shape[0]
    assert batch == SUBLANES and dec_batch == SUBLANES
    n_prompt_rows = batch * seq
    assert n_prompt_rows % ROW_TILE == 0 and x_sample.shape[1] * dec_batch == ROW_TILE
    n_prompt_tiles = n_prompt_rows // ROW_TILE

    x = (x_prompt, x_sample)

    ffn_in_b = w_ffn_in[0, 0].astype(BF16)
    ffn_out_b = w_ffn_out[0, 0].astype(BF16)
    wpm_b = w_pool_mix.astype(BF16)
    wax_b = jnp.concatenate([w_rg_a, w_rg_x], axis=-1).astype(BF16)
    rgp = jnp.concatenate(
        [conv_w, conv_b[:, None], b_rg_a[:, None], b_rg_x[:, None], lru_param[:, None]], axis=1)
    pscale = pool_scale[:, None, :]
    sconv = jnp.swapaxes(state_conv, 1, 2).reshape(depth, CONV_ROWS, D_RNN)
    spool = jnp.swapaxes(state_pool, 1, 2).reshape(depth, POOL_ROWS, D_POOL)

    convs, hs, pools = [], [], []
    for l in range(depth):
        x, (w_in_b, w_br_rg_b, w_br_pool_b, w_out_b, ffn_in_b, ffn_out_b) = _ffn(
            x, norm_gains, ffn_in_b, ffn_out_b, l, 0,
            casts=[(w_in, (l,)), (w_br_rg, (l,)), (w_br_pool, (l,)), (w_out, (l,)),
                   (w_ffn_in, (l, 1)), (w_ffn_out, (l, 1))])
        recg, pooled, hb, conv_o, h_o, pool_o = _mixer_a(
            x, norm_gains, w_in_b, rgp, wax_b, wpm_b, pscale, sconv, state_h, spool, l, n_prompt_tiles)
        x = _mixer_b(x, norm_gains, hb, recg, pooled, w_in_b, w_br_rg_b, w_br_pool_b, w_out_b, l)
        next_ffn = [(w_ffn_in, (l + 1, 0)), (w_ffn_out, (l + 1, 0))] if l + 1 < depth else []
        x, next_b = _ffn(x, norm_gains, ffn_in_b, ffn_out_b, l, 1, casts=next_ffn,
                         final_seq=seq if l + 1 == depth else None)
        if next_b:
            ffn_in_b, ffn_out_b = next_b
        convs.append(jnp.swapaxes(conv_o, 1, 2).reshape(2, CONV_ROWS, D_RNN))
        hs.append(jnp.swapaxes(h_o, 1, 2).reshape(2, SUBLANES, D_RNN))
        pools.append(pool_o)

    y_prompt, y_sample = x
    conv_all = jnp.stack(convs)
    h_all = jnp.stack(hs)
    pool_all = jnp.stack(pools)

    def unroll_state(s, which, frames):
        s = s[:, which]
        return jnp.swapaxes(s.reshape(depth, frames, SUBLANES, s.shape[-1]), 1, 2)

    return (
        y_prompt,
        y_sample,
        unroll_state(conv_all, 0, CONV_W - 1),
        h_all[:, 0],
        unroll_state(pool_all, 0, POOL_MAX - 1),
        unroll_state(conv_all, 1, CONV_W - 1),
        h_all[:, 1],
        unroll_state(pool_all, 1, POOL_MAX - 1),
    )
```

```python
import functools
import math

import jax
import jax.numpy as jnp
from jax import lax
from jax.experimental import pallas as pl
from jax.experimental.pallas import tpu as pltpu

F32 = jnp.float32
BF16 = jnp.bfloat16

D_MODEL = 2048
D_FF = 3 * D_MODEL
D_RNN = D_MODEL
D_POOL = D_MODEL // 2
N_RG_HEADS = 16
RG_BLOCK = D_RNN // N_RG_HEADS
CONV_W = 4
LRU_C = 8.0
POOL_WINDOWS = (2, 4, 8, 16)
POOL_GROUP = D_POOL // len(POOL_WINDOWS)
POOL_MAX = 16
EPS = 1e-6
LOG2_E = 1.4426950408889634

SUBLANES = 8
ROW_TILE = 512
FFN_CHUNK = 1024
RG_CHUNK = 512
OUT_CHUNK = 512
NORM_ROWS = 32
RESIDUAL_ROWS = 8
ACC_COLS = 512
VMEM_LIMIT = 56 * 1024 * 1024
FIRST_FFN_VMEM_LIMIT = 62 * 1024 * 1024

CONV_ROWS = (CONV_W - 1) * SUBLANES
POOL_ROWS = (POOL_MAX - 1) * SUBLANES
N_RG = D_RNN // RG_CHUNK
RG_HEADS = RG_CHUNK // RG_BLOCK
PAIR = 2 * SUBLANES
FRAMES = ROW_TILE // SUBLANES
N_GROUPS = 4
RECUR_UNROLL = 4


def _rms(x, g):
    ms = jnp.mean(x * x, axis=-1, keepdims=True)
    return x * lax.rsqrt(ms + EPS) * g


def _norm_rows(x_ref, g, dst_refs, lo, hi):
    for r in range(lo, hi, NORM_ROWS):
        rows = slice(r, r + NORM_ROWS)
        y = _rms(x_ref[rows, :], g).astype(BF16)
        for dst_ref in dst_refs:
            dst_ref[rows, :] = y


def _norm_to_bf16(x_ref, g, *dst_refs):
    _norm_rows(x_ref, g, dst_refs, 0, x_ref.shape[0])


def _residual_norm(x_ref, g, o_ref):
    for r in range(0, x_ref.shape[0], RESIDUAL_ROWS):
        rows = slice(r, r + RESIDUAL_ROWS)
        o_ref[rows, :] = x_ref[rows, :] + _rms(o_ref[rows, :], g)


def _sigmoid(x):
    return 0.5 * jnp.tanh(0.5 * x) + 0.5


def _gelu_tanh(x):
    c = math.sqrt(2.0 / math.pi)
    half = 0.5 * x
    return half + half * jnp.tanh(x * (c + (c * 0.044715) * (x * x)))


def _sqrt_nonneg(z):
    return jnp.where(z > 0.0, z * lax.rsqrt(z), 0.0)


def _dot(a, b):
    return jnp.dot(a, b, preferred_element_type=F32)


def _ffn_kernel(*refs, pre, post, n_chunks, n_casts, n_in_tiles, n_prompt_tiles):
    if n_in_tiles is None:
        x_ref, g_ref, wg_ref, wu_ref, wo_ref = refs[:5]
        rest = refs[5:]
    else:
        xp_ref, xs_ref, g_ref, wg_ref, wu_ref, wo_ref = refs[:6]
        rest = refs[6:-1]
        x_ref = refs[-1]
    cast_src = rest[:n_casts]
    if n_prompt_tiles is None:
        o_ref = rest[n_casts]
        cast_dst = rest[n_casts + 1:2 * n_casts + 1]
        xn_ref = rest[2 * n_casts + 1]
    else:
        yp_ref, ys_ref = rest[n_casts:n_casts + 2]
        cast_dst = rest[n_casts + 2:2 * n_casts + 2]
        xn_ref, o_ref = rest[2 * n_casts + 2:2 * n_casts + 4]
    f = pl.program_id(1)

    for src, dst in zip(cast_src, cast_dst):
        dst[...] = src[...].astype(BF16)

    def chunk(first):
        xn = xn_ref[...]
        gate = _dot(xn, wg_ref[...])
        up = _dot(xn, wu_ref[...])
        half = 0.5 * gate
        hid = ((half + half * jnp.tanh(half)) * up).astype(BF16)
        for n in range(D_MODEL // ACC_COLS):
            cs = slice(n * ACC_COLS, (n + 1) * ACC_COLS)
            part = _dot(hid, wo_ref[:, cs])
            if first:
                o_ref[:, cs] = part
            else:
                o_ref[:, cs] += part

    i = pl.program_id(0)

    if n_in_tiles is not None:
        def gather(src_ref):
            for t in range(FRAMES):
                x_ref[t * SUBLANES:(t + 1) * SUBLANES, :] = src_ref[:, t, :]

        @pl.when((f == 0) & (i < n_in_tiles))
        def _():
            gather(xp_ref)

        @pl.when((f == 0) & (i >= n_in_tiles))
        def _():
            gather(xs_ref)

    @pl.when(f == 0)
    def _():
        _norm_to_bf16(x_ref, g_ref[pre:pre + 1, :], xn_ref)
        chunk(True)

    @pl.when(f > 0)
    def _():
        chunk(False)

    if n_prompt_tiles is None:
        @pl.when(f == n_chunks - 1)
        def _():
            _residual_norm(x_ref, 0.5 * g_ref[post:post + 1, :], o_ref)
    else:

        def emit(dst_ref):
            g = 0.5 * g_ref[post:post + 1, :]
            for t in range(FRAMES):
                rows = slice(t * SUBLANES, (t + 1) * SUBLANES)
                dst_ref[:, t, :] = x_ref[rows, :] + _rms(o_ref[rows, :], g)

        @pl.when((f == n_chunks - 1) & (i < n_prompt_tiles))
        def _():
            emit(yp_ref)

        @pl.when((f == n_chunks - 1) & (i >= n_prompt_tiles))
        def _():
            emit(ys_ref)


BF16_TILE_ROWS = 16


def _ffn(x, gains, w_in, w_out, layer, which, casts=(), final_seq=None):
    from_streams = isinstance(x, tuple)
    if from_streams:
        x_prompt, x_sample = x
        n_in_tiles = x_prompt.shape[1] // FRAMES
        rows = (n_in_tiles + 1) * ROW_TILE
        x_args = [x_prompt, x_sample]
        x_specs = [
            pl.BlockSpec((SUBLANES, FRAMES, D_MODEL), lambda i, f: (0, jnp.minimum(i, n_in_tiles - 1), 0)),
            pl.BlockSpec((SUBLANES, FRAMES, D_MODEL), lambda i, f: (0, 0, 0)),
        ]
        x_scratch = [pltpu.VMEM((ROW_TILE, D_MODEL), F32)]
        vmem_limit = FIRST_FFN_VMEM_LIMIT
    else:
        n_in_tiles = None
        rows = x.shape[0]
        x_args = [x]
        x_specs = [pl.BlockSpec((ROW_TILE, D_MODEL), lambda i, f: (i, 0))]
        x_scratch = []
        vmem_limit = VMEM_LIMIT
    n_chunks = D_FF // FFN_CHUNK
    n_tiles = rows // ROW_TILE
    n_steps = n_tiles * n_chunks

    cast_in_specs, cast_out_specs, cast_out_shapes = [], [], []
    for src, lead in casts:
        m_rows, m_cols = src.shape[-2:]
        per_step = -(-m_rows // n_steps)
        blk_rows = -(-per_step // BF16_TILE_ROWS) * BF16_TILE_ROWS
        assert m_rows % blk_rows == 0
        n_blk = m_rows // blk_rows

        def blk(i, f, n_blk=n_blk):
            return jnp.minimum(i * n_chunks + f, n_blk - 1)

        cast_in_specs.append(pl.BlockSpec(
            (None,) * len(lead) + (blk_rows, m_cols), lambda i, f, lead=lead, blk=blk: lead + (blk(i, f), 0)))
        cast_out_specs.append(pl.BlockSpec((blk_rows, m_cols), lambda i, f, blk=blk: (blk(i, f), 0)))
        cast_out_shapes.append(jax.ShapeDtypeStruct((m_rows, m_cols), BF16))

    if final_seq is None:
        n_prompt_tiles = None
        main_shapes = [jax.ShapeDtypeStruct((rows, D_MODEL), F32)]
        main_specs = [pl.BlockSpec((ROW_TILE, D_MODEL), lambda i, f: (i, 0))]
        scratch = [pltpu.VMEM((ROW_TILE, D_MODEL), BF16)]
    else:
        n_prompt_tiles = n_tiles - 1
        main_shapes = [jax.ShapeDtypeStruct((SUBLANES, final_seq, D_MODEL), F32),
                       jax.ShapeDtypeStruct((SUBLANES, FRAMES, D_MODEL), F32)]
        main_specs = [
            pl.BlockSpec((SUBLANES, FRAMES, D_MODEL), lambda i, f: (0, jnp.minimum(i, n_prompt_tiles - 1), 0)),
            pl.BlockSpec((SUBLANES, FRAMES, D_MODEL), lambda i, f: (0, 0, 0)),
        ]
        scratch = [pltpu.VMEM((ROW_TILE, D_MODEL), BF16), pltpu.VMEM((ROW_TILE, D_MODEL), F32)]

    kern = functools.partial(_ffn_kernel, pre=4 * which, post=4 * which + 1, n_chunks=n_chunks,
                             n_casts=len(casts), n_in_tiles=n_in_tiles, n_prompt_tiles=n_prompt_tiles)
    outs = pl.pallas_call(
        kern,
        out_shape=main_shapes + cast_out_shapes,
        grid=(n_tiles, n_chunks),
        in_specs=x_specs + [
            pl.BlockSpec((None, 6, D_MODEL), lambda i, f: (layer, 0, 0)),
            pl.BlockSpec((D_MODEL, FFN_CHUNK), lambda i, f: (0, f)),
            pl.BlockSpec((D_MODEL, FFN_CHUNK), lambda i, f: (0, n_chunks + f)),
            pl.BlockSpec((FFN_CHUNK, D_MODEL), lambda i, f: (f, 0)),
        ] + cast_in_specs,
        out_specs=main_specs + cast_out_specs,
        scratch_shapes=scratch + x_scratch,
        compiler_params=pltpu.CompilerParams(
            dimension_semantics=("arbitrary", "arbitrary"), vmem_limit_bytes=vmem_limit),
        name=f"ffn_l{layer}_{which}",
    )(*x_args, gains, w_in, w_in, w_out, *[src for src, _ in casts])
    n_main = len(main_shapes)
    main = outs[0] if n_main == 1 else tuple(outs[:n_main])
    return main, list(outs[n_main:])


N_STEPS = N_RG + 2


def _project_pieces(hb_ref, par, wxr_ref, wy_ref, zc_ref, y_ref, j):
    def run():
        hb = hb_ref[par]
        zc_ref[j % 2][CONV_ROWS:, :] = _dot(hb, wxr_ref[...])
        y_ref[j % 3][...] = _dot(hb, wy_ref[...])
    return [run]


def _pool_project_pieces(hb_ref, par, wp_ref, zp_ref):
    def run():
        zp_ref[POOL_ROWS:, :] = _dot(hb_ref[par], wp_ref[...])
    return [run]


def _conv(j, seq, rgp_ref, zc_ref, xc_ref, xcb_ref, cpre_ref, convo_ref):
    tm = ROW_TILE
    zc = zc_ref[j % 2]
    zc[0:CONV_ROWS, :] = cpre_ref[j]
    half_xc = 0.5 * rgp_ref[4:5, :]
    for k in range(CONV_W):
        half_xc = half_xc + zc[k * SUBLANES:k * SUBLANES + tm, :] * (0.5 * rgp_ref[k:k + 1, :])
    new_pre = zc[tm:tm + CONV_ROWS, :]
    cpre_ref[j] = new_pre
    convo_ref[seq, j] = new_pre
    xc_ref[j % 2][...] = half_xc
    xcb_ref[...] = half_xc.astype(BF16)


def _gate_project(j, wax_ref, xcb_ref, ri_ref):
    for hh in range(RG_HEADS):
        sl = slice(hh * RG_BLOCK, (hh + 1) * RG_BLOCK)
        ri_ref[j % 2][hh] = _dot(xcb_ref[:, sl], wax_ref[hh])


def _recur_groups(j, seq, rgp_ref, ri_ref, xc_ref, y_ref, hst_ref, cst_ref, recg_ref, ho_ref):
    tm = ROW_TILE
    carry = {}

    def setup():
        neg_lam = -rgp_ref[7:8, :]
        softplus = jnp.maximum(neg_lam, 0.0) + jnp.log1p(jnp.exp(-jnp.abs(neg_lam)))
        cst_ref[0] = jnp.broadcast_to(0.5 * rgp_ref[5:6, :], (PAIR, RG_CHUNK))
        cst_ref[1] = jnp.broadcast_to(0.5 * rgp_ref[6:7, :], (PAIR, RG_CHUNK))
        cst_ref[2] = jnp.broadcast_to((0.5 * LRU_C) * softplus, (PAIR, RG_CHUNK))
        carry["h"] = hst_ref[j]

    def pair(r0, h):
        rows = pl.ds(r0, PAIR)
        a_parts, u_parts = [], []
        for hh in range(RG_HEADS):
            sl = slice(hh * RG_BLOCK, (hh + 1) * RG_BLOCK)
            ri = ri_ref[j % 2][hh, rows, :]
            t_r = jnp.tanh(ri[:, :RG_BLOCK] + cst_ref[0, :, sl])
            t_i = jnp.tanh(ri[:, RG_BLOCK:] + cst_ref[1, :, sl])
            ch = cst_ref[2, :, sl]
            neg_log_a = ch * t_r + ch
            a = jnp.exp2(neg_log_a * (-LOG2_E))
            half_xc = xc_ref[j % 2][rows, sl]
            gated = half_xc * t_i + half_xc
            u_parts.append(_sqrt_nonneg(jnp.tanh(neg_log_a) * (a * a + 1.0)) * gated)
            a_parts.append(a)
        a = jnp.concatenate(a_parts, axis=1)
        u = jnp.concatenate(u_parts, axis=1)
        h1 = a[:SUBLANES] * h + u[:SUBLANES]
        h2 = a[SUBLANES:] * h1 + u[SUBLANES:]
        rec = jnp.concatenate([h1, h2], axis=0)
        recg_ref[rows, :] = (rec * _gelu_tanh(y_ref[j % 3][rows, :])).astype(BF16)
        return h2

    def group(k):
        def run():
            if k == 0:
                setup()
            span = tm // N_GROUPS

            def body(p, h):
                return pair(pl.multiple_of(k * span + p * PAIR, PAIR), h)

            carry["h"] = lax.fori_loop(0, span // PAIR, body, carry["h"], unroll=RECUR_UNROLL)
            if k == N_GROUPS - 1:
                hst_ref[j] = carry["h"]
                ho_ref[seq, j] = carry["h"]
        return run

    return [group(k) for k in range(N_GROUPS)]


def _pool_groups(i, seq, n_prompt_tiles, wpm_ref, ps_ref, zp_ref, pooled_ref, poolo_ref):
    tm = ROW_TILE
    frames = tm // SUBLANES

    def group(g):
        def run():
            w = POOL_WINDOWS[g]
            frame = lax.shift_right_logical(lax.broadcasted_iota(jnp.int32, (tm, 1), 0), 3)
            seen = jnp.where(i >= n_prompt_tiles, POOL_MAX - 1, i * frames) + 1
            seen = (frame + seen).astype(F32)
            cs = slice(g * POOL_GROUP, (g + 1) * POOL_GROUP)
            s = zp_ref[POOL_ROWS - (w - 1) * SUBLANES:, cs]
            shift = SUBLANES
            while shift < w * SUBLANES:
                s = s[shift:, :] + s[:s.shape[0] - shift, :]
                shift *= 2
            mean = s * (1.0 / jnp.minimum(seen, float(w)))
            pooled = (mean - zp_ref[POOL_ROWS:, cs]).astype(BF16)
            mixed = _dot(pooled, wpm_ref[g]) * ps_ref[:, cs]
            pooled_ref[:, cs] = mixed.astype(BF16)
            if g == len(POOL_WINDOWS) - 1:
                new_pre = zp_ref[tm:tm + POOL_ROWS, :]
                zp_ref[0:POOL_ROWS, :] = new_pre
                poolo_ref[seq] = new_pre
        return run

    return [group(g) for g in range(len(POOL_WINDOWS))]


def _mixer_a_kernel(x_ref, g_ref, wxr_ref, wy_ref, wp_ref, rgp_conv_ref, rgp_recur_ref, wax_ref, wpm_ref,
                    ps_ref, sconv_ref, sh_ref, spool_ref,
                    recg_ref, pooled_ref, hbn_ref, convo_ref, ho_ref, poolo_ref,
                    hb_ref, cpre_ref, hst_ref, zc0_ref, zc1_ref, xc0_ref, xc1_ref, xcb_ref, ri0_ref, ri1_ref,
                    y0_ref, y1_ref, y2_ref, cst_ref, zp_ref,
                    *, n_prompt_tiles):
    zc_ref = (zc0_ref, zc1_ref)
    xc_ref = (xc0_ref, xc1_ref)
    ri_ref = (ri0_ref, ri1_ref)
    y_ref = (y0_ref, y1_ref, y2_ref)
    i = pl.program_id(0)
    s = pl.program_id(1)
    seq = (i >= n_prompt_tiles).astype(jnp.int32)
    par = i % 2

    def project(j):
        return _project_pieces(hb_ref, par, wxr_ref, wy_ref, zc_ref, y_ref, j)

    def conv(j):
        return lambda: _conv(j, seq, rgp_conv_ref, zc_ref, xc_ref, xcb_ref, cpre_ref, convo_ref)

    def gate_project(j):
        return lambda: _gate_project(j, wax_ref, xcb_ref, ri_ref)

    def recur(j):
        return _recur_groups(j, seq, rgp_recur_ref, ri_ref, xc_ref, y_ref, hst_ref, cst_ref, recg_ref, ho_ref)

    def norm_next():
        span = ROW_TILE // N_GROUPS
        return [functools.partial(_norm_rows, x_ref, g_ref[2:3, :], (hb_ref.at[1 - par], hbn_ref),
                                  k * span, (k + 1) * span)
                for k in range(N_GROUPS)]

    @pl.when(s == 0)
    def _():
        @pl.when(i == 0)
        def _():
            cpre_ref[...] = jnp.zeros(cpre_ref.shape, F32)
            hst_ref[...] = jnp.zeros(hst_ref.shape, F32)
            zp_ref[0:POOL_ROWS, :] = jnp.zeros((POOL_ROWS, D_POOL), F32)
            _norm_to_bf16(x_ref, g_ref[2:3, :], hb_ref.at[0], hbn_ref)

        @pl.when(i == n_prompt_tiles)
        def _():
            for j in range(N_RG):
                cs = slice(j * RG_CHUNK, (j + 1) * RG_CHUNK)
                cpre_ref[j] = sconv_ref[:, cs]
                hst_ref[j] = sh_ref[:, cs]
            zp_ref[0:POOL_ROWS, :] = spool_ref[...]

        for piece in project(0):
            piece()

    def stages(pieces, conv_stage, groups, gate_stage):
        for piece in pieces:
            piece()
        conv_stage()
        gate_stage()
        for group in groups:
            group()

    @pl.when(s == 1)
    def _():
        stages(project(1), conv(0), norm_next(), gate_project(0))

    for step in range(2, N_RG):
        @pl.when(s == step)
        def _(step=step):
            stages(project(step), conv(step - 1), recur(step - 2), gate_project(step - 1))

    @pl.when(s == N_RG)
    def _():
        stages(_pool_project_pieces(hb_ref, par, wp_ref, zp_ref), conv(N_RG - 1), recur(N_RG - 2),
               gate_project(N_RG - 1))

    @pl.when(s == N_RG + 1)
    def _():
        for group in _pool_groups(i, seq, n_prompt_tiles, wpm_ref, ps_ref, zp_ref, pooled_ref, poolo_ref):
            group()
        for group in recur(N_RG - 1):
            group()


def _mixer_a(x, gains, w_in, rgp, wax, wpm, pscale, sconv, sh, spool, layer, n_prompt_tiles):
    rows = x.shape[0]
    n_tiles = rows // ROW_TILE
    pool_blk = (2 * D_RNN) // D_POOL

    def chunk(s, lag):
        return jnp.clip(s - lag, 0, N_RG - 1)

    def x_tile(i, s):
        return jnp.minimum(i + (s >= 1).astype(jnp.int32), n_tiles - 1)

    kern = functools.partial(_mixer_a_kernel, n_prompt_tiles=n_prompt_tiles)
    return pl.pallas_call(
        kern,
        out_shape=(
            jax.ShapeDtypeStruct((rows, D_RNN), BF16),
            jax.ShapeDtypeStruct((rows, D_POOL), BF16),
            jax.ShapeDtypeStruct((rows, D_MODEL), BF16),
            jax.ShapeDtypeStruct((2, N_RG, CONV_ROWS, RG_CHUNK), F32),
            jax.ShapeDtypeStruct((2, N_RG, SUBLANES, RG_CHUNK), F32),
            jax.ShapeDtypeStruct((2, POOL_ROWS, D_POOL), F32),
        ),
        grid=(n_tiles, N_STEPS),
        in_specs=[
            pl.BlockSpec((ROW_TILE, D_MODEL), lambda i, s: (x_tile(i, s), 0)),
            pl.BlockSpec((None, 6, D_MODEL), lambda i, s: (layer, 0, 0)),
            pl.BlockSpec((D_MODEL, RG_CHUNK), lambda i, s: (0, chunk(s, 0))),
            pl.BlockSpec((D_MODEL, RG_CHUNK), lambda i, s: (0, N_RG + chunk(s, 0))),
            pl.BlockSpec((D_MODEL, D_POOL), lambda i, s: (0, pool_blk)),
            pl.BlockSpec((None, 8, RG_CHUNK), lambda i, s: (layer, 0, chunk(s, 1))),
            pl.BlockSpec((None, 8, RG_CHUNK), lambda i, s: (layer, 0, chunk(s, 2))),
            pl.BlockSpec((None, RG_HEADS, RG_BLOCK, 2 * RG_BLOCK), lambda i, s: (layer, chunk(s, 1), 0, 0)),
            pl.BlockSpec((None, len(POOL_WINDOWS), POOL_GROUP, POOL_GROUP), lambda i, s: (layer, 0, 0, 0)),
            pl.BlockSpec((None, 1, D_POOL), lambda i, s: (layer, 0, 0)),
            pl.BlockSpec((None, CONV_ROWS, D_RNN), lambda i, s: (layer, 0, 0)),
            pl.BlockSpec((None, SUBLANES, D_RNN), lambda i, s: (layer, 0, 0)),
            pl.BlockSpec((None, POOL_ROWS, D_POOL), lambda i, s: (layer, 0, 0)),
        ],
        out_specs=(
            pl.BlockSpec((ROW_TILE, RG_CHUNK), lambda i, s: (i, chunk(s, 2))),
            pl.BlockSpec((ROW_TILE, D_POOL), lambda i, s: (i, 0)),
            pl.BlockSpec((ROW_TILE, D_MODEL), lambda i, s: (x_tile(i, s), 0)),
            pl.BlockSpec((2, N_RG, CONV_ROWS, RG_CHUNK), lambda i, s: (0, 0, 0, 0)),
            pl.BlockSpec((2, N_RG, SUBLANES, RG_CHUNK), lambda i, s: (0, 0, 0, 0)),
            pl.BlockSpec((2, POOL_ROWS, D_POOL), lambda i, s: (0, 0, 0)),
        ),
        scratch_shapes=[
            pltpu.VMEM((2, ROW_TILE, D_MODEL), BF16),
            pltpu.VMEM((N_RG, CONV_ROWS, RG_CHUNK), F32),
            pltpu.VMEM((N_RG, SUBLANES, RG_CHUNK), F32),
            pltpu.VMEM((CONV_ROWS + ROW_TILE, RG_CHUNK), F32),
            pltpu.VMEM((CONV_ROWS + ROW_TILE, RG_CHUNK), F32),
            pltpu.VMEM((ROW_TILE, RG_CHUNK), F32),
            pltpu.VMEM((ROW_TILE, RG_CHUNK), F32),
            pltpu.VMEM((ROW_TILE, RG_CHUNK), BF16),
            pltpu.VMEM((RG_HEADS, ROW_TILE, 2 * RG_BLOCK), F32),
            pltpu.VMEM((RG_HEADS, ROW_TILE, 2 * RG_BLOCK), F32),
            pltpu.VMEM((ROW_TILE, RG_CHUNK), F32),
            pltpu.VMEM((ROW_TILE, RG_CHUNK), F32),
            pltpu.VMEM((ROW_TILE, RG_CHUNK), F32),
            pltpu.VMEM((3, PAIR, RG_CHUNK), F32),
            pltpu.VMEM((POOL_ROWS + ROW_TILE, D_POOL), F32),
        ],
        compiler_params=pltpu.CompilerParams(
            dimension_semantics=("arbitrary", "arbitrary"), vmem_limit_bytes=VMEM_LIMIT),
        name=f"mixer_a_l{layer}",
    )(x, gains, w_in, w_in, w_in, rgp, rgp, wax, wpm, pscale, sconv, sh, spool)


def _mixer_b_kernel(x_ref, g_ref, hb_ref, recg_ref, pooled_ref, wgr_ref, wgp_ref, wbr_ref, wbp_ref, wo_ref,
                    o_ref, *, n_chunks):
    n = pl.program_id(1)

    def chunk(first):
        hb = hb_ref[...]
        g_rg = _dot(hb, wgr_ref[...])
        g_pool = _dot(hb, wgp_ref[...])
        br = _dot(recg_ref[...], wbr_ref[...])
        bp = _dot(pooled_ref[...], wbp_ref[...])
        mix = (_sigmoid(g_rg) * br + _sigmoid(g_pool) * bp).astype(BF16)
        for c in range(D_MODEL // ACC_COLS):
            cs = slice(c * ACC_COLS, (c + 1) * ACC_COLS)
            part = _dot(mix, wo_ref[:, cs])
            if first:
                o_ref[:, cs] = part
            else:
                o_ref[:, cs] += part

    @pl.when(n == 0)
    def _():
        chunk(True)

    @pl.when(n > 0)
    def _():
        chunk(False)

    @pl.when(n == n_chunks - 1)
    def _():
        _residual_norm(x_ref, g_ref[3:4, :], o_ref)


def _mixer_b(x, gains, hb, recg, pooled, w_in, w_br_rg, w_br_pool, w_out, layer):
    rows = x.shape[0]
    n_chunks = D_MODEL // OUT_CHUNK
    g_rg_blk = (2 * D_RNN + D_POOL) // OUT_CHUNK
    g_pool_blk = g_rg_blk + n_chunks
    kern = functools.partial(_mixer_b_kernel, n_chunks=n_chunks)

    return pl.pallas_call(
        kern,
        out_shape=jax.ShapeDtypeStruct((rows, D_MODEL), F32),
        grid=(rows // ROW_TILE, n_chunks),
        in_specs=[
            pl.BlockSpec((ROW_TILE, D_MODEL), lambda i, n: (i, 0)),
            pl.BlockSpec((None, 6, D_MODEL), lambda i, n: (layer, 0, 0)),
            pl.BlockSpec((ROW_TILE, D_MODEL), lambda i, n: (i, 0)),
            pl.BlockSpec((ROW_TILE, D_RNN), lambda i, n: (i, 0)),
            pl.BlockSpec((ROW_TILE, D_POOL), lambda i, n: (i, 0)),
            pl.BlockSpec((D_MODEL, OUT_CHUNK), lambda i, n: (0, g_rg_blk + n)),
            pl.BlockSpec((D_MODEL, OUT_CHUNK), lambda i, n: (0, g_pool_blk + n)),
            pl.BlockSpec((D_RNN, OUT_CHUNK), lambda i, n: (0, n)),
            pl.BlockSpec((D_POOL, OUT_CHUNK), lambda i, n: (0, n)),
            pl.BlockSpec((OUT_CHUNK, D_MODEL), lambda i, n: (n, 0)),
        ],
        out_specs=pl.BlockSpec((ROW_TILE, D_MODEL), lambda i, n: (i, 0)),
        compiler_params=pltpu.CompilerParams(
            dimension_semantics=("arbitrary", "arbitrary"), vmem_limit_bytes=VMEM_LIMIT),
        name=f"mixer_b_l{layer}",
    )(x, gains, hb, recg, pooled, w_in, w_in, w_br_rg, w_br_pool, w_out)


def kernel(x_prompt, x_sample, state_conv, state_h, state_pool, norm_gains, w_ffn_in, w_ffn_out, w_in,
           conv_w, conv_b, w_rg_a, b_rg_a, w_rg_x, b_rg_x, lru_param, w_pool_mix, pool_scale,
           w_br_rg, w_br_pool, w_out):
    depth = norm_gains.shape[0]
    batch, seq, _ = x_prompt.shape
    dec_batch = x_sample.shape[0]
    assert batch == SUBLANES and dec_batch == SUBLANES
    n_prompt_rows = batch * seq
    assert n_prompt_rows % ROW_TILE == 0 and x_sample.shape[1] * dec_batch == ROW_TILE
    n_prompt_tiles = n_prompt_rows // ROW_TILE

    x = (x_prompt, x_sample)

    ffn_in_b = w_ffn_in[0, 0].astype(BF16)
    ffn_out_b = w_ffn_out[0, 0].astype(BF16)
    wpm_b = w_pool_mix.astype(BF16)
    wax_b = jnp.concatenate([w_rg_a, w_rg_x], axis=-1).astype(BF16)
    rgp = jnp.concatenate(
        [conv_w, conv_b[:, None], b_rg_a[:, None], b_rg_x[:, None], lru_param[:, None]], axis=1)
    pscale = pool_scale[:, None, :]
    sconv = jnp.swapaxes(state_conv, 1, 2).reshape(depth, CONV_ROWS, D_RNN)
    spool = jnp.swapaxes(state_pool, 1, 2).reshape(depth, POOL_ROWS, D_POOL)

    convs, hs, pools = [], [], []
    for l in range(depth):
        x, (w_in_b, w_br_rg_b, w_br_pool_b, w_out_b, ffn_in_b, ffn_out_b) = _ffn(
            x, norm_gains, ffn_in_b, ffn_out_b, l, 0,
            casts=[(w_in, (l,)), (w_br_rg, (l,)), (w_br_pool, (l,)), (w_out, (l,)),
                   (w_ffn_in, (l, 1)), (w_ffn_out, (l, 1))])
        recg, pooled, hb, conv_o, h_o, pool_o = _mixer_a(
            x, norm_gains, w_in_b, rgp, wax_b, wpm_b, pscale, sconv, state_h, spool, l, n_prompt_tiles)
        x = _mixer_b(x, norm_gains, hb, recg, pooled, w_in_b, w_br_rg_b, w_br_pool_b, w_out_b, l)
        next_ffn = [(w_ffn_in, (l + 1, 0)), (w_ffn_out, (l + 1, 0))] if l + 1 < depth else []
        x, next_b = _ffn(x, norm_gains, ffn_in_b, ffn_out_b, l, 1, casts=next_ffn,
                         final_seq=seq if l + 1 == depth else None)
        if next_b:
            ffn_in_b, ffn_out_b = next_b
        convs.append(jnp.swapaxes(conv_o, 1, 2).reshape(2, CONV_ROWS, D_RNN))
        hs.append(jnp.swapaxes(h_o, 1, 2).reshape(2, SUBLANES, D_RNN))
        pools.append(pool_o)

    y_prompt, y_sample = x
    conv_all = jnp.stack(convs)
    h_all = jnp.stack(hs)
    pool_all = jnp.stack(pools)

    def unroll_state(s, which, frames):
        s = s[:, which]
        return jnp.swapaxes(s.reshape(depth, frames, SUBLANES, s.shape[-1]), 1, 2)

    return (
        y_prompt,
        y_sample,
        unroll_state(conv_all, 0, CONV_W - 1),
        h_all[:, 0],
        unroll_state(pool_all, 0, POOL_MAX - 1),
        unroll_state(conv_all, 1, CONV_W - 1),
        h_all[:, 1],
        unroll_state(pool_all, 1, POOL_MAX - 1),
    )
```

```python
import functools
import math

import jax
import jax.numpy as jnp
from jax import lax
from jax.experimental import pallas as pl
from jax.experimental.pallas import tpu as pltpu

F32 = jnp.float32
BF16 = jnp.bfloat16

D_MODEL = 2048
D_FF = 3 * D_MODEL
D_RNN = D_MODEL
D_POOL = D_MODEL // 2
N_RG_HEADS = 16
RG_BLOCK = D_RNN // N_RG_HEADS
CONV_W = 4
LRU_C = 8.0
POOL_WINDOWS = (2, 4, 8, 16)
POOL_GROUP = D_POOL // len(POOL_WINDOWS)
POOL_MAX = 16
EPS = 1e-6
LOG2_E = 1.4426950408889634

SUBLANES = 8
ROW_TILE = 512
FFN_CHUNK = 1024
RG_CHUNK = 512
OUT_CHUNK = 512
NORM_ROWS = 32
RESIDUAL_ROWS = 8
ACC_COLS = 512
VMEM_LIMIT = 56 * 1024 * 1024
FIRST_FFN_VMEM_LIMIT = 62 * 1024 * 1024

CONV_ROWS = (CONV_W - 1) * SUBLANES
POOL_ROWS = (POOL_MAX - 1) * SUBLANES
N_RG = D_RNN // RG_CHUNK
RG_HEADS = RG_CHUNK // RG_BLOCK
PAIR = 2 * SUBLANES
FRAMES = ROW_TILE // SUBLANES
N_GROUPS = 4
RECUR_UNROLL = 4


def _rms(x, g):
    ms = jnp.mean(x * x, axis=-1, keepdims=True)
    return x * lax.rsqrt(ms + EPS) * g


def _norm_rows(x_ref, g, dst_refs, lo, hi):
    for r in range(lo, hi, NORM_ROWS):
        rows = slice(r, r + NORM_ROWS)
        y = _rms(x_ref[rows, :], g).astype(BF16)
        for dst_ref in dst_refs:
            dst_ref[rows, :] = y


def _norm_to_bf16(x_ref, g, *dst_refs):
    _norm_rows(x_ref, g, dst_refs, 0, x_ref.shape[0])


def _residual_norm(x_ref, g, o_ref):
    for r in range(0, x_ref.shape[0], RESIDUAL_ROWS):
        rows = slice(r, r + RESIDUAL_ROWS)
        o_ref[rows, :] = x_ref[rows, :] + _rms(o_ref[rows, :], g)


def _sigmoid(x):
    return 0.5 * jnp.tanh(0.5 * x) + 0.5


def _gelu_tanh(x):
    c = math.sqrt(2.0 / math.pi)
    half = 0.5 * x
    return half + half * jnp.tanh(x * (c + (c * 0.044715) * (x * x)))


def _sqrt_nonneg(z):
    return jnp.where(z > 0.0, z * lax.rsqrt(z), 0.0)


def _dot(a, b):
    return jnp.dot(a, b, preferred_element_type=F32)


def _ffn_kernel(*refs, pre, post, n_chunks, n_casts, n_in_tiles, n_prompt_tiles):
    if n_in_tiles is None:
        x_ref, g_ref, wg_ref, wu_ref, wo_ref = refs[:5]
        rest = refs[5:]
    else:
        xp_ref, xs_ref, g_ref, wg_ref, wu_ref, wo_ref = refs[:6]
        rest = refs[6:-1]
        x_ref = refs[-1]
    cast_src = rest[:n_casts]
    if n_prompt_tiles is None:
        o_ref = rest[n_casts]
        cast_dst = rest[n_casts + 1:2 * n_casts + 1]
        xn_ref = rest[2 * n_casts + 1]
    else:
        yp_ref, ys_ref = rest[n_casts:n_casts + 2]
        cast_dst = rest[n_casts + 2:2 * n_casts + 2]
        xn_ref, o_ref = rest[2 * n_casts + 2:2 * n_casts + 4]
    f = pl.program_id(1)

    for src, dst in zip(cast_src, cast_dst):
        dst[...] = src[...].astype(BF16)

    def chunk(first):
        xn = xn_ref[...]
        gate = _dot(xn, wg_ref[...])
        up = _dot(xn, wu_ref[...])
        half = 0.5 * gate
        hid = ((half + half * jnp.tanh(half)) * up).astype(BF16)
        for n in range(D_MODEL // ACC_COLS):
            cs = slice(n * ACC_COLS, (n + 1) * ACC_COLS)
            part = _dot(hid, wo_ref[:, cs])
            if first:
                o_ref[:, cs] = part
            else:
                o_ref[:, cs] += part

    i = pl.program_id(0)

    if n_in_tiles is not None:
        def gather(src_ref):
            for t in range(FRAMES):
                x_ref[t * SUBLANES:(t + 1) * SUBLANES, :] = src_ref[:, t, :]

        @pl.when((f == 0) & (i < n_in_tiles))
        def _():
            gather(xp_ref)

        @pl.when((f == 0) & (i >= n_in_tiles))
        def _():
            gather(xs_ref)

    @pl.when(f == 0)
    def _():
        _norm_to_bf16(x_ref, g_ref[pre:pre + 1, :], xn_ref)
        chunk(True)

    @pl.when(f > 0)
    def _():
        chunk(False)

    if n_prompt_tiles is None:
        @pl.when(f == n_chunks - 1)
        def _():
            _residual_norm(x_ref, 0.5 * g_ref[post:post + 1, :], o_ref)
    else:

        def emit(dst_ref):
            g = 0.5 * g_ref[post:post + 1, :]
            for t in range(FRAMES):
                rows = slice(t * SUBLANES, (t + 1) * SUBLANES)
                dst_ref[:, t, :] = x_ref[rows, :] + _rms(o_ref[rows, :], g)

        @pl.when((f == n_chunks - 1) & (i < n_prompt_tiles))
        def _():
            emit(yp_ref)

        @pl.when((f == n_chunks - 1) & (i >= n_prompt_tiles))
        def _():
            emit(ys_ref)


BF16_TILE_ROWS = 16


def _ffn(x, gains, w_in, w_out, layer, which, casts=(), final_seq=None):
    from_streams = isinstance(x, tuple)
    if from_streams:
        x_prompt, x_sample = x
        n_in_tiles = x_prompt.shape[1] // FRAMES
        rows = (n_in_tiles + 1) * ROW_TILE
        x_args = [x_prompt, x_sample]
        x_specs = [
            pl.BlockSpec((SUBLANES, FRAMES, D_MODEL), lambda i, f: (0, jnp.minimum(i, n_in_tiles - 1), 0)),
            pl.BlockSpec((SUBLANES, FRAMES, D_MODEL), lambda i, f: (0, 0, 0)),
        ]
        x_scratch = [pltpu.VMEM((ROW_TILE, D_MODEL), F32)]
        vmem_limit = FIRST_FFN_VMEM_LIMIT
    else:
        n_in_tiles = None
        rows = x.shape[0]
        x_args = [x]
        x_specs = [pl.BlockSpec((ROW_TILE, D_MODEL), lambda i, f: (i, 0))]
        x_scratch = []
        vmem_limit = VMEM_LIMIT
    n_chunks = D_FF // FFN_CHUNK
    n_tiles = rows // ROW_TILE
    n_steps = n_tiles * n_chunks

    cast_in_specs, cast_out_specs, cast_out_shapes = [], [], []
    for src, lead in casts:
        m_rows, m_cols = src.shape[-2:]
        per_step = -(-m_rows // n_steps)
        blk_rows = -(-per_step // BF16_TILE_ROWS) * BF16_TILE_ROWS
        assert m_rows % blk_rows == 0
        n_blk = m_rows // blk_rows

        def blk(i, f, n_blk=n_blk):
            return jnp.minimum(i * n_chunks + f, n_blk - 1)

        cast_in_specs.append(pl.BlockSpec(
            (None,) * len(lead) + (blk_rows, m_cols), lambda i, f, lead=lead, blk=blk: lead + (blk(i, f), 0)))
        cast_out_specs.append(pl.BlockSpec((blk_rows, m_cols), lambda i, f, blk=blk: (blk(i, f), 0)))
        cast_out_shapes.append(jax.ShapeDtypeStruct((m_rows, m_cols), BF16))

    if final_seq is None:
        n_prompt_tiles = None
        main_shapes = [jax.ShapeDtypeStruct((rows, D_MODEL), F32)]
        main_specs = [pl.BlockSpec((ROW_TILE, D_MODEL), lambda i, f: (i, 0))]
        scratch = [pltpu.VMEM((ROW_TILE, D_MODEL), BF16)]
    else:
        n_prompt_tiles = n_tiles - 1
        main_shapes = [jax.ShapeDtypeStruct((SUBLANES, final_seq, D_MODEL), F32),
                       jax.ShapeDtypeStruct((SUBLANES, FRAMES, D_MODEL), F32)]
        main_specs = [
            pl.BlockSpec((SUBLANES, FRAMES, D_MODEL), lambda i, f: (0, jnp.minimum(i, n_prompt_tiles - 1), 0)),
            pl.BlockSpec((SUBLANES, FRAMES, D_MODEL), lambda i, f: (0, 0, 0)),
        ]
        scratch = [pltpu.VMEM((ROW_TILE, D_MODEL), BF16), pltpu.VMEM((ROW_TILE, D_MODEL), F32)]

    kern = functools.partial(_ffn_kernel, pre=4 * which, post=4 * which + 1, n_chunks=n_chunks,
                             n_casts=len(casts), n_in_tiles=n_in_tiles, n_prompt_tiles=n_prompt_tiles)
    outs = pl.pallas_call(
        kern,
        out_shape=main_shapes + cast_out_shapes,
        grid=(n_tiles, n_chunks),
        in_specs=x_specs + [
            pl.BlockSpec((None, 6, D_MODEL), lambda i, f: (layer, 0, 0)),
            pl.BlockSpec((D_MODEL, FFN_CHUNK), lambda i, f: (0, f)),
            pl.BlockSpec((D_MODEL, FFN_CHUNK), lambda i, f: (0, n_chunks + f)),
            pl.BlockSpec((FFN_CHUNK, D_MODEL), lambda i, f: (f, 0)),
        ] + cast_in_specs,
        out_specs=main_specs + cast_out_specs,
        scratch_shapes=scratch + x_scratch,
        compiler_params=pltpu.CompilerParams(
            dimension_semantics=("arbitrary", "arbitrary"), vmem_limit_bytes=vmem_limit),
        name=f"ffn_l{layer}_{which}",
    )(*x_args, gains, w_in, w_in, w_out, *[src for src, _ in casts])
    n_main = len(main_shapes)
    main = outs[0] if n_main == 1 else tuple(outs[:n_main])
    return main, list(outs[n_main:])


N_STEPS = N_RG + 2


def _project_pieces(hb_ref, par, wxr_ref, wy_ref, zc_ref, y_ref, j):
    def run():
        hb = hb_ref[par]
        zc_ref[j % 2][CONV_ROWS:, :] = _dot(hb, wxr_ref[...])
        y_ref[j % 3][...] = _dot(hb, wy_ref[...])
    return [run]


def _pool_project_pieces(hb_ref, par, wp_ref, zp_ref):
    def run():
        zp_ref[POOL_ROWS:, :] = _dot(hb_ref[par], wp_ref[...])
    return [run]


def _conv(j, seq, rgp_ref, zc_ref, xc_ref, xcb_ref, cpre_ref, convo_ref):
    tm = ROW_TILE
    zc = zc_ref[j % 2]
    zc[0:CONV_ROWS, :] = cpre_ref[j]
    half_xc = 0.5 * rgp_ref[4:5, :]
    for k in range(CONV_W):
        half_xc = half_xc + zc[k * SUBLANES:k * SUBLANES + tm, :] * (0.5 * rgp_ref[k:k + 1, :])
    new_pre = zc[tm:tm + CONV_ROWS, :]
    cpre_ref[j] = new_pre
    convo_ref[seq, j] = new_pre
    xc_ref[j % 2][...] = half_xc
    xcb_ref[...] = half_xc.astype(BF16)


def _gate_project(j, wax_ref, xcb_ref, ri_ref):
    for hh in range(RG_HEADS):
        sl = slice(hh * RG_BLOCK, (hh + 1) * RG_BLOCK)
        ri_ref[j % 2][hh] = _dot(xcb_ref[:, sl], wax_ref[hh])


def _recur_groups(j, seq, rgp_ref, ri_ref, xc_ref, y_ref, hst_ref, cst_ref, recg_ref, ho_ref):
    tm = ROW_TILE
    carry = {}

    def setup():
        neg_lam = -rgp_ref[7:8, :]
        softplus = jnp.maximum(neg_lam, 0.0) + jnp.log1p(jnp.exp(-jnp.abs(neg_lam)))
        cst_ref[0] = jnp.broadcast_to(0.5 * rgp_ref[5:6, :], (PAIR, RG_CHUNK))
        cst_ref[1] = jnp.broadcast_to(0.5 * rgp_ref[6:7, :], (PAIR, RG_CHUNK))
        cst_ref[2] = jnp.broadcast_to((0.5 * LRU_C) * softplus, (PAIR, RG_CHUNK))
        carry["h"] = hst_ref[j]

    def pair(r0, h):
        rows = pl.ds(r0, PAIR)
        a_parts, u_parts = [], []
        for hh in range(RG_HEADS):
            sl = slice(hh * RG_BLOCK, (hh + 1) * RG_BLOCK)
            ri = ri_ref[j % 2][hh, rows, :]
            t_r = jnp.tanh(ri[:, :RG_BLOCK] + cst_ref[0, :, sl])
            t_i = jnp.tanh(ri[:, RG_BLOCK:] + cst_ref[1, :, sl])
            ch = cst_ref[2, :, sl]
            neg_log_a = ch * t_r + ch
            a = jnp.exp2(neg_log_a * (-LOG2_E))
            half_xc = xc_ref[j % 2][rows, sl]
            gated = half_xc * t_i + half_xc
            u_parts.append(_sqrt_nonneg(jnp.tanh(neg_log_a) * (a * a + 1.0)) * gated)
            a_parts.append(a)
        a = jnp.concatenate(a_parts, axis=1)
        u = jnp.concatenate(u_parts, axis=1)
        h1 = a[:SUBLANES] * h + u[:SUBLANES]
        h2 = a[SUBLANES:] * h1 + u[SUBLANES:]
        rec = jnp.concatenate([h1, h2], axis=0)
        recg_ref[rows, :] = (rec * _gelu_tanh(y_ref[j % 3][rows, :])).astype(BF16)
        return h2

    def run():
        setup()

        def body(p, h):
            return pair(pl.multiple_of(p * PAIR, PAIR), h)

        h_last = lax.fori_loop(0, tm // PAIR, body, carry["h"], unroll=RECUR_UNROLL)
        hst_ref[j] = h_last
        ho_ref[seq, j] = h_last

    return [run]


def _pool_groups(i, seq, n_prompt_tiles, wpm_ref, ps_ref, zp_ref, pooled_ref, poolo_ref):
    tm = ROW_TILE
    frames = tm // SUBLANES

    def group(g):
        def run():
            w = POOL_WINDOWS[g]
            frame = lax.shift_right_logical(lax.broadcasted_iota(jnp.int32, (tm, 1), 0), 3)
            seen = jnp.where(i >= n_prompt_tiles, POOL_MAX - 1, i * frames) + 1
            seen = (frame + seen).astype(F32)
            cs = slice(g * POOL_GROUP, (g + 1) * POOL_GROUP)
            s = zp_ref[POOL_ROWS - (w - 1) * SUBLANES:, cs]
            shift = SUBLANES
            while shift < w * SUBLANES:
                s = s[shift:, :] + s[:s.shape[0] - shift, :]
                shift *= 2
            mean = s * (1.0 / jnp.minimum(seen, float(w)))
            pooled = (mean - zp_ref[POOL_ROWS:, cs]).astype(BF16)
            mixed = _dot(pooled, wpm_ref[g]) * ps_ref[:, cs]
            pooled_ref[:, cs] = mixed.astype(BF16)
            if g == len(POOL_WINDOWS) - 1:
                new_pre = zp_ref[tm:tm + POOL_ROWS, :]
                zp_ref[0:POOL_ROWS, :] = new_pre
                poolo_ref[seq] = new_pre
        return run

    return [group(g) for g in range(len(POOL_WINDOWS))]


def _mixer_a_kernel(x_ref, g_ref, wxr_ref, wy_ref, wp_ref, rgp_conv_ref, rgp_recur_ref, wax_ref, wpm_ref,
                    ps_ref, sconv_ref, sh_ref, spool_ref,
                    recg_ref, pooled_ref, hbn_ref, convo_ref, ho_ref, poolo_ref,
                    hb_ref, cpre_ref, hst_ref, zc0_ref, zc1_ref, xc0_ref, xc1_ref, xcb_ref, ri0_ref, ri1_ref,
                    y0_ref, y1_ref, y2_ref, cst_ref, zp_ref,
                    *, n_prompt_tiles):
    zc_ref = (zc0_ref, zc1_ref)
    xc_ref = (xc0_ref, xc1_ref)
    ri_ref = (ri0_ref, ri1_ref)
    y_ref = (y0_ref, y1_ref, y2_ref)
    i = pl.program_id(0)
    s = pl.program_id(1)
    seq = (i >= n_prompt_tiles).astype(jnp.int32)
    par = i % 2

    def project(j):
        return _project_pieces(hb_ref, par, wxr_ref, wy_ref, zc_ref, y_ref, j)

    def conv(j):
        return lambda: _conv(j, seq, rgp_conv_ref, zc_ref, xc_ref, xcb_ref, cpre_ref, convo_ref)

    def gate_project(j):
        return lambda: _gate_project(j, wax_ref, xcb_ref, ri_ref)

    def recur(j):
        return _recur_groups(j, seq, rgp_recur_ref, ri_ref, xc_ref, y_ref, hst_ref, cst_ref, recg_ref, ho_ref)

    def norm_next():
        span = ROW_TILE // N_GROUPS
        return [functools.partial(_norm_rows, x_ref, g_ref[2:3, :], (hb_ref.at[1 - par], hbn_ref),
                                  k * span, (k + 1) * span)
                for k in range(N_GROUPS)]

    @pl.when(s == 0)
    def _():
        @pl.when(i == 0)
        def _():
            cpre_ref[...] = jnp.zeros(cpre_ref.shape, F32)
            hst_ref[...] = jnp.zeros(hst_ref.shape, F32)
            zp_ref[0:POOL_ROWS, :] = jnp.zeros((POOL_ROWS, D_POOL), F32)
            _norm_to_bf16(x_ref, g_ref[2:3, :], hb_ref.at[0], hbn_ref)

        @pl.when(i == n_prompt_tiles)
        def _():
            for j in range(N_RG):
                cs = slice(j * RG_CHUNK, (j + 1) * RG_CHUNK)
                cpre_ref[j] = sconv_ref[:, cs]
                hst_ref[j] = sh_ref[:, cs]
            zp_ref[0:POOL_ROWS, :] = spool_ref[...]

        for piece in project(0):
            piece()

    def stages(pieces, conv_stage, groups, gate_stage):
        for piece in pieces:
            piece()
        conv_stage()
        gate_stage()
        for group in groups:
            group()

    @pl.when(s == 1)
    def _():
        stages(project(1), conv(0), norm_next(), gate_project(0))

    for step in range(2, N_RG):
        @pl.when(s == step)
        def _(step=step):
            stages(project(step), conv(step - 1), recur(step - 2), gate_project(step - 1))

    @pl.when(s == N_RG)
    def _():
        stages(_pool_project_pieces(hb_ref, par, wp_ref, zp_ref), conv(N_RG - 1), recur(N_RG - 2),
               gate_project(N_RG - 1))

    @pl.when(s == N_RG + 1)
    def _():
        for group in _pool_groups(i, seq, n_prompt_tiles, wpm_ref, ps_ref, zp_ref, pooled_ref, poolo_ref):
            group()
        for group in recur(N_RG - 1):
            group()


def _mixer_a(x, gains, w_in, rgp, wax, wpm, pscale, sconv, sh, spool, layer, n_prompt_tiles):
    rows = x.shape[0]
    n_tiles = rows // ROW_TILE
    pool_blk = (2 * D_RNN) // D_POOL

    def chunk(s, lag):
        return jnp.clip(s - lag, 0, N_RG - 1)

    def x_tile(i, s):
        return jnp.minimum(i + (s >= 1).astype(jnp.int32), n_tiles - 1)

    kern = functools.partial(_mixer_a_kernel, n_prompt_tiles=n_prompt_tiles)
    return pl.pallas_call(
        kern,
        out_shape=(
            jax.ShapeDtypeStruct((rows, D_RNN), BF16),
            jax.ShapeDtypeStruct((rows, D_POOL), BF16),
            jax.ShapeDtypeStruct((rows, D_MODEL), BF16),
            jax.ShapeDtypeStruct((2, N_RG, CONV_ROWS, RG_CHUNK), F32),
            jax.ShapeDtypeStruct((2, N_RG, SUBLANES, RG_CHUNK), F32),
            jax.ShapeDtypeStruct((2, POOL_ROWS, D_POOL), F32),
        ),
        grid=(n_tiles, N_STEPS),
        in_specs=[
            pl.BlockSpec((ROW_TILE, D_MODEL), lambda i, s: (x_tile(i, s), 0)),
            pl.BlockSpec((None, 6, D_MODEL), lambda i, s: (layer, 0, 0)),
            pl.BlockSpec((D_MODEL, RG_CHUNK), lambda i, s: (0, chunk(s, 0))),
            pl.BlockSpec((D_MODEL, RG_CHUNK), lambda i, s: (0, N_RG + chunk(s, 0))),
            pl.BlockSpec((D_MODEL, D_POOL), lambda i, s: (0, pool_blk)),
            pl.BlockSpec((None, 8, RG_CHUNK), lambda i, s: (layer, 0, chunk(s, 1))),
            pl.BlockSpec((None, 8, RG_CHUNK), lambda i, s: (layer, 0, chunk(s, 2))),
            pl.BlockSpec((None, RG_HEADS, RG_BLOCK, 2 * RG_BLOCK), lambda i, s: (layer, chunk(s, 1), 0, 0)),
            pl.BlockSpec((None, len(POOL_WINDOWS), POOL_GROUP, POOL_GROUP), lambda i, s: (layer, 0, 0, 0)),
            pl.BlockSpec((None, 1, D_POOL), lambda i, s: (layer, 0, 0)),
            pl.BlockSpec((None, CONV_ROWS, D_RNN), lambda i, s: (layer, 0, 0)),
            pl.BlockSpec((None, SUBLANES, D_RNN), lambda i, s: (layer, 0, 0)),
            pl.BlockSpec((None, POOL_ROWS, D_POOL), lambda i, s: (layer, 0, 0)),
        ],
        out_specs=(
            pl.BlockSpec((ROW_TILE, RG_CHUNK), lambda i, s: (i, chunk(s, 2))),
            pl.BlockSpec((ROW_TILE, D_POOL), lambda i, s: (i, 0)),
            pl.BlockSpec((ROW_TILE, D_MODEL), lambda i, s: (x_tile(i, s), 0)),
            pl.BlockSpec((2, N_RG, CONV_ROWS, RG_CHUNK), lambda i, s: (0, 0, 0, 0)),
            pl.BlockSpec((2, N_RG, SUBLANES, RG_CHUNK), lambda i, s: (0, 0, 0, 0)),
            pl.BlockSpec((2, POOL_ROWS, D_POOL), lambda i, s: (0, 0, 0)),
        ),
        scratch_shapes=[
            pltpu.VMEM((2, ROW_TILE, D_MODEL), BF16),
            pltpu.VMEM((N_RG, CONV_ROWS, RG_CHUNK), F32),
            pltpu.VMEM((N_RG, SUBLANES, RG_CHUNK), F32),
            pltpu.VMEM((CONV_ROWS + ROW_TILE, RG_CHUNK), F32),
            pltpu.VMEM((CONV_ROWS + ROW_TILE, RG_CHUNK), F32),
            pltpu.VMEM((ROW_TILE, RG_CHUNK), F32),
            pltpu.VMEM((ROW_TILE, RG_CHUNK), F32),
            pltpu.VMEM((ROW_TILE, RG_CHUNK), BF16),
            pltpu.VMEM((RG_HEADS, ROW_TILE, 2 * RG_BLOCK), F32),
            pltpu.VMEM((RG_HEADS, ROW_TILE, 2 * RG_BLOCK), F32),
            pltpu.VMEM((ROW_TILE, RG_CHUNK), F32),
            pltpu.VMEM((ROW_TILE, RG_CHUNK), F32),
            pltpu.VMEM((ROW_TILE, RG_CHUNK), F32),
            pltpu.VMEM((3, PAIR, RG_CHUNK), F32),
            pltpu.VMEM((POOL_ROWS + ROW_TILE, D_POOL), F32),
        ],
        compiler_params=pltpu.CompilerParams(
            dimension_semantics=("arbitrary", "arbitrary"), vmem_limit_bytes=VMEM_LIMIT),
        name=f"mixer_a_l{layer}",
    )(x, gains, w_in, w_in, w_in, rgp, rgp, wax, wpm, pscale, sconv, sh, spool)


def _mixer_b_kernel(x_ref, g_ref, hb_ref, recg_ref, pooled_ref, wgr_ref, wgp_ref, wbr_ref, wbp_ref, wo_ref,
                    o_ref, *, n_chunks):
    n = pl.program_id(1)

    def chunk(first):
        hb = hb_ref[...]
        g_rg = _dot(hb, wgr_ref[...])
        g_pool = _dot(hb, wgp_ref[...])
        br = _dot(recg_ref[...], wbr_ref[...])
        bp = _dot(pooled_ref[...], wbp_ref[...])
        mix = (_sigmoid(g_rg) * br + _sigmoid(g_pool) * bp).astype(BF16)
        for c in range(D_MODEL // ACC_COLS):
            cs = slice(c * ACC_COLS, (c + 1) * ACC_COLS)
            part = _dot(mix, wo_ref[:, cs])
            if first:
                o_ref[:, cs] = part
            else:
                o_ref[:, cs] += part

    @pl.when(n == 0)
    def _():
        chunk(True)

    @pl.when(n > 0)
    def _():
        chunk(False)

    @pl.when(n == n_chunks - 1)
    def _():
        _residual_norm(x_ref, g_ref[3:4, :], o_ref)


def _mixer_b(x, gains, hb, recg, pooled, w_in, w_br_rg, w_br_pool, w_out, layer):
    rows = x.shape[0]
    n_chunks = D_MODEL // OUT_CHUNK
    g_rg_blk = (2 * D_RNN + D_POOL) // OUT_CHUNK
    g_pool_blk = g_rg_blk + n_chunks
    kern = functools.partial(_mixer_b_kernel, n_chunks=n_chunks)

    return pl.pallas_call(
        kern,
        out_shape=jax.ShapeDtypeStruct((rows, D_MODEL), F32),
        grid=(rows // ROW_TILE, n_chunks),
        in_specs=[
            pl.BlockSpec((ROW_TILE, D_MODEL), lambda i, n: (i, 0)),
            pl.BlockSpec((None, 6, D_MODEL), lambda i, n: (layer, 0, 0)),
            pl.BlockSpec((ROW_TILE, D_MODEL), lambda i, n: (i, 0)),
            pl.BlockSpec((ROW_TILE, D_RNN), lambda i, n: (i, 0)),
            pl.BlockSpec((ROW_TILE, D_POOL), lambda i, n: (i, 0)),
            pl.BlockSpec((D_MODEL, OUT_CHUNK), lambda i, n: (0, g_rg_blk + n)),
            pl.BlockSpec((D_MODEL, OUT_CHUNK), lambda i, n: (0, g_pool_blk + n)),
            pl.BlockSpec((D_RNN, OUT_CHUNK), lambda i, n: (0, n)),
            pl.BlockSpec((D_POOL, OUT_CHUNK), lambda i, n: (0, n)),
            pl.BlockSpec((OUT_CHUNK, D_MODEL), lambda i, n: (n, 0)),
        ],
        out_specs=pl.BlockSpec((ROW_TILE, D_MODEL), lambda i, n: (i, 0)),
        compiler_params=pltpu.CompilerParams(
            dimension_semantics=("arbitrary", "arbitrary"), vmem_limit_bytes=VMEM_LIMIT),
        name=f"mixer_b_l{layer}",
    )(x, gains, hb, recg, pooled, w_in, w_in, w_br_rg, w_br_pool, w_out)


def kernel(x_prompt, x_sample, state_conv, state_h, state_pool, norm_gains, w_ffn_in, w_ffn_out, w_in,
           conv_w, conv_b, w_rg_a, b_rg_a, w_rg_x, b_rg_x, lru_param, w_pool_mix, pool_scale,
           w_br_rg, w_br_pool, w_out):
    depth = norm_gains.shape[0]
    batch, seq, _ = x_prompt.shape
    dec_batch = x_sample.shape[0]
    assert batch == SUBLANES and dec_batch == SUBLANES
    n_prompt_rows = batch * seq
    assert n_prompt_rows % ROW_TILE == 0 and x_sample.shape[1] * dec_batch == ROW_TILE
    n_prompt_tiles = n_prompt_rows // ROW_TILE

    x = (x_prompt, x_sample)

    ffn_in_b = w_ffn_in[0, 0].astype(BF16)
    ffn_out_b = w_ffn_out[0, 0].astype(BF16)
    wpm_b = w_pool_mix.astype(BF16)
    wax_b = jnp.concatenate([w_rg_a, w_rg_x], axis=-1).astype(BF16)
    rgp = jnp.concatenate(
        [conv_w, conv_b[:, None], b_rg_a[:, None], b_rg_x[:, None], lru_param[:, None]], axis=1)
    pscale = pool_scale[:, None, :]
    sconv = jnp.swapaxes(state_conv, 1, 2).reshape(depth, CONV_ROWS, D_RNN)
    spool = jnp.swapaxes(state_pool, 1, 2).reshape(depth, POOL_ROWS, D_POOL)

    convs, hs, pools = [], [], []
    for l in range(depth):
        x, (w_in_b, w_br_rg_b, w_br_pool_b, w_out_b, ffn_in_b, ffn_out_b) = _ffn(
            x, norm_gains, ffn_in_b, ffn_out_b, l, 0,
            casts=[(w_in, (l,)), (w_br_rg, (l,)), (w_br_pool, (l,)), (w_out, (l,)),
                   (w_ffn_in, (l, 1)), (w_ffn_out, (l, 1))])
        recg, pooled, hb, conv_o, h_o, pool_o = _mixer_a(
            x, norm_gains, w_in_b, rgp, wax_b, wpm_b, pscale, sconv, state_h, spool, l, n_prompt_tiles)
        x = _mixer_b(x, norm_gains, hb, recg, pooled, w_in_b, w_br_rg_b, w_br_pool_b, w_out_b, l)
        next_ffn = [(w_ffn_in, (l + 1, 0)), (w_ffn_out, (l + 1, 0))] if l + 1 < depth else []
        x, next_b = _ffn(x, norm_gains, ffn_in_b, ffn_out_b, l, 1, casts=next_ffn,
                         final_seq=seq if l + 1 == depth else None)
        if next_b:
            ffn_in_b, ffn_out_b = next_b
        convs.append(jnp.swapaxes(conv_o, 1, 2).reshape(2, CONV_ROWS, D_RNN))
        hs.append(jnp.swapaxes(h_o, 1, 2).reshape(2, SUBLANES, D_RNN))
        pools.append(pool_o)

    y_prompt, y_sample = x
    conv_all = jnp.stack(convs)
    h_all = jnp.stack(hs)
    pool_all = jnp.stack(pools)

    def unroll_state(s, which, frames):
        s = s[:, which]
        return jnp.swapaxes(s.reshape(depth, frames, SUBLANES, s.shape[-1]), 1, 2)

    return (
        y_prompt,
        y_sample,
        unroll_state(conv_all, 0, CONV_W - 1),
        h_all[:, 0],
        unroll_state(pool_all, 0, POOL_MAX - 1),
        unroll_state(conv_all, 1, CONV_W - 1),
        h_all[:, 1],
        unroll_state(pool_all, 1, POOL_MAX - 1),
    )
```

```python
import functools
import math

import jax
import jax.numpy as jnp
from jax import lax
from jax.experimental import pallas as pl
from jax.experimental.pallas import tpu as pltpu

F32 = jnp.float32
BF16 = jnp.bfloat16

D_MODEL = 2048
D_FF = 3 * D_MODEL
D_RNN = D_MODEL
D_POOL = D_MODEL // 2
N_RG_HEADS = 16
RG_BLOCK = D_RNN // N_RG_HEADS
CONV_W = 4
LRU_C = 8.0
POOL_WINDOWS = (2, 4, 8, 16)
POOL_GROUP = D_POOL // len(POOL_WINDOWS)
POOL_MAX = 16
EPS = 1e-6
LOG2_E = 1.4426950408889634

SUBLANES = 8
ROW_TILE = 512
FFN_CHUNK = 1024
HID_SPLIT = 2
RG_CHUNK = 512
OUT_CHUNK = 512
NORM_ROWS = 32
RESIDUAL_ROWS = 8
ACC_COLS = 512
VMEM_LIMIT = 56 * 1024 * 1024
FIRST_FFN_VMEM_LIMIT = 62 * 1024 * 1024

CONV_ROWS = (CONV_W - 1) * SUBLANES
POOL_ROWS = (POOL_MAX - 1) * SUBLANES
N_RG = D_RNN // RG_CHUNK
RG_HEADS = RG_CHUNK // RG_BLOCK
PAIR = 2 * SUBLANES
FRAMES = ROW_TILE // SUBLANES
N_GROUPS = 4
RECUR_UNROLL = 4


def _rms(x, g):
    ms = jnp.mean(x * x, axis=-1, keepdims=True)
    return x * lax.rsqrt(ms + EPS) * g


def _norm_rows(x_ref, g, dst_refs, lo, hi):
    for r in range(lo, hi, NORM_ROWS):
        rows = slice(r, r + NORM_ROWS)
        y = _rms(x_ref[rows, :], g).astype(BF16)
        for dst_ref in dst_refs:
            dst_ref[rows, :] = y


def _norm_to_bf16(x_ref, g, *dst_refs):
    _norm_rows(x_ref, g, dst_refs, 0, x_ref.shape[0])


def _residual_norm(x_ref, g, o_ref):
    for r in range(0, x_ref.shape[0], RESIDUAL_ROWS):
        rows = slice(r, r + RESIDUAL_ROWS)
        o_ref[rows, :] = x_ref[rows, :] + _rms(o_ref[rows, :], g)


def _sigmoid(x):
    return 0.5 * jnp.tanh(0.5 * x) + 0.5


def _gelu_tanh(x):
    c = math.sqrt(2.0 / math.pi)
    half = 0.5 * x
    return half + half * jnp.tanh(x * (c + (c * 0.044715) * (x * x)))


def _sqrt_nonneg(z):
    return jnp.where(z > 0.0, z * lax.rsqrt(z), 0.0)


def _dot(a, b):
    return jnp.dot(a, b, preferred_element_type=F32)


def _ffn_kernel(*refs, pre, post, n_chunks, n_casts, n_in_tiles, n_prompt_tiles):
    if n_in_tiles is None:
        x_ref, g_ref, wg_ref, wu_ref, wo_ref = refs[:5]
        rest = refs[5:]
    else:
        xp_ref, xs_ref, g_ref, wg_ref, wu_ref, wo_ref = refs[:6]
        rest = refs[6:-1]
        x_ref = refs[-1]
    cast_src = rest[:n_casts]
    if n_prompt_tiles is None:
        o_ref = rest[n_casts]
        cast_dst = rest[n_casts + 1:2 * n_casts + 1]
        xn_ref = rest[2 * n_casts + 1]
    else:
        yp_ref, ys_ref = rest[n_casts:n_casts + 2]
        cast_dst = rest[n_casts + 2:2 * n_casts + 2]
        xn_ref, o_ref = rest[2 * n_casts + 2:2 * n_casts + 4]
    f = pl.program_id(1)

    for src, dst in zip(cast_src, cast_dst):
        dst[...] = src[...].astype(BF16)

    def chunk(first):
        xn = xn_ref[...]
        width = FFN_CHUNK // HID_SPLIT
        for k in range(HID_SPLIT):
            ks = slice(k * width, (k + 1) * width)
            gate = _dot(xn, wg_ref[:, ks])
            up = _dot(xn, wu_ref[:, ks])
            half = 0.5 * gate
            hid = ((half + half * jnp.tanh(half)) * up).astype(BF16)
            for n in range(D_MODEL // ACC_COLS):
                cs = slice(n * ACC_COLS, (n + 1) * ACC_COLS)
                part = _dot(hid, wo_ref[ks, cs])
                if first and k == 0:
                    o_ref[:, cs] = part
                else:
                    o_ref[:, cs] += part

    i = pl.program_id(0)

    if n_in_tiles is not None:
        def gather(src_ref):
            for t in range(FRAMES):
                x_ref[t * SUBLANES:(t + 1) * SUBLANES, :] = src_ref[:, t, :]

        @pl.when((f == 0) & (i < n_in_tiles))
        def _():
            gather(xp_ref)

        @pl.when((f == 0) & (i >= n_in_tiles))
        def _():
            gather(xs_ref)

    @pl.when(f == 0)
    def _():
        _norm_to_bf16(x_ref, g_ref[pre:pre + 1, :], xn_ref)
        chunk(True)

    @pl.when(f > 0)
    def _():
        chunk(False)

    if n_prompt_tiles is None:
        @pl.when(f == n_chunks - 1)
        def _():
            _residual_norm(x_ref, 0.5 * g_ref[post:post + 1, :], o_ref)
    else:

        def emit(dst_ref):
            g = 0.5 * g_ref[post:post + 1, :]
            for t in range(FRAMES):
                rows = slice(t * SUBLANES, (t + 1) * SUBLANES)
                dst_ref[:, t, :] = x_ref[rows, :] + _rms(o_ref[rows, :], g)

        @pl.when((f == n_chunks - 1) & (i < n_prompt_tiles))
        def _():
            emit(yp_ref)

        @pl.when((f == n_chunks - 1) & (i >= n_prompt_tiles))
        def _():
            emit(ys_ref)


BF16_TILE_ROWS = 16


def _ffn(x, gains, w_in, w_out, layer, which, casts=(), final_seq=None):
    from_streams = isinstance(x, tuple)
    if from_streams:
        x_prompt, x_sample = x
        n_in_tiles = x_prompt.shape[1] // FRAMES
        rows = (n_in_tiles + 1) * ROW_TILE
        x_args = [x_prompt, x_sample]
        x_specs = [
            pl.BlockSpec((SUBLANES, FRAMES, D_MODEL), lambda i, f: (0, jnp.minimum(i, n_in_tiles - 1), 0)),
            pl.BlockSpec((SUBLANES, FRAMES, D_MODEL), lambda i, f: (0, 0, 0)),
        ]
        x_scratch = [pltpu.VMEM((ROW_TILE, D_MODEL), F32)]
        vmem_limit = FIRST_FFN_VMEM_LIMIT
    else:
        n_in_tiles = None
        rows = x.shape[0]
        x_args = [x]
        x_specs = [pl.BlockSpec((ROW_TILE, D_MODEL), lambda i, f: (i, 0))]
        x_scratch = []
        vmem_limit = VMEM_LIMIT
    n_chunks = D_FF // FFN_CHUNK
    n_tiles = rows // ROW_TILE
    n_steps = n_tiles * n_chunks

    cast_in_specs, cast_out_specs, cast_out_shapes = [], [], []
    for src, lead in casts:
        m_rows, m_cols = src.shape[-2:]
        per_step = -(-m_rows // n_steps)
        blk_rows = -(-per_step // BF16_TILE_ROWS) * BF16_TILE_ROWS
        assert m_rows % blk_rows == 0
        n_blk = m_rows // blk_rows

        def blk(i, f, n_blk=n_blk):
            return jnp.minimum(i * n_chunks + f, n_blk - 1)

        cast_in_specs.append(pl.BlockSpec(
            (None,) * len(lead) + (blk_rows, m_cols), lambda i, f, lead=lead, blk=blk: lead + (blk(i, f), 0)))
        cast_out_specs.append(pl.BlockSpec((blk_rows, m_cols), lambda i, f, blk=blk: (blk(i, f), 0)))
        cast_out_shapes.append(jax.ShapeDtypeStruct((m_rows, m_cols), BF16))

    if final_seq is None:
        n_prompt_tiles = None
        main_shapes = [jax.ShapeDtypeStruct((rows, D_MODEL), F32)]
        main_specs = [pl.BlockSpec((ROW_TILE, D_MODEL), lambda i, f: (i, 0))]
        scratch = [pltpu.VMEM((ROW_TILE, D_MODEL), BF16)]
    else:
        n_prompt_tiles = n_tiles - 1
        main_shapes = [jax.ShapeDtypeStruct((SUBLANES, final_seq, D_MODEL), F32),
                       jax.ShapeDtypeStruct((SUBLANES, FRAMES, D_MODEL), F32)]
        main_specs = [
            pl.BlockSpec((SUBLANES, FRAMES, D_MODEL), lambda i, f: (0, jnp.minimum(i, n_prompt_tiles - 1), 0)),
            pl.BlockSpec((SUBLANES, FRAMES, D_MODEL), lambda i, f: (0, 0, 0)),
        ]
        scratch = [pltpu.VMEM((ROW_TILE, D_MODEL), BF16), pltpu.VMEM((ROW_TILE, D_MODEL), F32)]

    kern = functools.partial(_ffn_kernel, pre=4 * which, post=4 * which + 1, n_chunks=n_chunks,
                             n_casts=len(casts), n_in_tiles=n_in_tiles, n_prompt_tiles=n_prompt_tiles)
    outs = pl.pallas_call(
        kern,
        out_shape=main_shapes + cast_out_shapes,
        grid=(n_tiles, n_chunks),
        in_specs=x_specs + [
            pl.BlockSpec((None, 6, D_MODEL), lambda i, f: (layer, 0, 0)),
            pl.BlockSpec((D_MODEL, FFN_CHUNK), lambda i, f: (0, f)),
            pl.BlockSpec((D_MODEL, FFN_CHUNK), lambda i, f: (0, n_chunks + f)),
            pl.BlockSpec((FFN_CHUNK, D_MODEL), lambda i, f: (f, 0)),
        ] + cast_in_specs,
        out_specs=main_specs + cast_out_specs,
        scratch_shapes=scratch + x_scratch,
        compiler_params=pltpu.CompilerParams(
            dimension_semantics=("arbitrary", "arbitrary"), vmem_limit_bytes=vmem_limit),
        name=f"ffn_l{layer}_{which}",
    )(*x_args, gains, w_in, w_in, w_out, *[src for src, _ in casts])
    n_main = len(main_shapes)
    main = outs[0] if n_main == 1 else tuple(outs[:n_main])
    return main, list(outs[n_main:])


N_STEPS = N_RG + 2


def _project_pieces(hb_ref, par, wxr_ref, wy_ref, zc_ref, y_ref, j):
    def run():
        hb = hb_ref[par]
        zc_ref[j % 2][CONV_ROWS:, :] = _dot(hb, wxr_ref[...])
        y_ref[j % 3][...] = _dot(hb, wy_ref[...])
    return [run]


def _pool_project_pieces(hb_ref, par, wp_ref, zp_ref):
    def run():
        zp_ref[POOL_ROWS:, :] = _dot(hb_ref[par], wp_ref[...])
    return [run]


def _conv(j, seq, rgp_ref, zc_ref, xc_ref, xcb_ref, cpre_ref, convo_ref):
    tm = ROW_TILE
    zc = zc_ref[j % 2]
    zc[0:CONV_ROWS, :] = cpre_ref[j]
    half_xc = 0.5 * rgp_ref[4:5, :]
    for k in range(CONV_W):
        half_xc = half_xc + zc[k * SUBLANES:k * SUBLANES + tm, :] * (0.5 * rgp_ref[k:k + 1, :])
    new_pre = zc[tm:tm + CONV_ROWS, :]
    cpre_ref[j] = new_pre
    convo_ref[seq, j] = new_pre
    xc_ref[j % 2][...] = half_xc
    xcb_ref[...] = half_xc.astype(BF16)


def _gate_project(j, wax_ref, xcb_ref, ri_ref):
    for hh in range(RG_HEADS):
        sl = slice(hh * RG_BLOCK, (hh + 1) * RG_BLOCK)
        ri_ref[j % 2][hh] = _dot(xcb_ref[:, sl], wax_ref[hh])


def _recur_groups(j, seq, rgp_ref, ri_ref, xc_ref, y_ref, hst_ref, cst_ref, recg_ref, ho_ref):
    tm = ROW_TILE
    carry = {}

    def setup():
        neg_lam = -rgp_ref[7:8, :]
        softplus = jnp.maximum(neg_lam, 0.0) + jnp.log1p(jnp.exp(-jnp.abs(neg_lam)))
        cst_ref[0] = jnp.broadcast_to(0.5 * rgp_ref[5:6, :], (PAIR, RG_CHUNK))
        cst_ref[1] = jnp.broadcast_to(0.5 * rgp_ref[6:7, :], (PAIR, RG_CHUNK))
        cst_ref[2] = jnp.broadcast_to((0.5 * LRU_C) * softplus, (PAIR, RG_CHUNK))
        carry["h"] = hst_ref[j]

    def pair(r0, h):
        rows = pl.ds(r0, PAIR)
        a_parts, u_parts = [], []
        for hh in range(RG_HEADS):
            sl = slice(hh * RG_BLOCK, (hh + 1) * RG_BLOCK)
            ri = ri_ref[j % 2][hh, rows, :]
            t_r = jnp.tanh(ri[:, :RG_BLOCK] + cst_ref[0, :, sl])
            t_i = jnp.tanh(ri[:, RG_BLOCK:] + cst_ref[1, :, sl])
            ch = cst_ref[2, :, sl]
            neg_log_a = ch * t_r + ch
            a = jnp.exp2(neg_log_a * (-LOG2_E))
            half_xc = xc_ref[j % 2][rows, sl]
            gated = half_xc * t_i + half_xc
            u_parts.append(_sqrt_nonneg(jnp.tanh(neg_log_a) * (a * a + 1.0)) * gated)
            a_parts.append(a)
        a = jnp.concatenate(a_parts, axis=1)
        u = jnp.concatenate(u_parts, axis=1)
        h1 = a[:SUBLANES] * h + u[:SUBLANES]
        h2 = a[SUBLANES:] * h1 + u[SUBLANES:]
        rec = jnp.concatenate([h1, h2], axis=0)
        recg_ref[rows, :] = (rec * _gelu_tanh(y_ref[j % 3][rows, :])).astype(BF16)
        return h2

    def group(k):
        def run():
            if k == 0:
                setup()
            span = tm // N_GROUPS

            def body(p, h):
                return pair(pl.multiple_of(k * span + p * PAIR, PAIR), h)

            carry["h"] = lax.fori_loop(0, span // PAIR, body, carry["h"], unroll=RECUR_UNROLL)
            if k == N_GROUPS - 1:
                hst_ref[j] = carry["h"]
                ho_ref[seq, j] = carry["h"]
        return run

    return [group(k) for k in range(N_GROUPS)]


def _pool_groups(i, seq, n_prompt_tiles, wpm_ref, ps_ref, zp_ref, pooled_ref, poolo_ref):
    tm = ROW_TILE
    frames = tm // SUBLANES

    def group(g):
        def run():
            w = POOL_WINDOWS[g]
            frame = lax.shift_right_logical(lax.broadcasted_iota(jnp.int32, (tm, 1), 0), 3)
            seen = jnp.where(i >= n_prompt_tiles, POOL_MAX - 1, i * frames) + 1
            seen = (frame + seen).astype(F32)
            cs = slice(g * POOL_GROUP, (g + 1) * POOL_GROUP)
            s = zp_ref[POOL_ROWS - (w - 1) * SUBLANES:, cs]
            shift = SUBLANES
            while shift < w * SUBLANES:
                s = s[shift:, :] + s[:s.shape[0] - shift, :]
                shift *= 2
            mean = s * (1.0 / jnp.minimum(seen, float(w)))
            pooled = (mean - zp_ref[POOL_ROWS:, cs]).astype(BF16)
            mixed = _dot(pooled, wpm_ref[g]) * ps_ref[:, cs]
            pooled_ref[:, cs] = mixed.astype(BF16)
            if g == len(POOL_WINDOWS) - 1:
                new_pre = zp_ref[tm:tm + POOL_ROWS, :]
                zp_ref[0:POOL_ROWS, :] = new_pre
                poolo_ref[seq] = new_pre
        return run

    return [group(g) for g in range(len(POOL_WINDOWS))]


def _mixer_a_kernel(x_ref, g_ref, wxr_ref, wy_ref, wp_ref, rgp_conv_ref, rgp_recur_ref, wax_ref, wpm_ref,
                    ps_ref, sconv_ref, sh_ref, spool_ref,
                    recg_ref, pooled_ref, hbn_ref, convo_ref, ho_ref, poolo_ref,
                    hb_ref, cpre_ref, hst_ref, zc0_ref, zc1_ref, xc0_ref, xc1_ref, xcb_ref, ri0_ref, ri1_ref,
                    y0_ref, y1_ref, y2_ref, cst_ref, zp_ref,
                    *, n_prompt_tiles):
    zc_ref = (zc0_ref, zc1_ref)
    xc_ref = (xc0_ref, xc1_ref)
    ri_ref = (ri0_ref, ri1_ref)
    y_ref = (y0_ref, y1_ref, y2_ref)
    i = pl.program_id(0)
    s = pl.program_id(1)
    seq = (i >= n_prompt_tiles).astype(jnp.int32)
    par = i % 2

    def project(j):
        return _project_pieces(hb_ref, par, wxr_ref, wy_ref, zc_ref, y_ref, j)

    def conv(j):
        return lambda: _conv(j, seq, rgp_conv_ref, zc_ref, xc_ref, xcb_ref, cpre_ref, convo_ref)

    def gate_project(j):
        return lambda: _gate_project(j, wax_ref, xcb_ref, ri_ref)

    def recur(j):
        return _recur_groups(j, seq, rgp_recur_ref, ri_ref, xc_ref, y_ref, hst_ref, cst_ref, recg_ref, ho_ref)

    def norm_next():
        span = ROW_TILE // N_GROUPS
        return [functools.partial(_norm_rows, x_ref, g_ref[2:3, :], (hb_ref.at[1 - par], hbn_ref),
                                  k * span, (k + 1) * span)
                for k in range(N_GROUPS)]

    @pl.when(s == 0)
    def _():
        @pl.when(i == 0)
        def _():
            cpre_ref[...] = jnp.zeros(cpre_ref.shape, F32)
            hst_ref[...] = jnp.zeros(hst_ref.shape, F32)
            zp_ref[0:POOL_ROWS, :] = jnp.zeros((POOL_ROWS, D_POOL), F32)
            _norm_to_bf16(x_ref, g_ref[2:3, :], hb_ref.at[0], hbn_ref)

        @pl.when(i == n_prompt_tiles)
        def _():
            for j in range(N_RG):
                cs = slice(j * RG_CHUNK, (j + 1) * RG_CHUNK)
                cpre_ref[j] = sconv_ref[:, cs]
                hst_ref[j] = sh_ref[:, cs]
            zp_ref[0:POOL_ROWS, :] = spool_ref[...]

        for piece in project(0):
            piece()

    def stages(pieces, conv_stage, groups, gate_stage):
        for piece in pieces:
            piece()
        conv_stage()
        gate_stage()
        for group in groups:
            group()

    @pl.when(s == 1)
    def _():
        stages(project(1), conv(0), norm_next(), gate_project(0))

    for step in range(2, N_RG):
        @pl.when(s == step)
        def _(step=step):
            stages(project(step), conv(step - 1), recur(step - 2), gate_project(step - 1))

    @pl.when(s == N_RG)
    def _():
        stages(_pool_project_pieces(hb_ref, par, wp_ref, zp_ref), conv(N_RG - 1), recur(N_RG - 2),
               gate_project(N_RG - 1))

    @pl.when(s == N_RG + 1)
    def _():
        for group in _pool_groups(i, seq, n_prompt_tiles, wpm_ref, ps_ref, zp_ref, pooled_ref, poolo_ref):
            group()
        for group in recur(N_RG - 1):
            group()


def _mixer_a(x, gains, w_in, rgp, wax, wpm, pscale, sconv, sh, spool, layer, n_prompt_tiles):
    rows = x.shape[0]
    n_tiles = rows // ROW_TILE
    pool_blk = (2 * D_RNN) // D_POOL

    def chunk(s, lag):
        return jnp.clip(s - lag, 0, N_RG - 1)

    def x_tile(i, s):
        return jnp.minimum(i + (s >= 1).astype(jnp.int32), n_tiles - 1)

    kern = functools.partial(_mixer_a_kernel, n_prompt_tiles=n_prompt_tiles)
    return pl.pallas_call(
        kern,
        out_shape=(
            jax.ShapeDtypeStruct((rows, D_RNN), BF16),
            jax.ShapeDtypeStruct((rows, D_POOL), BF16),
            jax.ShapeDtypeStruct((rows, D_MODEL), BF16),
            jax.ShapeDtypeStruct((2, N_RG, CONV_ROWS, RG_CHUNK), F32),
            jax.ShapeDtypeStruct((2, N_RG, SUBLANES, RG_CHUNK), F32),
            jax.ShapeDtypeStruct((2, POOL_ROWS, D_POOL), F32),
        ),
        grid=(n_tiles, N_STEPS),
        in_specs=[
            pl.BlockSpec((ROW_TILE, D_MODEL), lambda i, s: (x_tile(i, s), 0)),
            pl.BlockSpec((None, 6, D_MODEL), lambda i, s: (layer, 0, 0)),
            pl.BlockSpec((D_MODEL, RG_CHUNK), lambda i, s: (0, chunk(s, 0))),
            pl.BlockSpec((D_MODEL, RG_CHUNK), lambda i, s: (0, N_RG + chunk(s, 0))),
            pl.BlockSpec((D_MODEL, D_POOL), lambda i, s: (0, pool_blk)),
            pl.BlockSpec((None, 8, RG_CHUNK), lambda i, s: (layer, 0, chunk(s, 1))),
            pl.BlockSpec((None, 8, RG_CHUNK), lambda i, s: (layer, 0, chunk(s, 2))),
            pl.BlockSpec((None, RG_HEADS, RG_BLOCK, 2 * RG_BLOCK), lambda i, s: (layer, chunk(s, 1), 0, 0)),
            pl.BlockSpec((None, len(POOL_WINDOWS), POOL_GROUP, POOL_GROUP), lambda i, s: (layer, 0, 0, 0)),
            pl.BlockSpec((None, 1, D_POOL), lambda i, s: (layer, 0, 0)),
            pl.BlockSpec((None, CONV_ROWS, D_RNN), lambda i, s: (layer, 0, 0)),
            pl.BlockSpec((None, SUBLANES, D_RNN), lambda i, s: (layer, 0, 0)),
            pl.BlockSpec((None, POOL_ROWS, D_POOL), lambda i, s: (layer, 0, 0)),
        ],
        out_specs=(
            pl.BlockSpec((ROW_TILE, RG_CHUNK), lambda i, s: (i, chunk(s, 2))),
            pl.BlockSpec((ROW_TILE, D_POOL), lambda i, s: (i, 0)),
            pl.BlockSpec((ROW_TILE, D_MODEL), lambda i, s: (x_tile(i, s), 0)),
            pl.BlockSpec((2, N_RG, CONV_ROWS, RG_CHUNK), lambda i, s: (0, 0, 0, 0)),
            pl.BlockSpec((2, N_RG, SUBLANES, RG_CHUNK), lambda i, s: (0, 0, 0, 0)),
            pl.BlockSpec((2, POOL_ROWS, D_POOL), lambda i, s: (0, 0, 0)),
        ),
        scratch_shapes=[
            pltpu.VMEM((2, ROW_TILE, D_MODEL), BF16),
            pltpu.VMEM((N_RG, CONV_ROWS, RG_CHUNK), F32),
            pltpu.VMEM((N_RG, SUBLANES, RG_CHUNK), F32),
            pltpu.VMEM((CONV_ROWS + ROW_TILE, RG_CHUNK), F32),
            pltpu.VMEM((CONV_ROWS + ROW_TILE, RG_CHUNK), F32),
            pltpu.VMEM((ROW_TILE, RG_CHUNK), F32),
            pltpu.VMEM((ROW_TILE, RG_CHUNK), F32),
            pltpu.VMEM((ROW_TILE, RG_CHUNK), BF16),
            pltpu.VMEM((RG_HEADS, ROW_TILE, 2 * RG_BLOCK), F32),
            pltpu.VMEM((RG_HEADS, ROW_TILE, 2 * RG_BLOCK), F32),
            pltpu.VMEM((ROW_TILE, RG_CHUNK), F32),
            pltpu.VMEM((ROW_TILE, RG_CHUNK), F32),
            pltpu.VMEM((ROW_TILE, RG_CHUNK), F32),
            pltpu.VMEM((3, PAIR, RG_CHUNK), F32),
            pltpu.VMEM((POOL_ROWS + ROW_TILE, D_POOL), F32),
        ],
        compiler_params=pltpu.CompilerParams(
            dimension_semantics=("arbitrary", "arbitrary"), vmem_limit_bytes=VMEM_LIMIT),
        name=f"mixer_a_l{layer}",
    )(x, gains, w_in, w_in, w_in, rgp, rgp, wax, wpm, pscale, sconv, sh, spool)


def _mixer_b_kernel(x_ref, g_ref, hb_ref, recg_ref, pooled_ref, wgr_ref, wgp_ref, wbr_ref, wbp_ref, wo_ref,
                    o_ref, *, n_chunks):
    n = pl.program_id(1)

    def chunk(first):
        hb = hb_ref[...]
        g_rg = _dot(hb, wgr_ref[...])
        g_pool = _dot(hb, wgp_ref[...])
        br = _dot(recg_ref[...], wbr_ref[...])
        bp = _dot(pooled_ref[...], wbp_ref[...])
        mix = (_sigmoid(g_rg) * br + _sigmoid(g_pool) * bp).astype(BF16)
        for c in range(D_MODEL // ACC_COLS):
            cs = slice(c * ACC_COLS, (c + 1) * ACC_COLS)
            part = _dot(mix, wo_ref[:, cs])
            if first:
                o_ref[:, cs] = part
            else:
                o_ref[:, cs] += part

    @pl.when(n == 0)
    def _():
        chunk(True)

    @pl.when(n > 0)
    def _():
        chunk(False)

    @pl.when(n == n_chunks - 1)
    def _():
        _residual_norm(x_ref, g_ref[3:4, :], o_ref)


def _mixer_b(x, gains, hb, recg, pooled, w_in, w_br_rg, w_br_pool, w_out, layer):
    rows = x.shape[0]
    n_chunks = D_MODEL // OUT_CHUNK
    g_rg_blk = (2 * D_RNN + D_POOL) // OUT_CHUNK
    g_pool_blk = g_rg_blk + n_chunks
    kern = functools.partial(_mixer_b_kernel, n_chunks=n_chunks)

    return pl.pallas_call(
        kern,
        out_shape=jax.ShapeDtypeStruct((rows, D_MODEL), F32),
        grid=(rows // ROW_TILE, n_chunks),
        in_specs=[
            pl.BlockSpec((ROW_TILE, D_MODEL), lambda i, n: (i, 0)),
            pl.BlockSpec((None, 6, D_MODEL), lambda i, n: (layer, 0, 0)),
            pl.BlockSpec((ROW_TILE, D_MODEL), lambda i, n: (i, 0)),
            pl.BlockSpec((ROW_TILE, D_RNN), lambda i, n: (i, 0)),
            pl.BlockSpec((ROW_TILE, D_POOL), lambda i, n: (i, 0)),
            pl.BlockSpec((D_MODEL, OUT_CHUNK), lambda i, n: (0, g_rg_blk + n)),
            pl.BlockSpec((D_MODEL, OUT_CHUNK), lambda i, n: (0, g_pool_blk + n)),
            pl.BlockSpec((D_RNN, OUT_CHUNK), lambda i, n: (0, n)),
            pl.BlockSpec((D_POOL, OUT_CHUNK), lambda i, n: (0, n)),
            pl.BlockSpec((OUT_CHUNK, D_MODEL), lambda i, n: (n, 0)),
        ],
        out_specs=pl.BlockSpec((ROW_TILE, D_MODEL), lambda i, n: (i, 0)),
        compiler_params=pltpu.CompilerParams(
            dimension_semantics=("arbitrary", "arbitrary"), vmem_limit_bytes=VMEM_LIMIT),
        name=f"mixer_b_l{layer}",
    )(x, gains, hb, recg, pooled, w_in, w_in, w_br_rg, w_br_pool, w_out)


def kernel(x_prompt, x_sample, state_conv, state_h, state_pool, norm_gains, w_ffn_in, w_ffn_out, w_in,
           conv_w, conv_b, w_rg_a, b_rg_a, w_rg_x, b_rg_x, lru_param, w_pool_mix, pool_scale,
           w_br_rg, w_br_pool, w_out):
    depth = norm_gains.shape[0]
    batch, seq, _ = x_prompt.shape
    dec_batch = x_sample.shape[0]
    assert batch == SUBLANES and dec_batch == SUBLANES
    n_prompt_rows = batch * seq
    assert n_prompt_rows % ROW_TILE == 0 and x_sample.shape[1] * dec_batch == ROW_TILE
    n_prompt_tiles = n_prompt_rows // ROW_TILE

    x = (x_prompt, x_sample)

    ffn_in_b = w_ffn_in[0, 0].astype(BF16)
    ffn_out_b = w_ffn_out[0, 0].astype(BF16)
    wpm_b = w_pool_mix.astype(BF16)
    wax_b = jnp.concatenate([w_rg_a, w_rg_x], axis=-1).astype(BF16)
    rgp = jnp.concatenate(
        [conv_w, conv_b[:, None], b_rg_a[:, None], b_rg_x[:, None], lru_param[:, None]], axis=1)
    pscale = pool_scale[:, None, :]
    sconv = jnp.swapaxes(state_conv, 1, 2).reshape(depth, CONV_ROWS, D_RNN)
    spool = jnp.swapaxes(state_pool, 1, 2).reshape(depth, POOL_ROWS, D_POOL)

    convs, hs, pools = [], [], []
    for l in range(depth):
        x, (w_in_b, w_br_rg_b, w_br_pool_b, w_out_b, ffn_in_b, ffn_out_b) = _ffn(
            x, norm_gains, ffn_in_b, ffn_out_b, l, 0,
            casts=[(w_in, (l,)), (w_br_rg, (l,)), (w_br_pool, (l,)), (w_out, (l,)),
                   (w_ffn_in, (l, 1)), (w_ffn_out, (l, 1))])
        recg, pooled, hb, conv_o, h_o, pool_o = _mixer_a(
            x, norm_gains, w_in_b, rgp, wax_b, wpm_b, pscale, sconv, state_h, spool, l, n_prompt_tiles)
        x = _mixer_b(x, norm_gains, hb, recg, pooled, w_in_b, w_br_rg_b, w_br_pool_b, w_out_b, l)
        next_ffn = [(w_ffn_in, (l + 1, 0)), (w_ffn_out, (l + 1, 0))] if l + 1 < depth else []
        x, next_b = _ffn(x, norm_gains, ffn_in_b, ffn_out_b, l, 1, casts=next_ffn,
                         final_seq=seq if l + 1 == depth else None)
        if next_b:
            ffn_in_b, ffn_out_b = next_b
        convs.append(jnp.swapaxes(conv_o, 1, 2).reshape(2, CONV_ROWS, D_RNN))
        hs.append(jnp.swapaxes(h_o, 1, 2).reshape(2, SUBLANES, D_RNN))
        pools.append(pool_o)

    y_prompt, y_sample = x
    conv_all = jnp.stack(convs)
    h_all = jnp.stack(hs)
    pool_all = jnp.stack(pools)

    def unroll_state(s, which, frames):
        s = s[:, which]
        return jnp.swapaxes(s.reshape(depth, frames, SUBLANES, s.shape[-1]), 1, 2)

    return (
        y_prompt,
        y_sample,
        unroll_state(conv_all, 0, CONV_W - 1),
        h_all[:, 0],
        unroll_state(pool_all, 0, POOL_MAX - 1),
        unroll_state(conv_all, 1, CONV_W - 1),
        h_all[:, 1],
        unroll_state(pool_all, 1, POOL_MAX - 1),
    )
```
